```python
import math
import jax, jax.numpy as jnp
from jax import lax
import numpy as np

D_MODEL = 2048
BATCH = 4
SEQ = 2048
DEPTH = 2

GRID_W = 64
CTX_LEN = 256
SSM_WIDTH = D_MODEL // 2
SSM_GROUP = 16
SSM_GROUPS = SSM_WIDTH // SSM_GROUP
SSM_STATE = 64
FFT_WIDTH = D_MODEL - SSM_WIDTH
FFT_GROUPS = 4
FFT_GROUP = FFT_WIDTH // FFT_GROUPS
N_BRANCH = 2
IN_WIDTH = SSM_WIDTH + FFT_WIDTH + N_BRANCH * D_MODEL
D_FF = 4 * D_MODEL
ALPHA = (2 * DEPTH) ** 0.25
BETA = (8 * DEPTH) ** -0.25
LN_EPS = 1e-5
DT_MIN = 1e-3
DT_MAX = 1e-1
POS_BASE = 10000.0

kernel_name = "hybrid_s5_fnet_dit_trunk"


def ln_plain(x):
    xf = x.astype(jnp.float32)
    mu = jnp.mean(xf, axis=-1, keepdims=True)
    xc = xf - mu
    var = jnp.mean(xc * xc, axis=-1, keepdims=True)
    return (xc * lax.rsqrt(var + LN_EPS)).astype(x.dtype)


def ln_affine(x, g, b):
    return ln_plain(x) * g + b


def adaln(cond, w_mod, b_mod):
    m = (jax.nn.silu(cond) @ w_mod + b_mod)[..., None, :]
    return jnp.split(m, 6, axis=-1)


def modulate(h, shift, scale):
    return h * (1.0 + scale) + shift


def grid_sincos(rows, dim):
    quarter = dim // 4
    omega = 1.0 / (POS_BASE ** (jnp.arange(quarter, dtype=jnp.float32) / quarter))
    row = jnp.repeat(jnp.arange(rows, dtype=jnp.float32), GRID_W)
    col = jnp.tile(jnp.arange(GRID_W, dtype=jnp.float32), rows)
    ar = row[:, None] * omega
    ac = col[:, None] * omega
    return jnp.concatenate([jnp.sin(ar), jnp.cos(ar), jnp.sin(ac), jnp.cos(ac)], axis=-1)


def zoh_discretise(lam_re, lam_im, log_step):
    dt = jnp.exp(log_step)[..., None]
    mag = jnp.exp(lam_re * dt)
    ang = lam_im * dt
    abar_re, abar_im = mag * jnp.cos(ang), mag * jnp.sin(ang)
    den = lam_re * lam_re + lam_im * lam_im
    nr, ni = abar_re - 1.0, abar_im
    coef_re = (nr * lam_re + ni * lam_im) / den
    coef_im = (ni * lam_re - nr * lam_im) / den
    return abar_re, abar_im, coef_re, coef_im


def ssm_drive(u, b_re, b_im, coef_re, coef_im):
    bu_re = jnp.einsum('blgc,gnc->blgn', u, b_re)
    bu_im = jnp.einsum('blgc,gnc->blgn', u, b_im)
    return coef_re * bu_re - coef_im * bu_im, coef_re * bu_im + coef_im * bu_re


def complex_linear_scan(abar_re, abar_im, v_re, v_im, h0=None):
    if h0 is not None:
        h0_re, h0_im = h0
        v_re = v_re.at[:, 0].add(abar_re * h0_re - abar_im * h0_im)
        v_im = v_im.at[:, 0].add(abar_re * h0_im + abar_im * h0_re)
    a_re = jnp.broadcast_to(abar_re, v_re.shape)
    a_im = jnp.broadcast_to(abar_im, v_im.shape)

    def combine(e1, e2):
        a1r, a1i, b1r, b1i = e1
        a2r, a2i, b2r, b2i = e2
        return (a2r * a1r - a2i * a1i, a2r * a1i + a2i * a1r,
                a2r * b1r - a2i * b1i + b2r, a2r * b1i + a2i * b1r + b2i)

    _, _, h_re, h_im = lax.associative_scan(combine, (a_re, a_im, v_re, v_im), axis=1)
    return h_re, h_im


def ssm_readout(h_re, h_im, c_re, c_im):
    return jnp.einsum('blgn,gcn->blgc', h_re, c_re) - jnp.einsum('blgn,gcn->blgc', h_im, c_im)


def _flip(t, rev):
    return t[:, ::-1] if rev else t


def bidirectional_s5(u_ctx, u_lat, need_ctx, lam_re, lam_im, log_step, b_re, b_im, c_re, c_im, d_skip):
    f32 = jnp.float32
    dtype = u_lat.dtype
    bsz, n_lat, _ = u_lat.shape
    n_ctx = u_ctx.shape[1]
    uc = u_ctx.astype(f32).reshape(bsz, n_ctx, SSM_GROUPS, SSM_GROUP)
    ul = u_lat.astype(f32).reshape(bsz, n_lat, SSM_GROUPS, SSM_GROUP)
    abar_re, abar_im, coef_re, coef_im = zoh_discretise(lam_re.astype(f32), lam_im.astype(f32), log_step.astype(f32))
    dsk = d_skip.astype(f32)
    y_lat = dsk * u_lat.astype(f32)
    y_ctx = dsk * u_ctx.astype(f32) if need_ctx else None
    for d in range(2):
        rev = d == 1
        br, bi = b_re[d].astype(f32), b_im[d].astype(f32)
        cr, ci = c_re[d].astype(f32), c_im[d].astype(f32)
        hc = complex_linear_scan(abar_re[d], abar_im[d], *ssm_drive(_flip(uc, rev), br, bi, coef_re[d], coef_im[d]))
        hl = complex_linear_scan(abar_re[d], abar_im[d], *ssm_drive(_flip(ul, rev), br, bi, coef_re[d], coef_im[d]),
                                 h0=(hc[0][:, -1], hc[1][:, -1]))
        y_lat = y_lat + _flip(ssm_readout(hl[0], hl[1], cr, ci), rev).reshape(bsz, n_lat, SSM_WIDTH)
        if need_ctx:
            y_ctx = y_ctx + _flip(ssm_readout(hc[0], hc[1], cr, ci), rev).reshape(bsz, n_ctx, SSM_WIDTH)
    return y_lat.astype(dtype), (y_ctx.astype(dtype) if need_ctx else None)


def s5_glu(y, w_glu):
    g = jax.nn.gelu(y)
    return g * jax.nn.sigmoid(g @ w_glu)


def fourier_mix(u):
    bsz, n, _ = u.shape
    g = u.astype(jnp.float32).reshape(bsz, n, FFT_GROUPS, FFT_GROUP)
    f = jnp.fft.fft2(g, axes=(1, 3), norm="ortho").real
    return f.reshape(bsz, n, FFT_WIDTH).astype(u.dtype)


def branch_merge(y_s, y_f, gate_s, gate_f, w_ps, w_pf, w_o):
    merged = jax.nn.sigmoid(gate_s) * (y_s @ w_ps) + jax.nn.sigmoid(gate_f) * (y_f @ w_pf)
    return merged @ w_o


def hybrid_mixer(h_lat, h_ctx, need_ctx, w_in, lam_re, lam_im, log_step, b_re, b_im, c_re, c_im, d_skip,
                 w_glu, w_ps, w_pf, w_o):
    cuts = (SSM_WIDTH, SSM_WIDTH + FFT_WIDTH, SSM_WIDTH + FFT_WIDTH + D_MODEL)
    us_lat, uf_lat, gs_lat, gf_lat = jnp.split(h_lat @ w_in, cuts, axis=-1)
    if need_ctx:
        us_ctx, uf_ctx, gs_ctx, gf_ctx = jnp.split(h_ctx @ w_in, cuts, axis=-1)
    else:
        us_ctx = h_ctx @ w_in[:, :SSM_WIDTH]
    ys_lat, ys_ctx = bidirectional_s5(us_ctx, us_lat, need_ctx, lam_re, lam_im, log_step, b_re, b_im, c_re, c_im, d_skip)
    out_lat = branch_merge(s5_glu(ys_lat, w_glu), fourier_mix(uf_lat), gs_lat, gf_lat, w_ps, w_pf, w_o)
    if not need_ctx:
        return out_lat, None
    out_ctx = branch_merge(s5_glu(ys_ctx, w_glu), fourier_mix(uf_ctx), gs_ctx, gf_ctx, w_ps, w_pf, w_o)
    return out_lat, out_ctx


def sq_relu_mlp(h, w_up, w_down):
    a = jax.nn.relu(h @ w_up)
    return (a * a) @ w_down


def setup_inputs(seed: int = 0) -> dict:
    key = jax.random.key(seed)
    ks = jax.random.split(key, 32)
    L, G, N, C = DEPTH, SSM_GROUPS, SSM_STATE, SSM_GROUP
    f32 = jnp.float32

    def nrm(k, shape, s):
        return jax.random.normal(k, shape, f32) * s

    n_idx = jnp.arange(N, dtype=f32)
    return {
        "x": nrm(ks[0], (BATCH, SEQ, D_MODEL), 1.0),
        "c": nrm(ks[1], (BATCH, D_MODEL), 1.0),
        "ctx": nrm(ks[2], (BATCH, CTX_LEN, D_MODEL), 1.0),
        "c_ctx": nrm(ks[3], (D_MODEL,), 1.0),
        "w_mod": nrm(ks[4], (L, D_MODEL, 6 * D_MODEL), D_MODEL ** -0.5),
        "b_mod": nrm(ks[5], (L, 6 * D_MODEL), 0.02),
        "w_in": nrm(ks[6], (L, D_MODEL, IN_WIDTH), D_MODEL ** -0.5),
        "lam_re": -0.5 + nrm(ks[7], (L, 2, G, N), 0.01),
        "lam_im": math.pi * n_idx + nrm(ks[8], (L, 2, G, N), 0.01),
        "log_step": jax.random.uniform(ks[9], (L, 2, G), f32, math.log(DT_MIN), math.log(DT_MAX)),
        "ssm_b_re": nrm(ks[10], (L, 2, G, N, C), (2 * C) ** -0.5),
        "ssm_b_im": nrm(ks[11], (L, 2, G, N, C), (2 * C) ** -0.5),
        "ssm_c_re": nrm(ks[12], (L, 2, G, C, N), N ** -0.5),
        "ssm_c_im": nrm(ks[13], (L, 2, G, C, N), N ** -0.5),
        "d_skip": nrm(ks[14], (L, SSM_WIDTH), 1.0),
        "w_glu": nrm(ks[15], (L, SSM_WIDTH, SSM_WIDTH), SSM_WIDTH ** -0.5),
        "w_ps": nrm(ks[16], (L, SSM_WIDTH, D_MODEL), SSM_WIDTH ** -0.5),
        "w_pf": nrm(ks[17], (L, FFT_WIDTH, D_MODEL), FFT_WIDTH ** -0.5),
        "w_o": nrm(ks[18], (L, D_MODEL, D_MODEL), BETA * D_MODEL ** -0.5),
        "ln1_g": 1.0 + nrm(ks[19], (L, D_MODEL), 0.02),
        "ln1_b": nrm(ks[20], (L, D_MODEL), 0.02),
        "w_up": nrm(ks[21], (L, D_MODEL, D_FF), D_MODEL ** -0.5),
        "w_down": nrm(ks[22], (L, D_FF, D_MODEL), BETA * D_FF ** -0.5),
        "ln2_g": 1.0 + nrm(ks[23], (L, D_MODEL), 0.02),
        "ln2_b": nrm(ks[24], (L, D_MODEL), 0.02),
    }


def reference(x, c, ctx, c_ctx, w_mod, b_mod, w_in, lam_re, lam_im, log_step, ssm_b_re, ssm_b_im,
              ssm_c_re, ssm_c_im, d_skip, w_glu, w_ps, w_pf, w_o, ln1_g, ln1_b, w_up, w_down, ln2_g, ln2_b):
    n_lat = x.shape[1]
    rows = n_lat // GRID_W
    x = x + grid_sincos(rows, D_MODEL).astype(x.dtype)[None]
    x_ctx = ctx
    for l in range(DEPTH):
        need_ctx = l < DEPTH - 1
        m_lat = adaln(c, w_mod[l], b_mod[l])
        m_ctx = adaln(c_ctx[None], w_mod[l], b_mod[l])
        h_lat = modulate(ln_plain(x), m_lat[0], m_lat[1])
        h_ctx = modulate(ln_plain(x_ctx), m_ctx[0], m_ctx[1])
        mix_lat, mix_ctx = hybrid_mixer(h_lat, h_ctx, need_ctx, w_in[l], lam_re[l], lam_im[l], log_step[l],
                                        ssm_b_re[l], ssm_b_im[l], ssm_c_re[l], ssm_c_im[l], d_skip[l],
                                        w_glu[l], w_ps[l], w_pf[l], w_o[l])
        x = ln_affine(ALPHA * x + m_lat[2] * mix_lat, ln1_g[l], ln1_b[l])
        h2 = modulate(ln_plain(x), m_lat[3], m_lat[4])
        x = ln_affine(ALPHA * x + m_lat[5] * sq_relu_mlp(h2, w_up[l], w_down[l]), ln2_g[l], ln2_b[l])
        if need_ctx:
            x_ctx = ln_affine(ALPHA * x_ctx + m_ctx[2] * mix_ctx, ln1_g[l], ln1_b[l])
            h2c = modulate(ln_plain(x_ctx), m_ctx[3], m_ctx[4])
            x_ctx = ln_affine(ALPHA * x_ctx + m_ctx[5] * sq_relu_mlp(h2c, w_up[l], w_down[l]), ln2_g[l], ln2_b[l])
    return x
```

```python
import functools
import math

import numpy as np
import jax
import jax.numpy as jnp
from jax import lax
from jax.experimental import pallas as pl
from jax.experimental.pallas import tpu as pltpu

D_MODEL = 2048
BATCH = 4
SEQ = 2048
DEPTH = 2
GRID_W = 64
CTX_LEN = 256
SSM_WIDTH = D_MODEL // 2
SSM_GROUP = 16
SSM_GROUPS = SSM_WIDTH // SSM_GROUP
SSM_STATE = 64
FFT_WIDTH = D_MODEL - SSM_WIDTH
FFT_GROUPS = 4
FFT_GROUP = FFT_WIDTH // FFT_GROUPS
IN_WIDTH = SSM_WIDTH + FFT_WIDTH + 2 * D_MODEL
D_FF = 4 * D_MODEL
ALPHA = (2 * DEPTH) ** 0.25
LN_EPS = 1e-5
POS_BASE = 10000.0

F32 = jnp.float32
BF16 = jnp.bfloat16

SUBLANES = 8
LANES = 128
STATE_W = SSM_GROUPS * SSM_STATE
SCAN_T = 32
S5_KGROUPS = 8
S5_NK = SSM_GROUPS // S5_KGROUPS
S5_KSTATE = S5_KGROUPS * SSM_STATE
SCAN_COLS = 512
MIB = 1024 * 1024


def _params(sem, vmem_mib):
    return pltpu.CompilerParams(dimension_semantics=sem, vmem_limit_bytes=vmem_mib * MIB)


def _dot(a, b):
    return jnp.dot(a, b, preferred_element_type=F32)


def _ln(x):
    mu = jnp.mean(x, axis=-1, keepdims=True)
    xc = x - mu
    var = jnp.mean(xc * xc, axis=-1, keepdims=True)
    return xc * lax.rsqrt(var + LN_EPS)


def _rowpat(x, fn):
    tm, d = x.shape
    return fn(x.reshape(tm // SUBLANES, SUBLANES, d)).reshape(tm, d)


def _mod_kernel(c_ref, w_ref, b_ref, o_ref):
    c = c_ref[...]
    sc = c * jax.nn.sigmoid(c)
    o_ref[...] = _dot(sc.astype(BF16), w_ref[...].astype(BF16)) + b_ref[...]


def _modulation(cond8, w_mod, b_mod):
    tn = 1024
    n = 6 * D_MODEL
    return pl.pallas_call(
        _mod_kernel,
        grid=(DEPTH, n // tn),
        in_specs=[
            pl.BlockSpec((SUBLANES, D_MODEL), lambda l, j: (0, 0)),
            pl.BlockSpec((None, D_MODEL, tn), lambda l, j: (l, 0, j)),
            pl.BlockSpec((None, 1, tn), lambda l, j: (l, 0, j)),
        ],
        out_specs=pl.BlockSpec((None, SUBLANES, tn), lambda l, j: (l, 0, j)),
        out_shape=jax.ShapeDtypeStruct((DEPTH, SUBLANES, n), F32),
        compiler_params=_params(("parallel", "parallel"), 40),
        name="adaln_modulation",
    )(cond8, w_mod, b_mod.reshape(DEPTH, 1, n))


def _pos_kernel(x_ref, p_ref, o_ref):
    o_ref[...] = x_ref[...] + p_ref[...]


def _add_pos_time_major(x, pos):
    tl = 512
    out = pl.pallas_call(
        _pos_kernel,
        grid=(SEQ // tl, BATCH),
        in_specs=[
            pl.BlockSpec((None, tl, D_MODEL), lambda i, b: (b, i, 0)),
            pl.BlockSpec((tl, D_MODEL), lambda i, b: (i, 0)),
        ],
        out_specs=pl.BlockSpec((tl, D_MODEL), lambda i, b: (i, b)),
        out_shape=jax.ShapeDtypeStruct((SEQ, BATCH * D_MODEL), F32),
        compiler_params=_params(("parallel", "parallel"), 40),
        name="add_pos",
    )(x, pos)
    return out.reshape(SEQ * BATCH, D_MODEL)


def _win_kernel(x_ref, mod_ref, w_ref, o_ref, h_ref):
    @pl.when(pl.program_id(1) == 0)
    def _():
        h = _rowpat(_ln(x_ref[...]), lambda v: v * (1.0 + mod_ref[1][None]) + mod_ref[0][None])
        h_ref[...] = h.astype(BF16)

    o_ref[...] = _dot(h_ref[...], w_ref[...].astype(BF16))


def _in_proj(x, mod, w, n_out):
    rows = x.shape[0]
    tm, tn = 1024, 512
    return pl.pallas_call(
        _win_kernel,
        grid=(rows // tm, n_out // tn),
        in_specs=[
            pl.BlockSpec((tm, D_MODEL), lambda i, j: (i, 0)),
            pl.BlockSpec((6, SUBLANES, D_MODEL), lambda i, j: (0, 0, 0)),
            pl.BlockSpec((D_MODEL, tn), lambda i, j: (0, j)),
        ],
        out_specs=pl.BlockSpec((tm, tn), lambda i, j: (i, j)),
        out_shape=jax.ShapeDtypeStruct((rows, n_out), F32),
        scratch_shapes=[pltpu.VMEM((tm, D_MODEL), BF16)],
        compiler_params=_params(("parallel", "arbitrary"), 48),
        name="in_proj",
    )(x, mod, w)


def _s5_kernel(need_y, uf_ref, ub_ref, h0_ref, pf_ref, pb_ref, pft_ref, pbt_ref, a_ref, wd_ref, wr_ref,
               *rest):
    if need_y:
        yf_ref, yb_ref, ht_ref, vre_ref, vim_ref, hc_ref, ycf_ref, ycb_ref = rest
    else:
        ht_ref, vre_ref, vim_ref, hc_ref = rest
    i = pl.program_id(0)

    @pl.when(i == 0)
    def _():
        hc_ref[...] = h0_ref[...]

    up_f = _dot(pf_ref[...], uf_ref[...].astype(BF16)).astype(BF16)
    up_b = _dot(pb_ref[...], ub_ref[...].astype(BF16)).astype(BF16)
    for k in range(S5_NK):
        cs = slice(k * LANES, (k + 1) * LANES)
        lhs = jnp.concatenate([up_f[:, cs], up_b[:, cs]], axis=1)
        v = _dot(lhs, wd_ref[k])
        ss = slice(k * S5_KSTATE, (k + 1) * S5_KSTATE)
        vre_ref[:, ss] = v[:, :S5_KSTATE]
        vim_ref[:, ss] = v[:, S5_KSTATE:]

    for cg in range(STATE_W // SCAN_COLS):
        cs = slice(cg * SCAN_COLS, (cg + 1) * SCAN_COLS)
        ar = a_ref[0, :, cs]
        ai = a_ref[1, :, cs]

        def body(s, carry, cs=cs, ar=ar, ai=ai):
            hr, hi = carry
            rs = pl.ds(pl.multiple_of(s * SUBLANES, SUBLANES), SUBLANES)
            nr = ar * hr - ai * hi + vre_ref[rs, cs]
            ni = ar * hi + ai * hr + vim_ref[rs, cs]
            vre_ref[rs, cs] = nr
            vim_ref[rs, cs] = ni
            return nr, ni

        hr, hi = lax.fori_loop(0, SCAN_T, body, (hc_ref[0, :, cs], hc_ref[1, :, cs]), unroll=2)
        hc_ref[0, :, cs] = hr
        hc_ref[1, :, cs] = hi

    if need_y:
        for k in range(S5_NK):
            ss = slice(k * S5_KSTATE, (k + 1) * S5_KSTATE)
            y = (_dot(vre_ref[:, ss].astype(BF16), wr_ref[k, 0])
                 + _dot(vim_ref[:, ss].astype(BF16), wr_ref[k, 1]))
            cs = slice(k * LANES, (k + 1) * LANES)
            ycf_ref[:, cs] = y[:, :LANES]
            ycb_ref[:, cs] = y[:, LANES:]
        yf_ref[...] = _dot(pft_ref[...], ycf_ref[...].astype(BF16))
        yb_ref[...] = _dot(pbt_ref[...], ycb_ref[...].astype(BF16))

    @pl.when(i == pl.num_programs(0) - 1)
    def _():
        ht_ref[...] = hc_ref[...]


def _scan_perms():
    t = SCAN_T
    pf = np.zeros((2 * BATCH * t, BATCH * t), np.float32)
    pb = np.zeros((2 * BATCH * t, BATCH * t), np.float32)
    for s in range(t):
        for b in range(BATCH):
            pf[s * 2 * BATCH + b, s * BATCH + b] = 1.0
            pb[s * 2 * BATCH + BATCH + b, (t - 1 - s) * BATCH + b] = 1.0
    return pf, pb


_PF, _PB = _scan_perms()


def _s5(u, h0, prm, need_y):
    a, wd, wr = prm
    steps = u.shape[0] // BATCH
    n = steps // SCAN_T
    rows_in = BATCH * SCAN_T
    rows_sc = 2 * rows_in
    pf = jnp.asarray(_PF, BF16)
    pb = jnp.asarray(_PB, BF16)
    const2 = lambda i: (0, 0)
    in_specs = [
        pl.BlockSpec((rows_in, SSM_WIDTH), lambda i: (i, 0)),
        pl.BlockSpec((rows_in, SSM_WIDTH), lambda i: (n - 1 - i, 0)),
        pl.BlockSpec((2, SUBLANES, STATE_W), lambda i: (0, 0, 0)),
        pl.BlockSpec((rows_sc, rows_in), const2),
        pl.BlockSpec((rows_sc, rows_in), const2),
        pl.BlockSpec((rows_in, rows_sc), const2),
        pl.BlockSpec((rows_in, rows_sc), const2),
        pl.BlockSpec((2, SUBLANES, STATE_W), lambda i: (0, 0, 0)),
        pl.BlockSpec((S5_NK, 2 * LANES, 2 * S5_KSTATE), lambda i: (0, 0, 0)),
        pl.BlockSpec((S5_NK, 2, S5_KSTATE, 2 * LANES), lambda i: (0, 0, 0, 0)),
    ]
    state_spec = pl.BlockSpec((2, SUBLANES, STATE_W), lambda i: (0, 0, 0))
    state_shape = jax.ShapeDtypeStruct((2, SUBLANES, STATE_W), F32)
    scratch = [pltpu.VMEM((rows_sc, STATE_W), F32), pltpu.VMEM((rows_sc, STATE_W), F32),
               pltpu.VMEM((2, SUBLANES, STATE_W), F32)]
    if need_y:
        out_specs = [pl.BlockSpec((rows_in, SSM_WIDTH), lambda i: (i, 0)),
                     pl.BlockSpec((rows_in, SSM_WIDTH), lambda i: (n - 1 - i, 0)),
                     state_spec]
        y_shape = jax.ShapeDtypeStruct((steps * BATCH, SSM_WIDTH), F32)
        out_shape = [y_shape, y_shape, state_shape]
        scratch += [pltpu.VMEM((rows_sc, SSM_WIDTH), F32), pltpu.VMEM((rows_sc, SSM_WIDTH), F32)]
    else:
        out_specs = [state_spec]
        out_shape = [state_shape]
    return pl.pallas_call(
        functools.partial(_s5_kernel, need_y),
        grid=(n,),
        in_specs=in_specs,
        out_specs=out_specs,
        out_shape=out_shape,
        scratch_shapes=scratch,
        compiler_params=_params(("arbitrary",), 48),
        name="s5_scan",
    )(u, u, h0, pf, pb, pf.T, pb.T, a, wd, wr)


def _s5_params(lam_re, lam_im, log_step, b_re, b_im, c_re, c_im):
    dt = jnp.exp(log_step)[..., None]
    mag = jnp.exp(lam_re * dt)
    ang = lam_im * dt
    abar_re, abar_im = mag * jnp.cos(ang), mag * jnp.sin(ang)
    den = lam_re * lam_re + lam_im * lam_im
    nr, ni = abar_re - 1.0, abar_im
    coef_re = (nr * lam_re + ni * lam_im) / den
    coef_im = (ni * lam_re - nr * lam_im) / den
    bb_re = coef_re[..., None] * b_re - coef_im[..., None] * b_im
    bb_im = coef_re[..., None] * b_im + coef_im[..., None] * b_re
    eye = jnp.eye(S5_KGROUPS, dtype=F32)
    shp = (2, S5_NK, S5_KGROUPS, SSM_STATE, SSM_GROUP)
    bb = jnp.stack([bb_re.reshape(shp), bb_im.reshape(shp)])
    wd = jnp.einsum('pdkgnc,gh->kdgcphn', bb, eye)
    wd = wd.reshape(S5_NK, 2 * LANES, 2 * S5_KSTATE).astype(BF16)
    shp = (2, S5_NK, S5_KGROUPS, SSM_GROUP, SSM_STATE)
    cc = jnp.stack([c_re.reshape(shp), -c_im.reshape(shp)])
    wr = jnp.einsum('pdkgcn,gh->kpgndhc', cc, eye)
    wr = wr.reshape(S5_NK, 2, S5_KSTATE, 2 * LANES).astype(BF16)
    a = jnp.stack([abar_re.reshape(2, STATE_W), abar_im.reshape(2, STATE_W)])
    a = jnp.repeat(a, BATCH, axis=1)
    return a, wd, wr


def _dft_tables(n):
    j = np.arange(n, dtype=np.int64)
    ang = 2.0 * np.pi * ((j[:, None] * j[None, :]) % n).astype(np.float64) / n
    s = 1.0 / math.sqrt(n)
    return (np.cos(ang) * s).astype(np.float32), (np.sin(ang) * s).astype(np.float32)


_CH_COS, _CH_SIN = _dft_tables(FFT_GROUP)
_CH_CS = np.concatenate([_CH_COS, _CH_SIN], axis=1)
_POS_TABLES = {n: _dft_tables(n) for n in (CTX_LEN, SEQ)}


def _chdft_kernel(u_ref, w_ref, p_ref, q_ref):
    r = _dot(u_ref[...].astype(BF16), w_ref[...].astype(BF16))
    p_ref[...] = r[:, :FFT_GROUP].astype(BF16)
    q_ref[...] = r[:, FFT_GROUP:].astype(BF16)


def _channel_dft(u):
    rows = u.shape[0]
    tm = 1024
    off = SSM_WIDTH // FFT_GROUP
    shape = jax.ShapeDtypeStruct((rows, FFT_WIDTH), BF16)
    return pl.pallas_call(
        _chdft_kernel,
        grid=(rows // tm, FFT_GROUPS),
        in_specs=[
            pl.BlockSpec((tm, FFT_GROUP), lambda i, g: (i, off + g)),
            pl.BlockSpec((FFT_GROUP, 2 * FFT_GROUP), lambda i, g: (0, 0)),
        ],
        out_specs=[pl.BlockSpec((tm, FFT_GROUP), lambda i, g: (i, g)),
                   pl.BlockSpec((tm, FFT_GROUP), lambda i, g: (i, g))],
        out_shape=[shape, shape],
        compiler_params=_params(("parallel", "parallel"), 32),
        name="channel_dft",
    )(u, jnp.asarray(_CH_CS))


def _posdft_kernel(nk, ac_ref, as_ref, p_ref, q_ref, o_ref, acc_ref):
    k = pl.program_id(2)

    @pl.when(k == 0)
    def _():
        acc_ref[...] = jnp.zeros_like(acc_ref)

    acc_ref[...] += (_dot(ac_ref[...].astype(BF16), p_ref[...])
                     - _dot(as_ref[...].astype(BF16), q_ref[...]))

    @pl.when(k == nk - 1)
    def _():
        o_ref[...] = acc_ref[...].astype(BF16)


def _position_dft(p, q):
    rows = p.shape[0]
    steps = rows // BATCH
    cols = BATCH * FFT_WIDTH
    cos, sin = _POS_TABLES[steps]
    tm = min(steps, 1024)
    tk = min(steps, 512)
    tn = 1024
    nk = steps // tk
    out = pl.pallas_call(
        functools.partial(_posdft_kernel, nk),
        grid=(steps // tm, cols // tn, nk),
        in_specs=[
            pl.BlockSpec((tm, tk), lambda i, j, k: (i, k)),
            pl.BlockSpec((tm, tk), lambda i, j, k: (i, k)),
            pl.BlockSpec((tk, tn), lambda i, j, k: (k, j)),
            pl.BlockSpec((tk, tn), lambda i, j, k: (k, j)),
        ],
        out_specs=pl.BlockSpec((tm, tn), lambda i, j, k: (i, j)),
        out_shape=jax.ShapeDtypeStruct((steps, cols), BF16),
        scratch_shapes=[pltpu.VMEM((tm, tn), F32)],
        compiler_params=_params(("parallel", "parallel", "arbitrary"), 40),
        name="position_dft",
    )(jnp.asarray(cos), jnp.asarray(sin), p.reshape(steps, cols), q.reshape(steps, cols))
    return out.reshape(rows, FFT_WIDTH)


def _merge_kernel(us_ref, yf_ref, yb_ref, dsk_ref, wglu_ref, yq_ref, gs_ref, gf_ref, wps_ref, wpf_ref,
                  o_ref, s_ref):
    @pl.when(pl.program_id(1) == 0)
    def _():
        ys = dsk_ref[...] * us_ref[...] + yf_ref[...] + yb_ref[...]
        g = jax.nn.gelu(ys)
        z = _dot(g.astype(BF16), wglu_ref[...].astype(BF16))
        s_ref[...] = (g * jax.nn.sigmoid(z)).astype(BF16)

    ps = _dot(s_ref[...], wps_ref[...].astype(BF16))
    pf = _dot(yq_ref[...], wpf_ref[...].astype(BF16))
    o_ref[...] = (jax.nn.sigmoid(gs_ref[...]) * ps + jax.nn.sigmoid(gf_ref[...]) * pf).astype(BF16)


def _merge(u, yf, yb, yq, d_skip, w_glu, w_ps, w_pf):
    rows = u.shape[0]
    tm, tn = 512, 512
    gs_off = (SSM_WIDTH + FFT_WIDTH) // tn
    gf_off = (SSM_WIDTH + FFT_WIDTH + D_MODEL) // tn
    row_blk = lambda i, j: (i, 0)
    return pl.pallas_call(
        _merge_kernel,
        grid=(rows // tm, D_MODEL // tn),
        in_specs=[
            pl.BlockSpec((tm, SSM_WIDTH), row_blk),
            pl.BlockSpec((tm, SSM_WIDTH), row_blk),
            pl.BlockSpec((tm, SSM_WIDTH), row_blk),
            pl.BlockSpec((1, SSM_WIDTH), lambda i, j: (0, 0)),
            pl.BlockSpec((SSM_WIDTH, SSM_WIDTH), lambda i, j: (0, 0)),
            pl.BlockSpec((tm, FFT_WIDTH), row_blk),
            pl.BlockSpec((tm, tn), lambda i, j: (i, gs_off + j)),
            pl.BlockSpec((tm, tn), lambda i, j: (i, gf_off + j)),
            pl.BlockSpec((SSM_WIDTH, tn), lambda i, j: (0, j)),
            pl.BlockSpec((FFT_WIDTH, tn), lambda i, j: (0, j)),
        ],
        out_specs=pl.BlockSpec((tm, tn), lambda i, j: (i, j)),
        out_shape=jax.ShapeDtypeStruct((rows, D_MODEL), BF16),
        scratch_shapes=[pltpu.VMEM((tm, SSM_WIDTH), BF16)],
        compiler_params=_params(("parallel", "arbitrary"), 48),
        name="glu_merge",
    )(u, yf, yb, d_skip.reshape(1, SSM_WIDTH), w_glu, yq, u, u, w_ps, w_pf)


def _wo_kernel(nk, mg_ref, wo_ref, x_ref, mod_ref, g_ref, b_ref, o_ref, acc_ref):
    k = pl.program_id(1)

    @pl.when(k == 0)
    def _():
        acc_ref[...] = jnp.zeros_like(acc_ref)

    acc_ref[...] += _dot(mg_ref[...], wo_ref[...].astype(BF16))

    @pl.when(k == nk - 1)
    def _():
        y = ALPHA * x_ref[...] + _rowpat(acc_ref[...], lambda v: v * mod_ref[2][None])
        o_ref[...] = _ln(y) * g_ref[...] + b_ref[...]


def _out_proj_ln(merged, w_o, x, mod, g, b):
    rows = x.shape[0]
    tm, tk = 512, 512
    nk = D_MODEL // tk
    vec = lambda i, k: (0, 0)
    return pl.pallas_call(
        functools.partial(_wo_kernel, nk),
        grid=(rows // tm, nk),
        in_specs=[
            pl.BlockSpec((tm, tk), lambda i, k: (i, k)),
            pl.BlockSpec((tk, D_MODEL), lambda i, k: (k, 0)),
            pl.BlockSpec((tm, D_MODEL), lambda i, k: (i, 0)),
            pl.BlockSpec((6, SUBLANES, D_MODEL), lambda i, k: (0, 0, 0)),
            pl.BlockSpec((1, D_MODEL), vec),
            pl.BlockSpec((1, D_MODEL), vec),
        ],
        out_specs=pl.BlockSpec((tm, D_MODEL), lambda i, k: (i, 0)),
        out_shape=jax.ShapeDtypeStruct((rows, D_MODEL), F32),
        scratch_shapes=[pltpu.VMEM((tm, D_MODEL), F32)],
        compiler_params=_params(("parallel", "arbitrary"), 48),
        name="out_proj_ln",
    )(merged, w_o, x, mod, g.reshape(1, D_MODEL), b.reshape(1, D_MODEL))


def _mlp_kernel(nf, x_ref, mod_ref, wu_ref, wd_ref, g_ref, b_ref, o_ref, h_ref, acc_ref):
    f = pl.program_id(1)

    @pl.when(f == 0)
    def _():
        h = _rowpat(_ln(x_ref[...]), lambda v: v * (1.0 + mod_ref[4][None]) + mod_ref[3][None])
        h_ref[...] = h.astype(BF16)
        acc_ref[...] = jnp.zeros_like(acc_ref)

    a = jnp.maximum(_dot(h_ref[...], wu_ref[...].astype(BF16)), 0.0)
    acc_ref[...] += _dot((a * a).astype(BF16), wd_ref[...].astype(BF16))

    @pl.when(f == nf - 1)
    def _():
        y = ALPHA * x_ref[...] + _rowpat(acc_ref[...], lambda v: v * mod_ref[5][None])
        o_ref[...] = _ln(y) * g_ref[...] + b_ref[...]


def _mlp(x, mod, w_up, w_down, g, b):
    rows = x.shape[0]
    tm, tf = 512, 512
    nf = D_FF // tf
    vec = lambda i, f: (0, 0)
    return pl.pallas_call(
        functools.partial(_mlp_kernel, nf),
        grid=(rows // tm, nf),
        in_specs=[
            pl.BlockSpec((tm, D_MODEL), lambda i, f: (i, 0)),
            pl.BlockSpec((6, SUBLANES, D_MODEL), lambda i, f: (0, 0, 0)),
            pl.BlockSpec((D_MODEL, tf), lambda i, f: (0, f)),
            pl.BlockSpec((tf, D_MODEL), lambda i, f: (f, 0)),
            pl.BlockSpec((1, D_MODEL), vec),
            pl.BlockSpec((1, D_MODEL), vec),
        ],
        out_specs=pl.BlockSpec((tm, D_MODEL), lambda i, f: (i, 0)),
        out_shape=jax.ShapeDtypeStruct((rows, D_MODEL), F32),
        scratch_shapes=[pltpu.VMEM((tm, D_MODEL), BF16), pltpu.VMEM((tm, D_MODEL), F32)],
        compiler_params=_params(("parallel", "arbitrary"), 48),
        name="mlp",
    )(x, mod, w_up, w_down, g.reshape(1, D_MODEL), b.reshape(1, D_MODEL))


def _pos_table():
    quarter = D_MODEL // 4
    omega = 1.0 / (POS_BASE ** (np.arange(quarter, dtype=np.float64) / quarter))
    t = np.arange(SEQ)
    ar = (t // GRID_W).astype(np.float64)[:, None] * omega
    ac = (t % GRID_W).astype(np.float64)[:, None] * omega
    return np.concatenate([np.sin(ar), np.cos(ar), np.sin(ac), np.cos(ac)], axis=-1).astype(np.float32)


_POS = _pos_table()


def _mod_patterns(m):
    m = m.reshape(DEPTH, SUBLANES, 6, D_MODEL)
    lat = jnp.transpose(jnp.concatenate([m[:, :BATCH], m[:, :BATCH]], axis=1), (0, 2, 1, 3))
    ctx = jnp.broadcast_to(m[:, BATCH][:, :, None, :], (DEPTH, 6, SUBLANES, D_MODEL))
    return lat, ctx


def kernel(x, c, ctx, c_ctx, w_mod, b_mod, w_in, lam_re, lam_im, log_step, ssm_b_re, ssm_b_im, ssm_c_re,
           ssm_c_im, d_skip, w_glu, w_ps, w_pf, w_o, ln1_g, ln1_b, w_up, w_down, ln2_g, ln2_b):
    cond8 = jnp.concatenate([c, c_ctx[None], jnp.zeros((SUBLANES - BATCH - 1, D_MODEL), F32)], axis=0)
    mod_lat, mod_ctx = _mod_patterns(_modulation(cond8, w_mod, b_mod))

    x_lat = _add_pos_time_major(x, jnp.asarray(_POS))
    x_ctx = jnp.transpose(ctx, (1, 0, 2)).reshape(CTX_LEN * BATCH, D_MODEL)
    h_zero = jnp.zeros((2, SUBLANES, STATE_W), F32)

    for l in range(DEPTH):
        need_ctx = l < DEPTH - 1
        prm = _s5_params(lam_re[l], lam_im[l], log_step[l], ssm_b_re[l], ssm_b_im[l], ssm_c_re[l],
                         ssm_c_im[l])

        def mixer_tail(u, yf, yb, xx, mod):
            p, q = _channel_dft(u)
            yq = _position_dft(p, q)
            merged = _merge(u, yf, yb, yq, d_skip[l], w_glu[l], w_ps[l], w_pf[l])
            x1 = _out_proj_ln(merged, w_o[l], xx, mod, ln1_g[l], ln1_b[l])
            return _mlp(x1, mod, w_up[l], w_down[l], ln2_g[l], ln2_b[l])

        u_lat = _in_proj(x_lat, mod_lat[l], w_in[l], IN_WIDTH)
        u_ctx = _in_proj(x_ctx, mod_ctx[l], w_in[l], IN_WIDTH if need_ctx else SSM_WIDTH)
        if need_ctx:
            yf_c, yb_c, h_t = _s5(u_ctx, h_zero, prm, True)
        else:
            (h_t,) = _s5(u_ctx, h_zero, prm, False)
        yf, yb, _ = _s5(u_lat, h_t, prm, True)
        x_lat = mixer_tail(u_lat, yf, yb, x_lat, mod_lat[l])
        if need_ctx:
            x_ctx = mixer_tail(u_ctx, yf_c, yb_c, x_ctx, mod_ctx[l])

    return jnp.transpose(x_lat.reshape(SEQ, BATCH, D_MODEL), (1, 0, 2))
```

```python
import functools
import math

import numpy as np
import jax
import jax.numpy as jnp
from jax import lax
from jax.experimental import pallas as pl
from jax.experimental.pallas import tpu as pltpu

D_MODEL = 2048
BATCH = 4
SEQ = 2048
DEPTH = 2
GRID_W = 64
CTX_LEN = 256
SSM_WIDTH = D_MODEL // 2
SSM_GROUP = 16
SSM_GROUPS = SSM_WIDTH // SSM_GROUP
SSM_STATE = 64
FFT_WIDTH = D_MODEL - SSM_WIDTH
FFT_GROUPS = 4
FFT_GROUP = FFT_WIDTH // FFT_GROUPS
IN_WIDTH = SSM_WIDTH + FFT_WIDTH + 2 * D_MODEL
REST_WIDTH = IN_WIDTH - SSM_WIDTH
D_FF = 4 * D_MODEL
ALPHA = (2 * DEPTH) ** 0.25
LN_EPS = 1e-5
POS_BASE = 10000.0

F32 = jnp.float32
BF16 = jnp.bfloat16

SUBLANES = 8
LANES = 128
STATE_W = SSM_GROUPS * SSM_STATE
SCAN_T = 32
S5_KGROUPS = 8
S5_NK = SSM_GROUPS // S5_KGROUPS
S5_KSTATE = S5_KGROUPS * SSM_STATE
SCAN_COLS = 512
MIB = 1024 * 1024

SHIFT1, SCALE1, GATE1, SHIFT2, SCALE2, GATE2 = range(6)


def _params(sem, vmem_mib):
    return pltpu.CompilerParams(dimension_semantics=sem, vmem_limit_bytes=vmem_mib * MIB)


def _dot(a, b):
    return jnp.dot(a, b, preferred_element_type=F32)


def _ln(x):
    mu = jnp.mean(x, axis=-1, keepdims=True)
    xc = x - mu
    var = jnp.mean(xc * xc, axis=-1, keepdims=True)
    return xc * lax.rsqrt(var + LN_EPS)


def _row(ref, j):
    return ref[j:j + 1, :]


def _modulated_ln(x, mod_ref, shift, scale):
    return (_ln(x) * (1.0 + _row(mod_ref, scale)) + _row(mod_ref, shift)).astype(BF16)


EPILOGUE_ROWS = 128


def _for_row_chunks(rows, fn):
    def body(r, carry):
        fn(pl.ds(pl.multiple_of(r * EPILOGUE_ROWS, EPILOGUE_ROWS), EPILOGUE_ROWS))
        return carry

    lax.fori_loop(0, rows // EPILOGUE_ROWS, body, 0)


def _mod_spec(layer, nb):
    if nb == 1:
        return pl.BlockSpec((None, None, SUBLANES, D_MODEL), lambda b, *_: (layer, 0, 0, 0))
    return pl.BlockSpec((None, None, SUBLANES, D_MODEL), lambda b, *_: (layer, b, 0, 0))


def _mod_kernel(c_ref, w_ref, b_ref, o_ref):
    c = c_ref[...]
    sc = c * jax.nn.sigmoid(c)
    o_ref[...] = _dot(sc.astype(BF16), w_ref[...].astype(BF16)) + b_ref[...]


def _modulation(cond8, w_mod, b_mod):
    tn = 1024
    n = 6 * D_MODEL
    return pl.pallas_call(
        _mod_kernel,
        grid=(DEPTH, n // tn),
        in_specs=[
            pl.BlockSpec((SUBLANES, D_MODEL), lambda l, j: (0, 0)),
            pl.BlockSpec((None, D_MODEL, tn), lambda l, j: (l, 0, j)),
            pl.BlockSpec((None, 1, tn), lambda l, j: (l, 0, j)),
        ],
        out_specs=pl.BlockSpec((None, SUBLANES, tn), lambda l, j: (l, 0, j)),
        out_shape=jax.ShapeDtypeStruct((DEPTH, SUBLANES, n), F32),
        compiler_params=_params(("parallel", "parallel"), 40),
        name="adaln_modulation",
    )(cond8, w_mod, b_mod.reshape(DEPTH, 1, n))


def _entry_kernel(has_pos, *refs):
    if has_pos:
        x_ref, p_ref, mod_ref, xo_ref, h_ref = refs
        x = x_ref[...] + p_ref[...]
        xo_ref[...] = x
    else:
        x_ref, mod_ref, h_ref = refs
        x = x_ref[...]
    h_ref[...] = _modulated_ln(x, mod_ref, SHIFT1, SCALE1)


def _entry(x, pos, mod):
    nb, rows, _ = x.shape
    tm = 512
    has_pos = pos is not None
    blk = pl.BlockSpec((None, tm, D_MODEL), lambda b, i: (b, i, 0))
    in_specs = [blk]
    args = [x]
    if has_pos:
        in_specs.append(pl.BlockSpec((tm, D_MODEL), lambda b, i: (i, 0)))
        args.append(pos)
    in_specs.append(_mod_spec(0, nb))
    args.append(mod)
    h_shape = jax.ShapeDtypeStruct(x.shape, BF16)
    out = pl.pallas_call(
        functools.partial(_entry_kernel, has_pos),
        grid=(nb, rows // tm),
        in_specs=in_specs,
        out_specs=[blk, blk] if has_pos else [blk],
        out_shape=[jax.ShapeDtypeStruct(x.shape, F32), h_shape] if has_pos else [h_shape],
        compiler_params=_params(("parallel", "parallel"), 40),
        name="entry_ln",
    )(*args)
    return out if has_pos else (x, out[0])


def _win_kernel(n_us, with_rest, h_ref, w_ref, us_ref, *rest):
    r = _dot(h_ref[...], w_ref[...].astype(BF16))
    if not with_rest:
        us_ref[...] = r
        return
    rest_ref, = rest
    j = pl.program_id(2)

    @pl.when(j < n_us)
    def _():
        us_ref[...] = r

    @pl.when(j >= n_us)
    def _():
        rest_ref[...] = r.astype(BF16)


def _in_proj(h, w_in, layer, with_rest):
    nb, rows, _ = h.shape
    tm = min(rows, 2048)
    tn = 512
    n_us = SSM_WIDTH // tn
    n_tiles = (IN_WIDTH if with_rest else SSM_WIDTH) // tn
    out_specs = [pl.BlockSpec((None, tm, tn), lambda b, i, j: (b, i, jnp.minimum(j, n_us - 1)))]
    out_shape = [jax.ShapeDtypeStruct((nb, rows, SSM_WIDTH), F32)]
    if with_rest:
        out_specs.append(pl.BlockSpec((None, tm, tn), lambda b, i, j: (b, i, jnp.maximum(j - n_us, 0))))
        out_shape.append(jax.ShapeDtypeStruct((nb, rows, REST_WIDTH), BF16))
    return pl.pallas_call(
        functools.partial(_win_kernel, n_us, with_rest),
        grid=(nb, rows // tm, n_tiles),
        in_specs=[
            pl.BlockSpec((None, tm, D_MODEL), lambda b, i, j: (b, i, 0)),
            pl.BlockSpec((None, D_MODEL, tn), lambda b, i, j: (layer, 0, j)),
        ],
        out_specs=out_specs,
        out_shape=out_shape,
        compiler_params=_params(("parallel", "parallel", "arbitrary"), 48),
        name="in_proj",
    )(h, w_in)


def _s5_kernel(need_y, uf_ref, ub_ref, h0_ref, pf_ref, pb_ref, pft_ref, pbt_ref, a_ref, wd_ref, wr_ref,
               *rest):
    if need_y:
        yf_ref, yb_ref, ht_ref, vre_ref, vim_ref, hc_ref, ycf_ref, ycb_ref = rest
    else:
        ht_ref, vre_ref, vim_ref, hc_ref = rest
    i = pl.program_id(0)
    rows_in = BATCH * SCAN_T

    @pl.when(i == 0)
    def _():
        hc_ref[...] = h0_ref[...]

    uf = uf_ref[...].reshape(rows_in, SSM_WIDTH).astype(BF16)
    ub = ub_ref[...].reshape(rows_in, SSM_WIDTH).astype(BF16)
    up_f = _dot(pf_ref[...], uf).astype(BF16)
    up_b = _dot(pb_ref[...], ub).astype(BF16)
    for k in range(S5_NK):
        cs = slice(k * LANES, (k + 1) * LANES)
        lhs = jnp.concatenate([up_f[:, cs], up_b[:, cs]], axis=1)
        v = _dot(lhs, wd_ref[k])
        ss = slice(k * S5_KSTATE, (k + 1) * S5_KSTATE)
        vre_ref[:, ss] = v[:, :S5_KSTATE]
        vim_ref[:, ss] = v[:, S5_KSTATE:]

    for cg in range(STATE_W // SCAN_COLS):
        cs = slice(cg * SCAN_COLS, (cg + 1) * SCAN_COLS)
        ar = a_ref[0, :, cs]
        ai = a_ref[1, :, cs]

        def body(s, carry, cs=cs, ar=ar, ai=ai):
            hr, hi = carry
            rs = pl.ds(pl.multiple_of(s * SUBLANES, SUBLANES), SUBLANES)
            nr = ar * hr - ai * hi + vre_ref[rs, cs]
            ni = ar * hi + ai * hr + vim_ref[rs, cs]
            vre_ref[rs, cs] = nr
            vim_ref[rs, cs] = ni
            return nr, ni

        hr, hi = lax.fori_loop(0, SCAN_T, body, (hc_ref[0, :, cs], hc_ref[1, :, cs]), unroll=2)
        hc_ref[0, :, cs] = hr
        hc_ref[1, :, cs] = hi

    if need_y:
        for k in range(S5_NK):
            ss = slice(k * S5_KSTATE, (k + 1) * S5_KSTATE)
            y = (_dot(vre_ref[:, ss].astype(BF16), wr_ref[k, 0])
                 + _dot(vim_ref[:, ss].astype(BF16), wr_ref[k, 1]))
            cs = slice(k * LANES, (k + 1) * LANES)
            ycf_ref[:, cs] = y[:, :LANES]
            ycb_ref[:, cs] = y[:, LANES:]
        yf = _dot(pft_ref[...], ycf_ref[...].astype(BF16))
        yb = _dot(pbt_ref[...], ycb_ref[...].astype(BF16))
        yf_ref[...] = yf.astype(BF16).reshape(BATCH, SCAN_T, SSM_WIDTH)
        yb_ref[...] = yb.astype(BF16).reshape(BATCH, SCAN_T, SSM_WIDTH)

    @pl.when(i == pl.num_programs(0) - 1)
    def _():
        ht_ref[...] = hc_ref[...]


def _scan_perms():
    t = SCAN_T
    pf = np.zeros((2 * BATCH * t, BATCH * t), np.float32)
    pb = np.zeros((2 * BATCH * t, BATCH * t), np.float32)
    for s in range(t):
        for b in range(BATCH):
            pf[s * 2 * BATCH + b, b * t + s] = 1.0
            pb[s * 2 * BATCH + BATCH + b, b * t + (t - 1 - s)] = 1.0
    return pf, pb


_PF, _PB = _scan_perms()


def _s5(u, h0, prm, need_y):
    a, wd, wr = prm
    steps = u.shape[1]
    n = steps // SCAN_T
    rows_in = BATCH * SCAN_T
    rows_sc = 2 * rows_in
    pf = jnp.asarray(_PF, BF16)
    pb = jnp.asarray(_PB, BF16)
    const2 = lambda i: (0, 0)
    fwd_blk = pl.BlockSpec((BATCH, SCAN_T, SSM_WIDTH), lambda i: (0, i, 0))
    bwd_blk = pl.BlockSpec((BATCH, SCAN_T, SSM_WIDTH), lambda i: (0, n - 1 - i, 0))
    state_spec = pl.BlockSpec((2, SUBLANES, STATE_W), lambda i: (0, 0, 0))
    in_specs = [
        fwd_blk, bwd_blk, state_spec,
        pl.BlockSpec((rows_sc, rows_in), const2),
        pl.BlockSpec((rows_sc, rows_in), const2),
        pl.BlockSpec((rows_in, rows_sc), const2),
        pl.BlockSpec((rows_in, rows_sc), const2),
        state_spec,
        pl.BlockSpec((S5_NK, 2 * LANES, 2 * S5_KSTATE), lambda i: (0, 0, 0)),
        pl.BlockSpec((S5_NK, 2, S5_KSTATE, 2 * LANES), lambda i: (0, 0, 0, 0)),
    ]
    state_shape = jax.ShapeDtypeStruct((2, SUBLANES, STATE_W), F32)
    scratch = [pltpu.VMEM((rows_sc, STATE_W), F32), pltpu.VMEM((rows_sc, STATE_W), F32),
               pltpu.VMEM((2, SUBLANES, STATE_W), F32)]
    if need_y:
        out_specs = [fwd_blk, bwd_blk, state_spec]
        y_shape = jax.ShapeDtypeStruct((BATCH, steps, SSM_WIDTH), BF16)
        out_shape = [y_shape, y_shape, state_shape]
        scratch += [pltpu.VMEM((rows_sc, SSM_WIDTH), F32), pltpu.VMEM((rows_sc, SSM_WIDTH), F32)]
    else:
        out_specs = [state_spec]
        out_shape = [state_shape]
    return pl.pallas_call(
        functools.partial(_s5_kernel, need_y),
        grid=(n,),
        in_specs=in_specs,
        out_specs=out_specs,
        out_shape=out_shape,
        scratch_shapes=scratch,
        compiler_params=_params(("arbitrary",), 48),
        name="s5_scan",
    )(u, u, h0, pf, pb, pf.T, pb.T, a, wd, wr)


def _s5_params(lam_re, lam_im, log_step, b_re, b_im, c_re, c_im):
    dt = jnp.exp(log_step)[..., None]
    mag = jnp.exp(lam_re * dt)
    ang = lam_im * dt
    abar_re, abar_im = mag * jnp.cos(ang), mag * jnp.sin(ang)
    den = lam_re * lam_re + lam_im * lam_im
    nr, ni = abar_re - 1.0, abar_im
    coef_re = (nr * lam_re + ni * lam_im) / den
    coef_im = (ni * lam_re - nr * lam_im) / den
    bb_re = coef_re[..., None] * b_re - coef_im[..., None] * b_im
    bb_im = coef_re[..., None] * b_im + coef_im[..., None] * b_re
    eye = jnp.eye(S5_KGROUPS, dtype=F32)
    shp = (2, S5_NK, S5_KGROUPS, SSM_STATE, SSM_GROUP)
    bb = jnp.stack([bb_re.reshape(shp), bb_im.reshape(shp)])
    wd = jnp.einsum('pdkgnc,gh->kdgcphn', bb, eye)
    wd = wd.reshape(S5_NK, 2 * LANES, 2 * S5_KSTATE).astype(BF16)
    shp = (2, S5_NK, S5_KGROUPS, SSM_GROUP, SSM_STATE)
    cc = jnp.stack([c_re.reshape(shp), -c_im.reshape(shp)])
    wr = jnp.einsum('pdkgcn,gh->kpgndhc', cc, eye)
    wr = wr.reshape(S5_NK, 2, S5_KSTATE, 2 * LANES).astype(BF16)
    a = jnp.stack([abar_re.reshape(2, STATE_W), abar_im.reshape(2, STATE_W)])
    a = jnp.repeat(a, BATCH, axis=1)
    return a, wd, wr


def _dft_tables(n):
    j = np.arange(n, dtype=np.int64)
    ang = 2.0 * np.pi * ((j[:, None] * j[None, :]) % n).astype(np.float64) / n
    s = 1.0 / math.sqrt(n)
    return (np.cos(ang) * s).astype(np.float32), (np.sin(ang) * s).astype(np.float32)


_CH_COS, _CH_SIN = _dft_tables(FFT_GROUP)
_CH_CS = np.concatenate([_CH_COS, _CH_SIN], axis=1)
_POS_TABLES = {n: _dft_tables(n) for n in (CTX_LEN, SEQ)}


def _chdft_kernel(u_ref, w_ref, p_ref, q_ref):
    r = _dot(u_ref[...], w_ref[...].astype(BF16))
    p_ref[...] = r[:, :FFT_GROUP].astype(BF16)
    q_ref[...] = r[:, FFT_GROUP:].astype(BF16)


def _channel_dft(rest):
    nb, rows, _ = rest.shape
    tm = min(rows, 2048)
    shape = jax.ShapeDtypeStruct((nb, rows, FFT_WIDTH), BF16)
    blk = pl.BlockSpec((None, tm, FFT_GROUP), lambda b, i, g: (b, i, g))
    return pl.pallas_call(
        _chdft_kernel,
        grid=(nb, rows // tm, FFT_GROUPS),
        in_specs=[blk, pl.BlockSpec((FFT_GROUP, 2 * FFT_GROUP), lambda b, i, g: (0, 0))],
        out_specs=[blk, blk],
        out_shape=[shape, shape],
        compiler_params=_params(("parallel", "parallel", "parallel"), 32),
        name="channel_dft",
    )(rest, jnp.asarray(_CH_CS))


def _posdft_kernel(nk, ac_ref, as_ref, p_ref, q_ref, o_ref, acc_ref):
    k = pl.program_id(3)

    @pl.when(k == 0)
    def _():
        acc_ref[...] = jnp.zeros_like(acc_ref)

    acc_ref[...] += (_dot(ac_ref[...].astype(BF16), p_ref[...])
                     - _dot(as_ref[...].astype(BF16), q_ref[...]))

    @pl.when(k == nk - 1)
    def _():
        o_ref[...] = acc_ref[...].astype(BF16)


def _position_dft(p, q):
    nb, steps, _ = p.shape
    cos, sin = _POS_TABLES[steps]
    tm = min(steps, 2048)
    tk = min(steps, 512)
    tn = FFT_WIDTH
    nk = steps // tk
    a_blk = pl.BlockSpec((tm, tk), lambda b, i, j, k: (i, k))
    x_blk = pl.BlockSpec((None, tk, tn), lambda b, i, j, k: (b, k, j))
    return pl.pallas_call(
        functools.partial(_posdft_kernel, nk),
        grid=(nb, steps // tm, FFT_WIDTH // tn, nk),
        in_specs=[a_blk, a_blk, x_blk, x_blk],
        out_specs=pl.BlockSpec((None, tm, tn), lambda b, i, j, k: (b, i, j)),
        out_shape=jax.ShapeDtypeStruct((nb, steps, FFT_WIDTH), BF16),
        scratch_shapes=[pltpu.VMEM((tm, tn), F32)],
        compiler_params=_params(("parallel", "parallel", "parallel", "arbitrary"), 48),
        name="position_dft",
    )(jnp.asarray(cos), jnp.asarray(sin), p, q)


def _glu_kernel(us_ref, yf_ref, yb_ref, dsk_ref, w_ref, o_ref):
    ys = dsk_ref[...] * us_ref[...] + yf_ref[...].astype(F32) + yb_ref[...].astype(F32)
    g = jax.nn.gelu(ys)
    z = _dot(g.astype(BF16), w_ref[...].astype(BF16))
    o_ref[...] = (g * jax.nn.sigmoid(z)).astype(BF16)


def _glu(us, yf, yb, d_skip, w_glu, layer):
    nb, rows, _ = us.shape
    tm = 1024
    blk = pl.BlockSpec((None, tm, SSM_WIDTH), lambda b, i: (b, i, 0))
    return pl.pallas_call(
        _glu_kernel,
        grid=(nb, rows // tm),
        in_specs=[blk, blk, blk,
                  pl.BlockSpec((None, 1, SSM_WIDTH), lambda b, i: (layer, 0, 0)),
                  pl.BlockSpec((None, SSM_WIDTH, SSM_WIDTH), lambda b, i: (layer, 0, 0))],
        out_specs=blk,
        out_shape=jax.ShapeDtypeStruct((nb, rows, SSM_WIDTH), BF16),
        compiler_params=_params(("parallel", "parallel"), 40),
        name="s5_glu",
    )(us, yf, yb, d_skip.reshape(DEPTH, 1, SSM_WIDTH), w_glu)


def _merge_kernel(s_ref, yq_ref, gs_ref, gf_ref, wps_ref, wpf_ref, o_ref):
    ps = _dot(s_ref[...], wps_ref[...].astype(BF16))
    pf = _dot(yq_ref[...], wpf_ref[...].astype(BF16))
    o_ref[...] = (jax.nn.sigmoid(gs_ref[...].astype(F32)) * ps
                  + jax.nn.sigmoid(gf_ref[...].astype(F32)) * pf).astype(BF16)


def _merge(s, yq, rest, w_ps, w_pf, layer):
    nb, rows, _ = s.shape
    tm = min(rows, 2048)
    tn = 512
    gs_off = FFT_WIDTH // tn
    gf_off = (FFT_WIDTH + D_MODEL) // tn
    row_blk = pl.BlockSpec((None, tm, SSM_WIDTH), lambda b, i, j: (b, i, 0))
    w_blk = pl.BlockSpec((None, SSM_WIDTH, tn), lambda b, i, j: (layer, 0, j))
    return pl.pallas_call(
        _merge_kernel,
        grid=(nb, rows // tm, D_MODEL // tn),
        in_specs=[
            row_blk, row_blk,
            pl.BlockSpec((None, tm, tn), lambda b, i, j: (b, i, gs_off + j)),
            pl.BlockSpec((None, tm, tn), lambda b, i, j: (b, i, gf_off + j)),
            w_blk, w_blk,
        ],
        out_specs=pl.BlockSpec((None, tm, tn), lambda b, i, j: (b, i, j)),
        out_shape=jax.ShapeDtypeStruct((nb, rows, D_MODEL), BF16),
        compiler_params=_params(("parallel", "parallel", "arbitrary"), 48),
        name="gated_merge",
    )(s, yq, rest, rest, w_ps, w_pf)


def _wo_kernel(nk, mg_ref, wo_ref, x_ref, mod_ref, g_ref, b_ref, o_ref, h_ref):
    k = pl.program_id(2)

    @pl.when(k == 0)
    def _():
        o_ref[...] = jnp.zeros_like(o_ref)

    o_ref[...] += _dot(mg_ref[...], wo_ref[...].astype(BF16))

    @pl.when(k == nk - 1)
    def _():
        def finish(rs):
            y = ALPHA * x_ref[rs, :] + _row(mod_ref, GATE1) * o_ref[rs, :]
            x1 = _ln(y) * g_ref[...] + b_ref[...]
            o_ref[rs, :] = x1
            h_ref[rs, :] = _modulated_ln(x1, mod_ref, SHIFT2, SCALE2)

        _for_row_chunks(o_ref.shape[0], finish)


def _out_proj_ln(merged, w_o, x, mod, g, b, layer):
    nb, rows, _ = x.shape
    tm, tk = 1024, 512
    nk = D_MODEL // tk
    row_blk = pl.BlockSpec((None, tm, D_MODEL), lambda b_, i, k: (b_, i, 0))
    vec = pl.BlockSpec((None, 1, D_MODEL), lambda b_, i, k: (layer, 0, 0))
    return pl.pallas_call(
        functools.partial(_wo_kernel, nk),
        grid=(nb, rows // tm, nk),
        in_specs=[
            pl.BlockSpec((None, tm, tk), lambda b_, i, k: (b_, i, k)),
            pl.BlockSpec((None, tk, D_MODEL), lambda b_, i, k: (layer, k, 0)),
            pl.BlockSpec((None, tm, D_MODEL), lambda b_, i, k: (b_, i, 0), pipeline_mode=pl.Buffered(1)),
            _mod_spec(layer, nb), vec, vec,
        ],
        out_specs=[row_blk, row_blk],
        out_shape=[jax.ShapeDtypeStruct(x.shape, F32), jax.ShapeDtypeStruct(x.shape, BF16)],
        compiler_params=_params(("parallel", "parallel", "arbitrary"), 52),
        name="out_proj_ln",
    )(merged, w_o, x, mod, g.reshape(DEPTH, 1, D_MODEL), b.reshape(DEPTH, 1, D_MODEL))


def _mlp_kernel(nf, emit_next, h_ref, x_ref, mod_ref, *refs):
    if emit_next:
        modn_ref, wu_ref, wd_ref, g_ref, b_ref, o_ref, hn_ref = refs
    else:
        wu_ref, wd_ref, g_ref, b_ref, o_ref = refs
    f = pl.program_id(2)

    @pl.when(f == 0)
    def _():
        o_ref[...] = jnp.zeros_like(o_ref)

    a = jnp.maximum(_dot(h_ref[...], wu_ref[...].astype(BF16)), 0.0)
    o_ref[...] += _dot((a * a).astype(BF16), wd_ref[...].astype(BF16))

    @pl.when(f == nf - 1)
    def _():
        def finish(rs):
            y = ALPHA * x_ref[rs, :] + _row(mod_ref, GATE2) * o_ref[rs, :]
            x2 = _ln(y) * g_ref[...] + b_ref[...]
            o_ref[rs, :] = x2
            if emit_next:
                hn_ref[rs, :] = _modulated_ln(x2, modn_ref, SHIFT1, SCALE1)

        _for_row_chunks(o_ref.shape[0], finish)


def _mlp(h2, x1, mod, w_up, w_down, g, b, layer, emit_next):
    nb, rows, _ = x1.shape
    tm, tf = 1024, 512
    nf = D_FF // tf
    row_blk = pl.BlockSpec((None, tm, D_MODEL), lambda b_, i, f: (b_, i, 0))
    vec = pl.BlockSpec((None, 1, D_MODEL), lambda b_, i, f: (layer, 0, 0))
    in_specs = [
        row_blk,
        pl.BlockSpec((None, tm, D_MODEL), lambda b_, i, f: (b_, i, 0), pipeline_mode=pl.Buffered(1)),
        _mod_spec(layer, nb),
    ]
    args = [h2, x1, mod]
    if emit_next:
        in_specs.append(_mod_spec(layer + 1, nb))
        args.append(mod)
    in_specs += [
        pl.BlockSpec((None, D_MODEL, tf), lambda b_, i, f: (layer, 0, f)),
        pl.BlockSpec((None, tf, D_MODEL), lambda b_, i, f: (layer, f, 0)),
        vec, vec,
    ]
    args += [w_up, w_down, g.reshape(DEPTH, 1, D_MODEL), b.reshape(DEPTH, 1, D_MODEL)]
    out_specs = [pl.BlockSpec((None, tm, D_MODEL), lambda b_, i, f: (b_, i, 0), pipeline_mode=pl.Buffered(1))]
    out_shape = [jax.ShapeDtypeStruct(x1.shape, F32)]
    if emit_next:
        out_specs.append(row_blk)
        out_shape.append(jax.ShapeDtypeStruct(x1.shape, BF16))
    out = pl.pallas_call(
        functools.partial(_mlp_kernel, nf, emit_next),
        grid=(nb, rows // tm, nf),
        in_specs=in_specs,
        out_specs=out_specs,
        out_shape=out_shape,
        compiler_params=_params(("parallel", "parallel", "arbitrary"), 56),
        name="mlp",
    )(*args)
    return out if emit_next else (out[0], None)


def _pos_table():
    quarter = D_MODEL // 4
    omega = 1.0 / (POS_BASE ** (np.arange(quarter, dtype=np.float64) / quarter))
    t = np.arange(SEQ)
    ar = (t // GRID_W).astype(np.float64)[:, None] * omega
    ac = (t % GRID_W).astype(np.float64)[:, None] * omega
    return np.concatenate([np.sin(ar), np.cos(ar), np.sin(ac), np.cos(ac)], axis=-1).astype(np.float32)


_POS = _pos_table()


def _mod_tables(m):
    m = m.reshape(DEPTH, SUBLANES, 6, D_MODEL)
    m = jnp.pad(m, ((0, 0), (0, 0), (0, SUBLANES - 6), (0, 0)))
    return m[:, :BATCH], m[:, BATCH:BATCH + 1]


def _as_batch(a):
    return a.reshape(BATCH, CTX_LEN, a.shape[-1])


def _as_slab(a):
    return a.reshape(1, BATCH * CTX_LEN, a.shape[-1])


def kernel(x, c, ctx, c_ctx, w_mod, b_mod, w_in, lam_re, lam_im, log_step, ssm_b_re, ssm_b_im, ssm_c_re,
           ssm_c_im, d_skip, w_glu, w_ps, w_pf, w_o, ln1_g, ln1_b, w_up, w_down, ln2_g, ln2_b):
    cond8 = jnp.concatenate([c, c_ctx[None], jnp.zeros((SUBLANES - BATCH - 1, D_MODEL), F32)], axis=0)
    mod_lat, mod_ctx = _mod_tables(_modulation(cond8, w_mod, b_mod))

    x_lat, h_lat = _entry(x, jnp.asarray(_POS), mod_lat)
    x_ctx, h_ctx = _entry(_as_slab(ctx), None, mod_ctx)
    h_zero = jnp.zeros((2, SUBLANES, STATE_W), F32)

    for l in range(DEPTH):
        need_ctx = l < DEPTH - 1
        prm = _s5_params(lam_re[l], lam_im[l], log_step[l], ssm_b_re[l], ssm_b_im[l], ssm_c_re[l],
                         ssm_c_im[l])

        def mixer_tail(us, rest, yf, yb, xx, mod, as_batch, as_rows):
            p, q = _channel_dft(rest)
            yq = as_rows(_position_dft(as_batch(p), as_batch(q)))
            s = _glu(us, as_rows(yf), as_rows(yb), d_skip, w_glu, l)
            merged = _merge(s, yq, rest, w_ps, w_pf, l)
            x1, h2 = _out_proj_ln(merged, w_o, xx, mod, ln1_g, ln1_b, l)
            return _mlp(h2, x1, mod, w_up, w_down, ln2_g, ln2_b, l, need_ctx)

        us_lat, rest_lat = _in_proj(h_lat, w_in, l, True)
        if need_ctx:
            us_ctx, rest_ctx = _in_proj(h_ctx, w_in, l, True)
            yf_c, yb_c, h_t = _s5(_as_batch(us_ctx), h_zero, prm, True)
        else:
            (us_ctx,) = _in_proj(h_ctx, w_in, l, False)
            (h_t,) = _s5(_as_batch(us_ctx), h_zero, prm, False)
        yf, yb, _ = _s5(us_lat, h_t, prm, True)
        ident = lambda a: a
        x_lat, h_lat = mixer_tail(us_lat, rest_lat, yf, yb, x_lat, mod_lat, ident, ident)
        if need_ctx:
            x_ctx, h_ctx = mixer_tail(us_ctx, rest_ctx, yf_c, yb_c, x_ctx, mod_ctx, _as_batch, _as_slab)

    return x_lat
```

```python
import functools
import math

import numpy as np
import jax
import jax.numpy as jnp
from jax import lax
from jax.experimental import pallas as pl
from jax.experimental.pallas import tpu as pltpu

D_MODEL = 2048
BATCH = 4
SEQ = 2048
DEPTH = 2
GRID_W = 64
CTX_LEN = 256
SSM_WIDTH = D_MODEL // 2
SSM_GROUP = 16
SSM_GROUPS = SSM_WIDTH // SSM_GROUP
SSM_STATE = 64
FFT_WIDTH = D_MODEL - SSM_WIDTH
FFT_GROUPS = 4
FFT_GROUP = FFT_WIDTH // FFT_GROUPS
IN_WIDTH = SSM_WIDTH + FFT_WIDTH + 2 * D_MODEL
REST_WIDTH = IN_WIDTH - SSM_WIDTH
D_FF = 4 * D_MODEL
ALPHA = (2 * DEPTH) ** 0.25
LN_EPS = 1e-5
POS_BASE = 10000.0

F32 = jnp.float32
BF16 = jnp.bfloat16

SUBLANES = 8
LANES = 128
STATE_W = SSM_GROUPS * SSM_STATE
SCAN_T = 32
S5_KGROUPS = 8
S5_NK = SSM_GROUPS // S5_KGROUPS
S5_KSTATE = S5_KGROUPS * SSM_STATE
SCAN_COLS = 512
MIB = 1024 * 1024

SHIFT1, SCALE1, GATE1, SHIFT2, SCALE2, GATE2 = range(6)


def _params(sem, vmem_mib):
    return pltpu.CompilerParams(dimension_semantics=sem, vmem_limit_bytes=vmem_mib * MIB)


def _dot(a, b):
    return jnp.dot(a, b, preferred_element_type=F32)


def _ln(x):
    mu = jnp.mean(x, axis=-1, keepdims=True)
    xc = x - mu
    var = jnp.mean(xc * xc, axis=-1, keepdims=True)
    return xc * lax.rsqrt(var + LN_EPS)


def _row(ref, j):
    return ref[j:j + 1, :]


def _modulated_ln(x, mod_ref, shift, scale):
    return (_ln(x) * (1.0 + _row(mod_ref, scale)) + _row(mod_ref, shift)).astype(BF16)


EPILOGUE_ROWS = 128


def _for_row_chunks(rows, fn):
    def body(r, carry):
        fn(pl.ds(pl.multiple_of(r * EPILOGUE_ROWS, EPILOGUE_ROWS), EPILOGUE_ROWS))
        return carry

    lax.fori_loop(0, rows // EPILOGUE_ROWS, body, 0)


def _mod_spec(layer, nb):
    if nb == 1:
        return pl.BlockSpec((None, None, SUBLANES, D_MODEL), lambda b, *_: (layer, 0, 0, 0))
    return pl.BlockSpec((None, None, SUBLANES, D_MODEL), lambda b, *_: (layer, b, 0, 0))


def _mod_kernel(c_ref, w_ref, b_ref, o_ref):
    c = c_ref[...]
    sc = c * jax.nn.sigmoid(c)
    o_ref[...] = _dot(sc.astype(BF16), w_ref[...].astype(BF16)) + b_ref[...]


def _modulation(cond8, w_mod, b_mod):
    tn = 1024
    n = 6 * D_MODEL
    return pl.pallas_call(
        _mod_kernel,
        grid=(DEPTH, n // tn),
        in_specs=[
            pl.BlockSpec((SUBLANES, D_MODEL), lambda l, j: (0, 0)),
            pl.BlockSpec((None, D_MODEL, tn), lambda l, j: (l, 0, j)),
            pl.BlockSpec((None, 1, tn), lambda l, j: (l, 0, j)),
        ],
        out_specs=pl.BlockSpec((None, SUBLANES, tn), lambda l, j: (l, 0, j)),
        out_shape=jax.ShapeDtypeStruct((DEPTH, SUBLANES, n), F32),
        compiler_params=_params(("parallel", "parallel"), 40),
        name="adaln_modulation",
    )(cond8, w_mod, b_mod.reshape(DEPTH, 1, n))


def _entry_kernel(has_pos, *refs):
    if has_pos:
        x_ref, p_ref, mod_ref, xo_ref, h_ref = refs
        x = x_ref[...] + p_ref[...]
        xo_ref[...] = x
    else:
        x_ref, mod_ref, h_ref = refs
        x = x_ref[...]
    h_ref[...] = _modulated_ln(x, mod_ref, SHIFT1, SCALE1)


def _entry(x, pos, mod):
    nb, rows, _ = x.shape
    tm = 512
    has_pos = pos is not None
    blk = pl.BlockSpec((None, tm, D_MODEL), lambda i, b: (b, i, 0))
    in_specs = [blk]
    args = [x]
    if has_pos:
        in_specs.append(pl.BlockSpec((tm, D_MODEL), lambda i, b: (i, 0)))
        args.append(pos)
    in_specs.append(pl.BlockSpec((None, None, SUBLANES, D_MODEL), lambda i, b: (0, b, 0, 0)))
    args.append(mod)
    h_shape = jax.ShapeDtypeStruct(x.shape, BF16)
    out = pl.pallas_call(
        functools.partial(_entry_kernel, has_pos),
        grid=(rows // tm, nb),
        in_specs=in_specs,
        out_specs=[blk, blk] if has_pos else [blk],
        out_shape=[jax.ShapeDtypeStruct(x.shape, F32), h_shape] if has_pos else [h_shape],
        compiler_params=_params(("parallel", "parallel"), 40),
        name="entry_ln",
    )(*args)
    return out if has_pos else (x, out[0])


def _win_kernel(n_us, with_rest, h_ref, w_ref, us_ref, *rest):
    r = _dot(h_ref[...], w_ref[...].astype(BF16))
    if not with_rest:
        us_ref[...] = r
        return
    rest_ref, = rest
    j = pl.program_id(2)

    @pl.when(j < n_us)
    def _():
        us_ref[...] = r

    @pl.when(j >= n_us)
    def _():
        rest_ref[...] = r.astype(BF16)


def _in_proj(h, w_in, layer, with_rest):
    nb, rows, _ = h.shape
    tm = min(rows, 2048)
    tn = 512
    n_us = SSM_WIDTH // tn
    n_tiles = (IN_WIDTH if with_rest else SSM_WIDTH) // tn
    out_specs = [pl.BlockSpec((None, tm, tn), lambda b, i, j: (b, i, jnp.minimum(j, n_us - 1)))]
    out_shape = [jax.ShapeDtypeStruct((nb, rows, SSM_WIDTH), F32)]
    if with_rest:
        out_specs.append(pl.BlockSpec((None, tm, tn), lambda b, i, j: (b, i, jnp.maximum(j - n_us, 0))))
        out_shape.append(jax.ShapeDtypeStruct((nb, rows, REST_WIDTH), BF16))
    return pl.pallas_call(
        functools.partial(_win_kernel, n_us, with_rest),
        grid=(nb, rows // tm, n_tiles),
        in_specs=[
            pl.BlockSpec((None, tm, D_MODEL), lambda b, i, j: (b, i, 0)),
            pl.BlockSpec((None, D_MODEL, tn), lambda b, i, j: (layer, 0, j)),
        ],
        out_specs=out_specs,
        out_shape=out_shape,
        compiler_params=_params(("parallel", "parallel", "arbitrary"), 48),
        name="in_proj",
    )(h, w_in)


def _s5_kernel(n, need_y, uf_ref, ub_ref, h0_ref, pf_ref, pb_ref, pft_ref, pbt_ref, a_ref, wd_ref, wr_ref,
               *rest):
    if need_y:
        yf_ref, yb_ref, ht_ref = rest[:3]
        scr = rest[3:]
        v_re, v_im, h_re, h_im = scr[0:2], scr[2:4], scr[4:6], scr[6:8]
        hc_ref, ycf_ref, ycb_ref = scr[8:11]
    else:
        ht_ref = rest[0]
        scr = rest[1:]
        v_re, v_im = scr[0:2], scr[2:4]
        hc_ref = scr[4]
    g = pl.program_id(0)
    rows_in = BATCH * SCAN_T

    @pl.when(g == 0)
    def _():
        hc_ref[...] = h0_ref[...]
        for buf in scr[:8 if need_y else 4]:
            buf[...] = jnp.zeros_like(buf)

    scan_valid = jnp.logical_and(g >= 1, g <= n)

    def stages(par):
        uf = uf_ref[...].reshape(rows_in, SSM_WIDTH).astype(BF16)
        ub = ub_ref[...].reshape(rows_in, SSM_WIDTH).astype(BF16)
        up_f = _dot(pf_ref[...], uf).astype(BF16)
        up_b = _dot(pb_ref[...], ub).astype(BF16)
        for k in range(S5_NK):
            cs = slice(k * LANES, (k + 1) * LANES)
            ss = slice(k * S5_KSTATE, (k + 1) * S5_KSTATE)
            lhs = jnp.concatenate([up_f[:, cs], up_b[:, cs]], axis=1)
            v = _dot(lhs, wd_ref[k])
            v_re[par][:, ss] = v[:, :S5_KSTATE]
            v_im[par][:, ss] = v[:, S5_KSTATE:]
            ar = a_ref[0, :, ss]
            ai = a_ref[1, :, ss]
            hr0 = hc_ref[0, :, ss]
            hi0 = hc_ref[1, :, ss]
            hr, hi = hr0, hi0
            for s in range(SCAN_T):
                rs = slice(s * SUBLANES, (s + 1) * SUBLANES)
                nr = ar * hr - ai * hi + v_re[1 - par][rs, ss]
                ni = ar * hi + ai * hr + v_im[1 - par][rs, ss]
                if need_y:
                    h_re[1 - par][rs, ss] = nr
                    h_im[1 - par][rs, ss] = ni
                hr, hi = nr, ni
            hc_ref[0, :, ss] = jnp.where(scan_valid, hr, hr0)
            hc_ref[1, :, ss] = jnp.where(scan_valid, hi, hi0)
            if need_y:
                h = jnp.concatenate([h_re[par][:, ss].astype(BF16), h_im[par][:, ss].astype(BF16)], axis=1)
                y = _dot(h, wr_ref[k])
                ycf_ref[:, cs] = y[:, :LANES]
                ycb_ref[:, cs] = y[:, LANES:]
        if need_y:
            yf = _dot(pft_ref[...], ycf_ref[...].astype(BF16))
            yb = _dot(pbt_ref[...], ycb_ref[...].astype(BF16))
            yf_ref[...] = yf.astype(BF16).reshape(BATCH, SCAN_T, SSM_WIDTH)
            yb_ref[...] = yb.astype(BF16).reshape(BATCH, SCAN_T, SSM_WIDTH)

    @pl.when(g % 2 == 0)
    def _():
        stages(0)

    @pl.when(g % 2 == 1)
    def _():
        stages(1)

    @pl.when(g == pl.num_programs(0) - 1)
    def _():
        ht_ref[...] = hc_ref[...]


def _scan_perms():
    t = SCAN_T
    pf = np.zeros((2 * BATCH * t, BATCH * t), np.float32)
    pb = np.zeros((2 * BATCH * t, BATCH * t), np.float32)
    for s in range(t):
        for b in range(BATCH):
            pf[s * 2 * BATCH + b, b * t + s] = 1.0
            pb[s * 2 * BATCH + BATCH + b, b * t + (t - 1 - s)] = 1.0
    return pf, pb


_PF, _PB = _scan_perms()


def _s5(u, h0, prm, need_y):
    a, wd, wr = prm
    steps = u.shape[1]
    n = steps // SCAN_T
    rows_in = BATCH * SCAN_T
    rows_sc = 2 * rows_in
    pf = jnp.asarray(_PF, BF16)
    pb = jnp.asarray(_PB, BF16)
    const2 = lambda g: (0, 0)
    once = pl.Buffered(1)
    blk = (BATCH, SCAN_T, SSM_WIDTH)
    drive_f = pl.BlockSpec(blk, lambda g: (0, jnp.minimum(g, n - 1), 0))
    drive_b = pl.BlockSpec(blk, lambda g: (0, jnp.maximum(n - 1 - g, 0), 0))
    read_f = pl.BlockSpec(blk, lambda g: (0, jnp.clip(g - 2, 0, n - 1), 0))
    read_b = pl.BlockSpec(blk, lambda g: (0, jnp.clip(n + 1 - g, 0, n - 1), 0))
    state_spec = pl.BlockSpec((2, SUBLANES, STATE_W), lambda g: (0, 0, 0))
    in_specs = [
        drive_f, drive_b, state_spec,
        pl.BlockSpec((rows_sc, rows_in), const2),
        pl.BlockSpec((rows_sc, rows_in), const2),
        pl.BlockSpec((rows_in, rows_sc), const2),
        pl.BlockSpec((rows_in, rows_sc), const2),
        state_spec,
        pl.BlockSpec((S5_NK, 2 * LANES, 2 * S5_KSTATE), lambda g: (0, 0, 0), pipeline_mode=once),
        pl.BlockSpec((S5_NK, 2 * S5_KSTATE, 2 * LANES), lambda g: (0, 0, 0), pipeline_mode=once),
    ]
    state_shape = jax.ShapeDtypeStruct((2, SUBLANES, STATE_W), F32)
    chunk_buf = pltpu.VMEM((rows_sc, STATE_W), F32)
    carry_buf = pltpu.VMEM((2, SUBLANES, STATE_W), F32)
    if need_y:
        out_specs = [read_f, read_b, state_spec]
        y_shape = jax.ShapeDtypeStruct((BATCH, steps, SSM_WIDTH), BF16)
        out_shape = [y_shape, y_shape, state_shape]
        scratch = [chunk_buf] * 8 + [carry_buf, pltpu.VMEM((rows_sc, SSM_WIDTH), F32),
                                     pltpu.VMEM((rows_sc, SSM_WIDTH), F32)]
    else:
        out_specs = [state_spec]
        out_shape = [state_shape]
        scratch = [chunk_buf] * 4 + [carry_buf]
    return pl.pallas_call(
        functools.partial(_s5_kernel, n, need_y),
        grid=(n + 2,),
        in_specs=in_specs,
        out_specs=out_specs,
        out_shape=out_shape,
        scratch_shapes=scratch,
        compiler_params=_params(("arbitrary",), 52),
        name="s5_scan",
    )(u, u, h0, pf, pb, pf.T, pb.T, a, wd, wr)


def _s5_params(lam_re, lam_im, log_step, b_re, b_im, c_re, c_im):
    dt = jnp.exp(log_step)[..., None]
    mag = jnp.exp(lam_re * dt)
    ang = lam_im * dt
    abar_re, abar_im = mag * jnp.cos(ang), mag * jnp.sin(ang)
    den = lam_re * lam_re + lam_im * lam_im
    nr, ni = abar_re - 1.0, abar_im
    coef_re = (nr * lam_re + ni * lam_im) / den
    coef_im = (ni * lam_re - nr * lam_im) / den
    bb_re = coef_re[..., None] * b_re - coef_im[..., None] * b_im
    bb_im = coef_re[..., None] * b_im + coef_im[..., None] * b_re
    shp = (2, S5_NK, S5_KGROUPS, SSM_STATE, SSM_GROUP)
    bb = jnp.stack([bb_re.reshape(shp), bb_im.reshape(shp)])
    bb = jnp.tile(jnp.transpose(bb, (2, 1, 3, 5, 0, 4)), (1, 1, 1, 1, 1, S5_KGROUPS))
    own = jnp.arange(S5_KSTATE)[None, :] // SSM_STATE == jnp.arange(S5_KGROUPS)[:, None]
    wd = jnp.where(own[None, None, :, None, None, :], bb, 0.0)
    wd = wd.reshape(S5_NK, 2 * LANES, 2 * S5_KSTATE).astype(BF16)
    shp = (2, S5_NK, S5_KGROUPS, SSM_GROUP, SSM_STATE)
    cc = jnp.stack([c_re.reshape(shp), -c_im.reshape(shp)])
    cc = jnp.tile(jnp.transpose(cc, (2, 0, 3, 5, 1, 4)), (1, 1, 1, 1, 1, S5_KGROUPS))
    own = jnp.arange(LANES)[None, :] // SSM_GROUP == jnp.arange(S5_KGROUPS)[:, None]
    wr = jnp.where(own[None, None, :, None, None, :], cc, 0.0)
    wr = wr.reshape(S5_NK, 2 * S5_KSTATE, 2 * LANES).astype(BF16)
    a = jnp.stack([abar_re.reshape(2, STATE_W), abar_im.reshape(2, STATE_W)])
    a = jnp.repeat(a, BATCH, axis=1)
    return a, wd, wr


def _dft_tables(n):
    j = np.arange(n, dtype=np.int64)
    ang = 2.0 * np.pi * ((j[:, None] * j[None, :]) % n).astype(np.float64) / n
    s = 1.0 / math.sqrt(n)
    return (np.cos(ang) * s).astype(np.float32), (np.sin(ang) * s).astype(np.float32)


_CH_COS, _CH_SIN = _dft_tables(FFT_GROUP)
_CH_CS = np.concatenate([_CH_COS, _CH_SIN], axis=1)
_POS_TABLES = {n: _dft_tables(n) for n in (CTX_LEN, SEQ)}


def _chdft_kernel(u_ref, w_ref, p_ref, q_ref):
    r = _dot(u_ref[...], w_ref[...].astype(BF16))
    p_ref[...] = r[:, :FFT_GROUP].astype(BF16)
    q_ref[...] = r[:, FFT_GROUP:].astype(BF16)


def _channel_dft(rest):
    nb, rows, _ = rest.shape
    tm = min(rows, 2048)
    shape = jax.ShapeDtypeStruct((nb, rows, FFT_WIDTH), BF16)
    blk = pl.BlockSpec((None, tm, FFT_GROUP), lambda b, i, g: (b, i, g))
    return pl.pallas_call(
        _chdft_kernel,
        grid=(nb, rows // tm, FFT_GROUPS),
        in_specs=[blk, pl.BlockSpec((FFT_GROUP, 2 * FFT_GROUP), lambda b, i, g: (0, 0))],
        out_specs=[blk, blk],
        out_shape=[shape, shape],
        compiler_params=_params(("parallel", "parallel", "parallel"), 32),
        name="channel_dft",
    )(rest, jnp.asarray(_CH_CS))


def _posdft_kernel(nk, ac_ref, as_ref, p_ref, q_ref, o_ref, acc_ref):
    k = pl.program_id(3)

    @pl.when(k == 0)
    def _():
        acc_ref[...] = jnp.zeros_like(acc_ref)

    acc_ref[...] += (_dot(ac_ref[...].astype(BF16), p_ref[...])
                     - _dot(as_ref[...].astype(BF16), q_ref[...]))

    @pl.when(k == nk - 1)
    def _():
        o_ref[...] = acc_ref[...].astype(BF16)


def _position_dft(p, q):
    nb, steps, _ = p.shape
    cos, sin = _POS_TABLES[steps]
    tm = min(steps, 2048)
    tk = min(steps, 512)
    tn = FFT_WIDTH
    nk = steps // tk
    a_blk = pl.BlockSpec((tm, tk), lambda b, i, j, k: (i, k))
    x_blk = pl.BlockSpec((None, tk, tn), lambda b, i, j, k: (b, k, j))
    return pl.pallas_call(
        functools.partial(_posdft_kernel, nk),
        grid=(nb, steps // tm, FFT_WIDTH // tn, nk),
        in_specs=[a_blk, a_blk, x_blk, x_blk],
        out_specs=pl.BlockSpec((None, tm, tn), lambda b, i, j, k: (b, i, j)),
        out_shape=jax.ShapeDtypeStruct((nb, steps, FFT_WIDTH), BF16),
        scratch_shapes=[pltpu.VMEM((tm, tn), F32)],
        compiler_params=_params(("parallel", "parallel", "parallel", "arbitrary"), 48),
        name="position_dft",
    )(jnp.asarray(cos), jnp.asarray(sin), p, q)


def _glu_kernel(us_ref, yf_ref, yb_ref, dsk_ref, w_ref, o_ref):
    ys = dsk_ref[...] * us_ref[...] + yf_ref[...].astype(F32) + yb_ref[...].astype(F32)
    g = jax.nn.gelu(ys)
    z = _dot(g.astype(BF16), w_ref[...].astype(BF16))
    o_ref[...] = (g * jax.nn.sigmoid(z)).astype(BF16)


def _glu(us, yf, yb, d_skip, w_glu, layer):
    nb, rows, _ = us.shape
    tm = 1024
    blk = pl.BlockSpec((None, tm, SSM_WIDTH), lambda b, i: (b, i, 0))
    return pl.pallas_call(
        _glu_kernel,
        grid=(nb, rows // tm),
        in_specs=[blk, blk, blk,
                  pl.BlockSpec((None, 1, SSM_WIDTH), lambda b, i: (layer, 0, 0)),
                  pl.BlockSpec((None, SSM_WIDTH, SSM_WIDTH), lambda b, i: (layer, 0, 0))],
        out_specs=blk,
        out_shape=jax.ShapeDtypeStruct((nb, rows, SSM_WIDTH), BF16),
        compiler_params=_params(("parallel", "parallel"), 40),
        name="s5_glu",
    )(us, yf, yb, d_skip.reshape(DEPTH, 1, SSM_WIDTH), w_glu)


def _merge_kernel(s_ref, yq_ref, gs_ref, gf_ref, wps_ref, wpf_ref, o_ref):
    ps = _dot(s_ref[...], wps_ref[...].astype(BF16))
    pf = _dot(yq_ref[...], wpf_ref[...].astype(BF16))
    o_ref[...] = (jax.nn.sigmoid(gs_ref[...].astype(F32)) * ps
                  + jax.nn.sigmoid(gf_ref[...].astype(F32)) * pf).astype(BF16)


def _merge(s, yq, rest, w_ps, w_pf, layer):
    nb, rows, _ = s.shape
    tm = min(rows, 2048)
    tn = 512
    gs_off = FFT_WIDTH // tn
    gf_off = (FFT_WIDTH + D_MODEL) // tn
    row_blk = pl.BlockSpec((None, tm, SSM_WIDTH), lambda b, i, j: (b, i, 0))
    w_blk = pl.BlockSpec((None, SSM_WIDTH, tn), lambda b, i, j: (layer, 0, j))
    return pl.pallas_call(
        _merge_kernel,
        grid=(nb, rows // tm, D_MODEL // tn),
        in_specs=[
            row_blk, row_blk,
            pl.BlockSpec((None, tm, tn), lambda b, i, j: (b, i, gs_off + j)),
            pl.BlockSpec((None, tm, tn), lambda b, i, j: (b, i, gf_off + j)),
            w_blk, w_blk,
        ],
        out_specs=pl.BlockSpec((None, tm, tn), lambda b, i, j: (b, i, j)),
        out_shape=jax.ShapeDtypeStruct((nb, rows, D_MODEL), BF16),
        compiler_params=_params(("parallel", "parallel", "arbitrary"), 48),
        name="gated_merge",
    )(s, yq, rest, rest, w_ps, w_pf)


def _wo_kernel(nk, mg_ref, wo_ref, x_ref, mod_ref, g_ref, b_ref, o_ref, h_ref):
    k = pl.program_id(2)

    @pl.when(k == 0)
    def _():
        o_ref[...] = jnp.zeros_like(o_ref)

    o_ref[...] += _dot(mg_ref[...], wo_ref[...].astype(BF16))

    @pl.when(k == nk - 1)
    def _():
        def finish(rs):
            y = ALPHA * x_ref[rs, :] + _row(mod_ref, GATE1) * o_ref[rs, :]
            x1 = _ln(y) * g_ref[...] + b_ref[...]
            o_ref[rs, :] = x1
            h_ref[rs, :] = _modulated_ln(x1, mod_ref, SHIFT2, SCALE2)

        _for_row_chunks(o_ref.shape[0], finish)


def _out_proj_ln(merged, w_o, x, mod, g, b, layer):
    nb, rows, _ = x.shape
    tm, tk = 1024, 512
    nk = D_MODEL // tk
    row_blk = pl.BlockSpec((None, tm, D_MODEL), lambda b_, i, k: (b_, i, 0))
    vec = pl.BlockSpec((None, 1, D_MODEL), lambda b_, i, k: (layer, 0, 0))
    return pl.pallas_call(
        functools.partial(_wo_kernel, nk),
        grid=(nb, rows // tm, nk),
        in_specs=[
            pl.BlockSpec((None, tm, tk), lambda b_, i, k: (b_, i, k)),
            pl.BlockSpec((None, tk, D_MODEL), lambda b_, i, k: (layer, k, 0)),
            pl.BlockSpec((None, tm, D_MODEL), lambda b_, i, k: (b_, i, 0), pipeline_mode=pl.Buffered(1)),
            _mod_spec(layer, nb), vec, vec,
        ],
        out_specs=[row_blk, row_blk],
        out_shape=[jax.ShapeDtypeStruct(x.shape, F32), jax.ShapeDtypeStruct(x.shape, BF16)],
        compiler_params=_params(("parallel", "parallel", "arbitrary"), 52),
        name="out_proj_ln",
    )(merged, w_o, x, mod, g.reshape(DEPTH, 1, D_MODEL), b.reshape(DEPTH, 1, D_MODEL))


def _mlp_kernel(nf, emit_next, h_ref, x_ref, mod_ref, *refs):
    if emit_next:
        modn_ref, wu_ref, wd_ref, g_ref, b_ref, o_ref, hn_ref = refs
    else:
        wu_ref, wd_ref, g_ref, b_ref, o_ref = refs
    f = pl.program_id(2)

    @pl.when(f == 0)
    def _():
        o_ref[...] = jnp.zeros_like(o_ref)

    a = jnp.maximum(_dot(h_ref[...], wu_ref[...].astype(BF16)), 0.0)
    o_ref[...] += _dot((a * a).astype(BF16), wd_ref[...].astype(BF16))

    @pl.when(f == nf - 1)
    def _():
        def finish(rs):
            y = ALPHA * x_ref[rs, :] + _row(mod_ref, GATE2) * o_ref[rs, :]
            x2 = _ln(y) * g_ref[...] + b_ref[...]
            o_ref[rs, :] = x2
            if emit_next:
                hn_ref[rs, :] = _modulated_ln(x2, modn_ref, SHIFT1, SCALE1)

        _for_row_chunks(o_ref.shape[0], finish)


def _mlp(h2, x1, mod, w_up, w_down, g, b, layer, emit_next):
    nb, rows, _ = x1.shape
    tm, tf = 1024, 512
    nf = D_FF // tf
    row_blk = pl.BlockSpec((None, tm, D_MODEL), lambda b_, i, f: (b_, i, 0))
    vec = pl.BlockSpec((None, 1, D_MODEL), lambda b_, i, f: (layer, 0, 0))
    in_specs = [
        row_blk,
        pl.BlockSpec((None, tm, D_MODEL), lambda b_, i, f: (b_, i, 0), pipeline_mode=pl.Buffered(1)),
        _mod_spec(layer, nb),
    ]
    args = [h2, x1, mod]
    if emit_next:
        in_specs.append(_mod_spec(layer + 1, nb))
        args.append(mod)
    in_specs += [
        pl.BlockSpec((None, D_MODEL, tf), lambda b_, i, f: (layer, 0, f)),
        pl.BlockSpec((None, tf, D_MODEL), lambda b_, i, f: (layer, f, 0)),
        vec, vec,
    ]
    args += [w_up, w_down, g.reshape(DEPTH, 1, D_MODEL), b.reshape(DEPTH, 1, D_MODEL)]
    out_specs = [pl.BlockSpec((None, tm, D_MODEL), lambda b_, i, f: (b_, i, 0), pipeline_mode=pl.Buffered(1))]
    out_shape = [jax.ShapeDtypeStruct(x1.shape, F32)]
    if emit_next:
        out_specs.append(row_blk)
        out_shape.append(jax.ShapeDtypeStruct(x1.shape, BF16))
    out = pl.pallas_call(
        functools.partial(_mlp_kernel, nf, emit_next),
        grid=(nb, rows // tm, nf),
        in_specs=in_specs,
        out_specs=out_specs,
        out_shape=out_shape,
        compiler_params=_params(("parallel", "parallel", "arbitrary"), 56),
        name="mlp",
    )(*args)
    return out if emit_next else (out[0], None)


def _pos_table():
    quarter = D_MODEL // 4
    omega = 1.0 / (POS_BASE ** (np.arange(quarter, dtype=np.float64) / quarter))
    t = np.arange(SEQ)
    ar = (t // GRID_W).astype(np.float64)[:, None] * omega
    ac = (t % GRID_W).astype(np.float64)[:, None] * omega
    return np.concatenate([np.sin(ar), np.cos(ar), np.sin(ac), np.cos(ac)], axis=-1).astype(np.float32)


_POS = _pos_table()


def _mod_tables(m):
    m = m.reshape(DEPTH, SUBLANES, 6, D_MODEL)
    m = jnp.pad(m, ((0, 0), (0, 0), (0, SUBLANES - 6), (0, 0)))
    return m[:, :BATCH], m[:, BATCH:BATCH + 1]


def _as_batch(a):
    return a.reshape(BATCH, CTX_LEN, a.shape[-1])


def _as_slab(a):
    return a.reshape(1, BATCH * CTX_LEN, a.shape[-1])


def kernel(x, c, ctx, c_ctx, w_mod, b_mod, w_in, lam_re, lam_im, log_step, ssm_b_re, ssm_b_im, ssm_c_re,
           ssm_c_im, d_skip, w_glu, w_ps, w_pf, w_o, ln1_g, ln1_b, w_up, w_down, ln2_g, ln2_b):
    cond8 = jnp.concatenate([c, c_ctx[None], jnp.zeros((SUBLANES - BATCH - 1, D_MODEL), F32)], axis=0)
    mod_lat, mod_ctx = _mod_tables(_modulation(cond8, w_mod, b_mod))

    x_lat, h_lat = _entry(x, jnp.asarray(_POS), mod_lat)
    x_ctx, h_ctx = _entry(_as_slab(ctx), None, mod_ctx)
    h_zero = jnp.zeros((2, SUBLANES, STATE_W), F32)

    for l in range(DEPTH):
        need_ctx = l < DEPTH - 1
        prm = _s5_params(lam_re[l], lam_im[l], log_step[l], ssm_b_re[l], ssm_b_im[l], ssm_c_re[l],
                         ssm_c_im[l])

        def mixer_tail(us, rest, yf, yb, xx, mod, as_batch, as_rows):
            p, q = _channel_dft(rest)
            yq = as_rows(_position_dft(as_batch(p), as_batch(q)))
            s = _glu(us, as_rows(yf), as_rows(yb), d_skip, w_glu, l)
            merged = _merge(s, yq, rest, w_ps, w_pf, l)
            x1, h2 = _out_proj_ln(merged, w_o, xx, mod, ln1_g, ln1_b, l)
            return _mlp(h2, x1, mod, w_up, w_down, ln2_g, ln2_b, l, need_ctx)

        us_lat, rest_lat = _in_proj(h_lat, w_in, l, True)
        if need_ctx:
            us_ctx, rest_ctx = _in_proj(h_ctx, w_in, l, True)
            yf_c, yb_c, h_t = _s5(_as_batch(us_ctx), h_zero, prm, True)
        else:
            (us_ctx,) = _in_proj(h_ctx, w_in, l, False)
            (h_t,) = _s5(_as_batch(us_ctx), h_zero, prm, False)
        yf, yb, _ = _s5(us_lat, h_t, prm, True)
        ident = lambda a: a
        x_lat, h_lat = mixer_tail(us_lat, rest_lat, yf, yb, x_lat, mod_lat, ident, ident)
        if need_ctx:
            x_ctx, h_ctx = mixer_tail(us_ctx, rest_ctx, yf_c, yb_c, x_ctx, mod_ctx, _as_batch, _as_slab)

    return x_lat
```

```python
import functools
import math

import numpy as np
import jax
import jax.numpy as jnp
from jax import lax
from jax.experimental import pallas as pl
from jax.experimental.pallas import tpu as pltpu

D_MODEL = 2048
BATCH = 4
SEQ = 2048
DEPTH = 2
GRID_W = 64
CTX_LEN = 256
SSM_WIDTH = D_MODEL // 2
SSM_GROUP = 16
SSM_GROUPS = SSM_WIDTH // SSM_GROUP
SSM_STATE = 64
FFT_WIDTH = D_MODEL - SSM_WIDTH
FFT_GROUPS = 4
FFT_GROUP = FFT_WIDTH // FFT_GROUPS
IN_WIDTH = SSM_WIDTH + FFT_WIDTH + 2 * D_MODEL
REST_WIDTH = IN_WIDTH - SSM_WIDTH
D_FF = 4 * D_MODEL
ALPHA = (2 * DEPTH) ** 0.25
LN_EPS = 1e-5
POS_BASE = 10000.0

F32 = jnp.float32
BF16 = jnp.bfloat16

SUBLANES = 8
LANES = 128
STATE_W = SSM_GROUPS * SSM_STATE
SCAN_T = 32
S5_KGROUPS = 8
S5_NK = SSM_GROUPS // S5_KGROUPS
S5_KSTATE = S5_KGROUPS * SSM_STATE
SCAN_COLS = 512
MIB = 1024 * 1024

SHIFT1, SCALE1, GATE1, SHIFT2, SCALE2, GATE2 = range(6)


def _params(sem, vmem_mib):
    return pltpu.CompilerParams(dimension_semantics=sem, vmem_limit_bytes=vmem_mib * MIB)


def _dot(a, b):
    return jnp.dot(a, b, preferred_element_type=F32)


def _ln(x):
    mu = jnp.mean(x, axis=-1, keepdims=True)
    xc = x - mu
    var = jnp.mean(xc * xc, axis=-1, keepdims=True)
    return xc * lax.rsqrt(var + LN_EPS)


def _row(ref, j):
    return ref[j:j + 1, :]


def _modulated_ln(x, mod_ref, shift, scale):
    return (_ln(x) * (1.0 + _row(mod_ref, scale)) + _row(mod_ref, shift)).astype(BF16)


EPILOGUE_ROWS = 128


def _for_row_chunks(rows, fn):
    def body(r, carry):
        fn(pl.ds(pl.multiple_of(r * EPILOGUE_ROWS, EPILOGUE_ROWS), EPILOGUE_ROWS))
        return carry

    lax.fori_loop(0, rows // EPILOGUE_ROWS, body, 0)


def _mod_spec(layer, nb):
    if nb == 1:
        return pl.BlockSpec((None, None, SUBLANES, D_MODEL), lambda b, *_: (layer, 0, 0, 0))
    return pl.BlockSpec((None, None, SUBLANES, D_MODEL), lambda b, *_: (layer, b, 0, 0))


def _mod_kernel(c_ref, w_ref, b_ref, o_ref):
    c = c_ref[...]
    sc = c * jax.nn.sigmoid(c)
    o_ref[...] = _dot(sc.astype(BF16), w_ref[...].astype(BF16)) + b_ref[...]


def _modulation(cond8, w_mod, b_mod):
    tn = 1024
    n = 6 * D_MODEL
    return pl.pallas_call(
        _mod_kernel,
        grid=(DEPTH, n // tn),
        in_specs=[
            pl.BlockSpec((SUBLANES, D_MODEL), lambda l, j: (0, 0)),
            pl.BlockSpec((None, D_MODEL, tn), lambda l, j: (l, 0, j)),
            pl.BlockSpec((None, 1, tn), lambda l, j: (l, 0, j)),
        ],
        out_specs=pl.BlockSpec((None, SUBLANES, tn), lambda l, j: (l, 0, j)),
        out_shape=jax.ShapeDtypeStruct((DEPTH, SUBLANES, n), F32),
        compiler_params=_params(("parallel", "parallel"), 40),
        name="adaln_modulation",
    )(cond8, w_mod, b_mod.reshape(DEPTH, 1, n))


def _entry_kernel(has_pos, *refs):
    if has_pos:
        x_ref, p_ref, mod_ref, xo_ref, h_ref = refs
        x = x_ref[...] + p_ref[...]
        xo_ref[...] = x
    else:
        x_ref, mod_ref, h_ref = refs
        x = x_ref[...]
    h_ref[...] = _modulated_ln(x, mod_ref, SHIFT1, SCALE1)


def _entry(x, pos, mod):
    nb, rows, _ = x.shape
    tm = 512
    has_pos = pos is not None
    blk = pl.BlockSpec((None, tm, D_MODEL), lambda i, b: (b, i, 0))
    in_specs = [blk]
    args = [x]
    if has_pos:
        in_specs.append(pl.BlockSpec((tm, D_MODEL), lambda i, b: (i, 0)))
        args.append(pos)
    in_specs.append(pl.BlockSpec((None, None, SUBLANES, D_MODEL), lambda i, b: (0, b, 0, 0)))
    args.append(mod)
    h_shape = jax.ShapeDtypeStruct(x.shape, BF16)
    out = pl.pallas_call(
        functools.partial(_entry_kernel, has_pos),
        grid=(rows // tm, nb),
        in_specs=in_specs,
        out_specs=[blk, blk] if has_pos else [blk],
        out_shape=[jax.ShapeDtypeStruct(x.shape, F32), h_shape] if has_pos else [h_shape],
        compiler_params=_params(("parallel", "parallel"), 40),
        name="entry_ln",
    )(*args)
    return out if has_pos else (x, out[0])


def _win_kernel(n_us, with_rest, h_ref, w_ref, us_ref, *rest):
    r = _dot(h_ref[...], w_ref[...].astype(BF16))
    if not with_rest:
        us_ref[...] = r
        return
    rest_ref, = rest
    j = pl.program_id(2)

    @pl.when(j < n_us)
    def _():
        us_ref[...] = r

    @pl.when(j >= n_us)
    def _():
        rest_ref[...] = r.astype(BF16)


def _in_proj(h, w_in, layer, with_rest):
    nb, rows, _ = h.shape
    tm = min(rows, 2048)
    tn = 512
    n_us = SSM_WIDTH // tn
    n_tiles = (IN_WIDTH if with_rest else SSM_WIDTH) // tn
    out_specs = [pl.BlockSpec((None, tm, tn), lambda b, i, j: (b, i, jnp.minimum(j, n_us - 1)))]
    out_shape = [jax.ShapeDtypeStruct((nb, rows, SSM_WIDTH), F32)]
    if with_rest:
        out_specs.append(pl.BlockSpec((None, tm, tn), lambda b, i, j: (b, i, jnp.maximum(j - n_us, 0))))
        out_shape.append(jax.ShapeDtypeStruct((nb, rows, REST_WIDTH), BF16))
    return pl.pallas_call(
        functools.partial(_win_kernel, n_us, with_rest),
        grid=(nb, rows // tm, n_tiles),
        in_specs=[
            pl.BlockSpec((None, tm, D_MODEL), lambda b, i, j: (b, i, 0)),
            pl.BlockSpec((None, D_MODEL, tn), lambda b, i, j: (layer, 0, j)),
        ],
        out_specs=out_specs,
        out_shape=out_shape,
        compiler_params=_params(("parallel", "parallel", "arbitrary"), 48),
        name="in_proj",
    )(h, w_in)


def _s5_kernel(n, need_y, uf_ref, ub_ref, h0_ref, pf_ref, pb_ref, pft_ref, pbt_ref, a_ref, wd_ref, wr_ref,
               *rest):
    if need_y:
        yf_ref, yb_ref, ht_ref = rest[:3]
        scr = rest[3:]
        v_re, v_im, h_re, h_im = scr[0:2], scr[2:4], scr[4:6], scr[6:8]
        hc_ref, ycf_ref, ycb_ref = scr[8:11]
    else:
        ht_ref = rest[0]
        scr = rest[1:]
        v_re, v_im = scr[0:2], scr[2:4]
        hc_ref = scr[4]
    g = pl.program_id(0)
    rows_in = BATCH * SCAN_T

    @pl.when(g == 0)
    def _():
        hc_ref[...] = h0_ref[...]
        for buf in scr[:8 if need_y else 4]:
            buf[...] = jnp.zeros_like(buf)

    scan_valid = jnp.logical_and(g >= 1, g <= n)

    def stages(par):
        uf = uf_ref[...].reshape(rows_in, SSM_WIDTH).astype(BF16)
        ub = ub_ref[...].reshape(rows_in, SSM_WIDTH).astype(BF16)
        up_f = _dot(pf_ref[...], uf).astype(BF16)
        up_b = _dot(pb_ref[...], ub).astype(BF16)
        for k in range(S5_NK):
            cs = slice(k * LANES, (k + 1) * LANES)
            ss = slice(k * S5_KSTATE, (k + 1) * S5_KSTATE)
            lhs = jnp.concatenate([up_f[:, cs], up_b[:, cs]], axis=1)
            v = _dot(lhs, wd_ref[k])
            v_re[par][:, ss] = v[:, :S5_KSTATE]
            v_im[par][:, ss] = v[:, S5_KSTATE:]
            ar = a_ref[0, :, ss]
            ai = a_ref[1, :, ss]
            hr0 = hc_ref[0, :, ss]
            hi0 = hc_ref[1, :, ss]
            hr, hi = hr0, hi0
            for s in range(SCAN_T):
                rs = slice(s * SUBLANES, (s + 1) * SUBLANES)
                nr = ar * hr - ai * hi + v_re[1 - par][rs, ss]
                ni = ar * hi + ai * hr + v_im[1 - par][rs, ss]
                if need_y:
                    h_re[1 - par][rs, ss] = nr
                    h_im[1 - par][rs, ss] = ni
                hr, hi = nr, ni
            hc_ref[0, :, ss] = jnp.where(scan_valid, hr, hr0)
            hc_ref[1, :, ss] = jnp.where(scan_valid, hi, hi0)
            if need_y:
                h = jnp.concatenate([h_re[par][:, ss].astype(BF16), h_im[par][:, ss].astype(BF16)], axis=1)
                y = _dot(h, wr_ref[k])
                ycf_ref[:, cs] = y[:, :LANES]
                ycb_ref[:, cs] = y[:, LANES:]
        if need_y:
            yf = _dot(pft_ref[...], ycf_ref[...].astype(BF16))
            yb = _dot(pbt_ref[...], ycb_ref[...].astype(BF16))
            yf_ref[...] = yf.astype(BF16).reshape(BATCH, SCAN_T, SSM_WIDTH)
            yb_ref[...] = yb.astype(BF16).reshape(BATCH, SCAN_T, SSM_WIDTH)

    @pl.when(g % 2 == 0)
    def _():
        stages(0)

    @pl.when(g % 2 == 1)
    def _():
        stages(1)

    @pl.when(g == pl.num_programs(0) - 1)
    def _():
        ht_ref[...] = hc_ref[...]


def _scan_perms():
    t = SCAN_T
    pf = np.zeros((2 * BATCH * t, BATCH * t), np.float32)
    pb = np.zeros((2 * BATCH * t, BATCH * t), np.float32)
    for s in range(t):
        for b in range(BATCH):
            pf[s * 2 * BATCH + b, b * t + s] = 1.0
            pb[s * 2 * BATCH + BATCH + b, b * t + (t - 1 - s)] = 1.0
    return pf, pb


_PF, _PB = _scan_perms()


def _s5(u, h0, prm, need_y):
    a, wd, wr = prm
    steps = u.shape[1]
    n = steps // SCAN_T
    rows_in = BATCH * SCAN_T
    rows_sc = 2 * rows_in
    pf = jnp.asarray(_PF, BF16)
    pb = jnp.asarray(_PB, BF16)
    const2 = lambda g: (0, 0)
    once = pl.Buffered(1)
    blk = (BATCH, SCAN_T, SSM_WIDTH)
    drive_f = pl.BlockSpec(blk, lambda g: (0, jnp.minimum(g, n - 1), 0))
    drive_b = pl.BlockSpec(blk, lambda g: (0, jnp.maximum(n - 1 - g, 0), 0))
    read_f = pl.BlockSpec(blk, lambda g: (0, jnp.clip(g - 2, 0, n - 1), 0))
    read_b = pl.BlockSpec(blk, lambda g: (0, jnp.clip(n + 1 - g, 0, n - 1), 0))
    state_spec = pl.BlockSpec((2, SUBLANES, STATE_W), lambda g: (0, 0, 0))
    in_specs = [
        drive_f, drive_b, state_spec,
        pl.BlockSpec((rows_sc, rows_in), const2),
        pl.BlockSpec((rows_sc, rows_in), const2),
        pl.BlockSpec((rows_in, rows_sc), const2),
        pl.BlockSpec((rows_in, rows_sc), const2),
        state_spec,
        pl.BlockSpec((S5_NK, 2 * LANES, 2 * S5_KSTATE), lambda g: (0, 0, 0), pipeline_mode=once),
        pl.BlockSpec((S5_NK, 2 * S5_KSTATE, 2 * LANES), lambda g: (0, 0, 0), pipeline_mode=once),
    ]
    state_shape = jax.ShapeDtypeStruct((2, SUBLANES, STATE_W), F32)
    chunk_buf = pltpu.VMEM((rows_sc, STATE_W), F32)
    carry_buf = pltpu.VMEM((2, SUBLANES, STATE_W), F32)
    if need_y:
        out_specs = [read_f, read_b, state_spec]
        y_shape = jax.ShapeDtypeStruct((BATCH, steps, SSM_WIDTH), BF16)
        out_shape = [y_shape, y_shape, state_shape]
        scratch = [chunk_buf] * 8 + [carry_buf, pltpu.VMEM((rows_sc, SSM_WIDTH), F32),
                                     pltpu.VMEM((rows_sc, SSM_WIDTH), F32)]
    else:
        out_specs = [state_spec]
        out_shape = [state_shape]
        scratch = [chunk_buf] * 4 + [carry_buf]
    return pl.pallas_call(
        functools.partial(_s5_kernel, n, need_y),
        grid=(n + 2,),
        in_specs=in_specs,
        out_specs=out_specs,
        out_shape=out_shape,
        scratch_shapes=scratch,
        compiler_params=_params(("arbitrary",), 52),
        name="s5_scan",
    )(u, u, h0, pf, pb, pf.T, pb.T, a, wd, wr)


def _s5_params(lam_re, lam_im, log_step, b_re, b_im, c_re, c_im):
    dt = jnp.exp(log_step)[..., None]
    mag = jnp.exp(lam_re * dt)
    ang = lam_im * dt
    abar_re, abar_im = mag * jnp.cos(ang), mag * jnp.sin(ang)
    den = lam_re * lam_re + lam_im * lam_im
    nr, ni = abar_re - 1.0, abar_im
    coef_re = (nr * lam_re + ni * lam_im) / den
    coef_im = (ni * lam_re - nr * lam_im) / den
    bb_re = coef_re[..., None] * b_re - coef_im[..., None] * b_im
    bb_im = coef_re[..., None] * b_im + coef_im[..., None] * b_re
    shp = (2, S5_NK, S5_KGROUPS, SSM_STATE, SSM_GROUP)
    bb = jnp.stack([bb_re.reshape(shp), bb_im.reshape(shp)])
    bb = jnp.tile(jnp.transpose(bb, (2, 1, 3, 5, 0, 4)), (1, 1, 1, 1, 1, S5_KGROUPS))
    own = jnp.arange(S5_KSTATE)[None, :] // SSM_STATE == jnp.arange(S5_KGROUPS)[:, None]
    wd = jnp.where(own[None, None, :, None, None, :], bb, 0.0)
    wd = wd.reshape(S5_NK, 2 * LANES, 2 * S5_KSTATE).astype(BF16)
    shp = (2, S5_NK, S5_KGROUPS, SSM_GROUP, SSM_STATE)
    cc = jnp.stack([c_re.reshape(shp), -c_im.reshape(shp)])
    cc = jnp.tile(jnp.transpose(cc, (2, 0, 3, 5, 1, 4)), (1, 1, 1, 1, 1, S5_KGROUPS))
    own = jnp.arange(LANES)[None, :] // SSM_GROUP == jnp.arange(S5_KGROUPS)[:, None]
    wr = jnp.where(own[None, None, :, None, None, :], cc, 0.0)
    wr = wr.reshape(S5_NK, 2 * S5_KSTATE, 2 * LANES).astype(BF16)
    a = jnp.stack([abar_re.reshape(2, STATE_W), abar_im.reshape(2, STATE_W)])
    a = jnp.repeat(a, BATCH, axis=1)
    return a, wd, wr


def _dft_tables(n):
    j = np.arange(n, dtype=np.int64)
    ang = 2.0 * np.pi * ((j[:, None] * j[None, :]) % n).astype(np.float64) / n
    s = 1.0 / math.sqrt(n)
    return (np.cos(ang) * s).astype(np.float32), (np.sin(ang) * s).astype(np.float32)


_CH_COS, _CH_SIN = _dft_tables(FFT_GROUP)
_CH_CS = np.concatenate([_CH_COS, _CH_SIN], axis=1)
_POS_TABLES = {n: _dft_tables(n) for n in (CTX_LEN, SEQ)}


def _chdft_kernel(u_ref, w_ref, p_ref, q_ref):
    r = _dot(u_ref[...], w_ref[...].astype(BF16))
    p_ref[...] = r[:, :FFT_GROUP].astype(BF16)
    q_ref[...] = r[:, FFT_GROUP:].astype(BF16)


def _channel_dft(rest):
    nb, rows, _ = rest.shape
    tm = min(rows, 2048)
    shape = jax.ShapeDtypeStruct((nb, rows, FFT_WIDTH), BF16)
    blk = pl.BlockSpec((None, tm, FFT_GROUP), lambda b, i, g: (b, i, g))
    return pl.pallas_call(
        _chdft_kernel,
        grid=(nb, rows // tm, FFT_GROUPS),
        in_specs=[blk, pl.BlockSpec((FFT_GROUP, 2 * FFT_GROUP), lambda b, i, g: (0, 0))],
        out_specs=[blk, blk],
        out_shape=[shape, shape],
        compiler_params=_params(("parallel", "parallel", "parallel"), 32),
        name="channel_dft",
    )(rest, jnp.asarray(_CH_CS))


def _posdft_kernel(nk, ac_ref, as_ref, p_ref, q_ref, o_ref, acc_ref):
    k = pl.program_id(3)

    @pl.when(k == 0)
    def _():
        acc_ref[...] = jnp.zeros_like(acc_ref)

    acc_ref[...] += (_dot(ac_ref[...].astype(BF16), p_ref[...])
                     - _dot(as_ref[...].astype(BF16), q_ref[...]))

    @pl.when(k == nk - 1)
    def _():
        o_ref[...] = acc_ref[...].astype(BF16)


def _position_dft(p, q):
    nb, steps, _ = p.shape
    cos, sin = _POS_TABLES[steps]
    tm = min(steps, 2048)
    tk = min(steps, 512)
    tn = FFT_WIDTH
    nk = steps // tk
    a_blk = pl.BlockSpec((tm, tk), lambda b, i, j, k: (i, k))
    x_blk = pl.BlockSpec((None, tk, tn), lambda b, i, j, k: (b, k, j))
    return pl.pallas_call(
        functools.partial(_posdft_kernel, nk),
        grid=(nb, steps // tm, FFT_WIDTH // tn, nk),
        in_specs=[a_blk, a_blk, x_blk, x_blk],
        out_specs=pl.BlockSpec((None, tm, tn), lambda b, i, j, k: (b, i, j)),
        out_shape=jax.ShapeDtypeStruct((nb, steps, FFT_WIDTH), BF16),
        scratch_shapes=[pltpu.VMEM((tm, tn), F32)],
        compiler_params=_params(("parallel", "parallel", "parallel", "arbitrary"), 48),
        name="position_dft",
    )(jnp.asarray(cos), jnp.asarray(sin), p, q)


def _glu_kernel(us_ref, yf_ref, yb_ref, dsk_ref, w_ref, o_ref):
    ys = dsk_ref[...] * us_ref[...] + yf_ref[...].astype(F32) + yb_ref[...].astype(F32)
    g = jax.nn.gelu(ys)
    z = _dot(g.astype(BF16), w_ref[...].astype(BF16))
    o_ref[...] = (g * jax.nn.sigmoid(z)).astype(BF16)


def _glu(us, yf, yb, d_skip, w_glu, layer):
    nb, rows, _ = us.shape
    tm = 1024
    blk = pl.BlockSpec((None, tm, SSM_WIDTH), lambda b, i: (b, i, 0))
    return pl.pallas_call(
        _glu_kernel,
        grid=(nb, rows // tm),
        in_specs=[blk, blk, blk,
                  pl.BlockSpec((None, 1, SSM_WIDTH), lambda b, i: (layer, 0, 0)),
                  pl.BlockSpec((None, SSM_WIDTH, SSM_WIDTH), lambda b, i: (layer, 0, 0))],
        out_specs=blk,
        out_shape=jax.ShapeDtypeStruct((nb, rows, SSM_WIDTH), BF16),
        compiler_params=_params(("parallel", "parallel"), 40),
        name="s5_glu",
    )(us, yf, yb, d_skip.reshape(DEPTH, 1, SSM_WIDTH), w_glu)


def _merge_kernel(s_ref, yq_ref, gs_ref, gf_ref, wps_ref, wpf_ref, o_ref):
    ps = _dot(s_ref[...], wps_ref[...].astype(BF16))
    pf = _dot(yq_ref[...], wpf_ref[...].astype(BF16))
    o_ref[...] = (jax.nn.sigmoid(gs_ref[...].astype(F32)) * ps
                  + jax.nn.sigmoid(gf_ref[...].astype(F32)) * pf).astype(BF16)


def _merge(s, yq, rest, w_ps, w_pf, layer):
    nb, rows, _ = s.shape
    tm = min(rows, 2048)
    tn = 512
    gs_off = FFT_WIDTH // tn
    gf_off = (FFT_WIDTH + D_MODEL) // tn
    row_blk = pl.BlockSpec((None, tm, SSM_WIDTH), lambda b, i, j: (b, i, 0))
    w_blk = pl.BlockSpec((None, SSM_WIDTH, tn), lambda b, i, j: (layer, 0, j))
    return pl.pallas_call(
        _merge_kernel,
        grid=(nb, rows // tm, D_MODEL // tn),
        in_specs=[
            row_blk, row_blk,
            pl.BlockSpec((None, tm, tn), lambda b, i, j: (b, i, gs_off + j)),
            pl.BlockSpec((None, tm, tn), lambda b, i, j: (b, i, gf_off + j)),
            w_blk, w_blk,
        ],
        out_specs=pl.BlockSpec((None, tm, tn), lambda b, i, j: (b, i, j)),
        out_shape=jax.ShapeDtypeStruct((nb, rows, D_MODEL), BF16),
        compiler_params=_params(("parallel", "parallel", "arbitrary"), 48),
        name="gated_merge",
    )(s, yq, rest, rest, w_ps, w_pf)


def _wo_kernel(nk, mg_ref, wo_ref, x_ref, mod_ref, g_ref, b_ref, o_ref, h_ref):
    k = pl.program_id(2)

    @pl.when(k == 0)
    def _():
        o_ref[...] = jnp.zeros_like(o_ref)

    o_ref[...] += _dot(mg_ref[...], wo_ref[...].astype(BF16))

    @pl.when(k == nk - 1)
    def _():
        def finish(rs):
            y = ALPHA * x_ref[rs, :] + _row(mod_ref, GATE1) * o_ref[rs, :]
            x1 = _ln(y) * g_ref[...] + b_ref[...]
            o_ref[rs, :] = x1
            h_ref[rs, :] = _modulated_ln(x1, mod_ref, SHIFT2, SCALE2)

        _for_row_chunks(o_ref.shape[0], finish)


def _out_proj_ln(merged, w_o, x, mod, g, b, layer):
    nb, rows, _ = x.shape
    tm, tk = 1024, 512
    nk = D_MODEL // tk
    row_blk = pl.BlockSpec((None, tm, D_MODEL), lambda b_, i, k: (b_, i, 0))
    vec = pl.BlockSpec((None, 1, D_MODEL), lambda b_, i, k: (layer, 0, 0))
    return pl.pallas_call(
        functools.partial(_wo_kernel, nk),
        grid=(nb, rows // tm, nk),
        in_specs=[
            pl.BlockSpec((None, tm, tk), lambda b_, i, k: (b_, i, k)),
            pl.BlockSpec((None, tk, D_MODEL), lambda b_, i, k: (layer, k, 0)),
            pl.BlockSpec((None, tm, D_MODEL), lambda b_, i, k: (b_, i, 0), pipeline_mode=pl.Buffered(1)),
            _mod_spec(layer, nb), vec, vec,
        ],
        out_specs=[row_blk, row_blk],
        out_shape=[jax.ShapeDtypeStruct(x.shape, F32), jax.ShapeDtypeStruct(x.shape, BF16)],
        compiler_params=_params(("parallel", "parallel", "arbitrary"), 52),
        name="out_proj_ln",
    )(merged, w_o, x, mod, g.reshape(DEPTH, 1, D_MODEL), b.reshape(DEPTH, 1, D_MODEL))


def _mlp_kernel(nf, emit_next, h_ref, x_ref, mod_ref, *refs):
    if emit_next:
        modn_ref, wu_ref, wd_ref, g_ref, b_ref, o_ref, hn_ref = refs
    else:
        wu_ref, wd_ref, g_ref, b_ref, o_ref = refs
    f = pl.program_id(2)

    @pl.when(f == 0)
    def _():
        o_ref[...] = jnp.zeros_like(o_ref)

    a = jnp.maximum(_dot(h_ref[...], wu_ref[...].astype(BF16)), 0.0)
    o_ref[...] += _dot((a * a).astype(BF16), wd_ref[...].astype(BF16))

    @pl.when(f == nf - 1)
    def _():
        def finish(rs):
            y = ALPHA * x_ref[rs, :] + _row(mod_ref, GATE2) * o_ref[rs, :]
            x2 = _ln(y) * g_ref[...] + b_ref[...]
            o_ref[rs, :] = x2
            if emit_next:
                hn_ref[rs, :] = _modulated_ln(x2, modn_ref, SHIFT1, SCALE1)

        _for_row_chunks(o_ref.shape[0], finish)


def _mlp(h2, x1, mod, w_up, w_down, g, b, layer, emit_next):
    nb, rows, _ = x1.shape
    tm, tf = 1024, 512
    nf = D_FF // tf
    row_blk = pl.BlockSpec((None, tm, D_MODEL), lambda b_, i, f: (b_, i, 0))
    vec = pl.BlockSpec((None, 1, D_MODEL), lambda b_, i, f: (layer, 0, 0))
    in_specs = [
        row_blk,
        pl.BlockSpec((None, tm, D_MODEL), lambda b_, i, f: (b_, i, 0), pipeline_mode=pl.Buffered(1)),
        _mod_spec(layer, nb),
    ]
    args = [h2, x1, mod]
    if emit_next:
        in_specs.append(_mod_spec(layer + 1, nb))
        args.append(mod)
    in_specs += [
        pl.BlockSpec((None, D_MODEL, tf), lambda b_, i, f: (layer, 0, f)),
        pl.BlockSpec((None, tf, D_MODEL), lambda b_, i, f: (layer, f, 0)),
        vec, vec,
    ]
    args += [w_up, w_down, g.reshape(DEPTH, 1, D_MODEL), b.reshape(DEPTH, 1, D_MODEL)]
    out_specs = [pl.BlockSpec((None, tm, D_MODEL), lambda b_, i, f: (b_, i, 0), pipeline_mode=pl.Buffered(1))]
    out_shape = [jax.ShapeDtypeStruct(x1.shape, F32)]
    if emit_next:
        out_specs.append(row_blk)
        out_shape.append(jax.ShapeDtypeStruct(x1.shape, BF16))
    out = pl.pallas_call(
        functools.partial(_mlp_kernel, nf, emit_next),
        grid=(nb, rows // tm, nf),
        in_specs=in_specs,
        out_specs=out_specs,
        out_shape=out_shape,
        compiler_params=_params(("parallel", "parallel", "arbitrary"), 56),
        name="mlp",
    )(*args)
    return out if emit_next else (out[0], None)


TAIL_TM = 1024
TAIL_TK = 512
TAIL_TF = 512


def _tail_kernel(layer, emit_next, merged_ref, x_hbm, wo_hbm, wu_hbm, wd_hbm, mod_ref, *refs):
    if emit_next:
        modn_ref, refs = refs[0], refs[1:]
    g1_ref, b1_ref, g2_ref, b2_ref, out_hbm = refs[:5]
    refs = refs[5:]
    if emit_next:
        hn_ref, refs = refs[0], refs[1:]
    xbuf, acc, h2, wa, wb, sem_a, sem_b, sem_x, sem_o = refs
    b = pl.program_id(0)
    i = pl.program_id(1)
    first = jnp.logical_and(b == 0, i == 0)
    last = jnp.logical_and(b == pl.num_programs(0) - 1, i == pl.num_programs(1) - 1)
    nk = D_MODEL // TAIL_TK
    nf = D_FF // TAIL_TF
    assert nk % 2 == 0
    rows = pl.ds(pl.multiple_of(i * TAIL_TM, TAIL_TM), TAIL_TM)

    def wo_copy(t, slot):
        return pltpu.make_async_copy(wo_hbm.at[layer, pl.ds(t * TAIL_TK, TAIL_TK), :], wa.at[slot], sem_a.at[slot])

    def wd_copy(f, slot):
        src = wd_hbm.at[layer, pl.ds(pl.multiple_of(f * TAIL_TK, TAIL_TK), TAIL_TK), :]
        return pltpu.make_async_copy(src, wa.at[slot], sem_a.at[slot])

    def wu_copy(f, slot):
        src = wu_hbm.at[layer, :, pl.ds(pl.multiple_of(f * TAIL_TF, TAIL_TF), TAIL_TF)]
        return pltpu.make_async_copy(src, wb.at[slot], sem_b.at[slot])

    def x_copy():
        return pltpu.make_async_copy(x_hbm.at[b, rows, :], xbuf, sem_x.at[0])

    def out_copy():
        return pltpu.make_async_copy(xbuf, out_hbm.at[b, rows, :], sem_o.at[0])

    wo_copy(0, 0).start()
    wo_copy(1, 1).start()
    wu_copy(0, 0).start()
    wu_copy(1, 1).start()

    for k in range(nk):
        slot = k % 2
        wo_copy(k, slot).wait()
        lhs = merged_ref[:, k * TAIL_TK:(k + 1) * TAIL_TK]
        if k == 0:
            acc[...] = _dot(lhs, wa[slot].astype(BF16))
        else:
            acc[...] += _dot(lhs, wa[slot].astype(BF16))
        if k + 2 < nk:
            wo_copy(k + 2, slot).start()
        else:
            wd_copy(k + 2 - nk, slot).start()
        if k == 0:
            @pl.when(jnp.logical_not(first))
            def _():
                out_copy().wait()

            x_copy().start()
    x_copy().wait()

    def finish1(rs):
        y = ALPHA * xbuf[rs, :] + _row(mod_ref, GATE1) * acc[rs, :]
        x1 = _ln(y) * g1_ref[...] + b1_ref[...]
        xbuf[rs, :] = x1
        h2[rs, :] = _modulated_ln(x1, mod_ref, SHIFT2, SCALE2)
        acc[rs, :] = jnp.zeros((EPILOGUE_ROWS, D_MODEL), F32)

    _for_row_chunks(TAIL_TM, finish1)

    def mlp_tile(f, carry):
        slot = f % 2
        wu_copy(f, slot).wait()
        a = jnp.maximum(_dot(h2[...], wb[slot].astype(BF16)), 0.0)
        a = (a * a).astype(BF16)

        @pl.when(f + 2 < nf)
        def _():
            wu_copy(f + 2, slot).start()

        wd_copy(f, slot).wait()
        acc[...] += _dot(a, wa[slot].astype(BF16))

        @pl.when(f + 2 < nf)
        def _():
            wd_copy(f + 2, slot).start()

        return carry

    lax.fori_loop(0, nf, mlp_tile, 0)

    def finish2(rs):
        y = ALPHA * xbuf[rs, :] + _row(mod_ref, GATE2) * acc[rs, :]
        x2 = _ln(y) * g2_ref[...] + b2_ref[...]
        xbuf[rs, :] = x2
        if emit_next:
            hn_ref[rs, :] = _modulated_ln(x2, modn_ref, SHIFT1, SCALE1)

    _for_row_chunks(TAIL_TM, finish2)
    out_copy().start()

    @pl.when(last)
    def _():
        out_copy().wait()


def _tail(merged, x, mod, w_o, w_up, w_down, g1, b1, g2, b2, layer, emit_next):
    nb, rows, _ = x.shape
    tm = TAIL_TM
    row_blk = pl.BlockSpec((None, tm, D_MODEL), lambda b_, i: (b_, i, 0))
    vec = pl.BlockSpec((None, 1, D_MODEL), lambda b_, i: (layer, 0, 0))
    hbm = pl.BlockSpec(memory_space=pl.ANY)
    in_specs = [row_blk, hbm, hbm, hbm, hbm, _mod_spec(layer, nb)]
    args = [merged, x, w_o, w_up, w_down, mod]
    if emit_next:
        in_specs.append(_mod_spec(layer + 1, nb))
        args.append(mod)
    in_specs += [vec, vec, vec, vec]
    args += [v.reshape(DEPTH, 1, D_MODEL) for v in (g1, b1, g2, b2)]
    out_specs = [hbm]
    out_shape = [jax.ShapeDtypeStruct(x.shape, F32)]
    if emit_next:
        out_specs.append(row_blk)
        out_shape.append(jax.ShapeDtypeStruct(x.shape, BF16))
    out = pl.pallas_call(
        functools.partial(_tail_kernel, layer, emit_next),
        grid=(nb, rows // tm),
        in_specs=in_specs,
        out_specs=out_specs,
        out_shape=out_shape,
        scratch_shapes=[
            pltpu.VMEM((tm, D_MODEL), F32),
            pltpu.VMEM((tm, D_MODEL), F32),
            pltpu.VMEM((tm, D_MODEL), BF16),
            pltpu.VMEM((2, TAIL_TK, D_MODEL), F32),
            pltpu.VMEM((2, D_MODEL, TAIL_TF), F32),
            pltpu.SemaphoreType.DMA((2,)),
            pltpu.SemaphoreType.DMA((2,)),
            pltpu.SemaphoreType.DMA((1,)),
            pltpu.SemaphoreType.DMA((1,)),
        ],
        compiler_params=_params(("arbitrary", "arbitrary"), 58),
        name="layer_tail",
    )(*args)
    return out if emit_next else (out[0], None)


def _pos_table():
    quarter = D_MODEL // 4
    omega = 1.0 / (POS_BASE ** (np.arange(quarter, dtype=np.float64) / quarter))
    t = np.arange(SEQ)
    ar = (t // GRID_W).astype(np.float64)[:, None] * omega
    ac = (t % GRID_W).astype(np.float64)[:, None] * omega
    return np.concatenate([np.sin(ar), np.cos(ar), np.sin(ac), np.cos(ac)], axis=-1).astype(np.float32)


_POS = _pos_table()


def _mod_tables(m):
    m = m.reshape(DEPTH, SUBLANES, 6, D_MODEL)
    m = jnp.pad(m, ((0, 0), (0, 0), (0, SUBLANES - 6), (0, 0)))
    return m[:, :BATCH], m[:, BATCH:BATCH + 1]


def _as_batch(a):
    return a.reshape(BATCH, CTX_LEN, a.shape[-1])


def _as_slab(a):
    return a.reshape(1, BATCH * CTX_LEN, a.shape[-1])


def kernel(x, c, ctx, c_ctx, w_mod, b_mod, w_in, lam_re, lam_im, log_step, ssm_b_re, ssm_b_im, ssm_c_re,
           ssm_c_im, d_skip, w_glu, w_ps, w_pf, w_o, ln1_g, ln1_b, w_up, w_down, ln2_g, ln2_b):
    cond8 = jnp.concatenate([c, c_ctx[None], jnp.zeros((SUBLANES - BATCH - 1, D_MODEL), F32)], axis=0)
    mod_lat, mod_ctx = _mod_tables(_modulation(cond8, w_mod, b_mod))

    x_lat, h_lat = _entry(x, jnp.asarray(_POS), mod_lat)
    x_ctx, h_ctx = _entry(_as_slab(ctx), None, mod_ctx)
    h_zero = jnp.zeros((2, SUBLANES, STATE_W), F32)

    for l in range(DEPTH):
        need_ctx = l < DEPTH - 1
        prm = _s5_params(lam_re[l], lam_im[l], log_step[l], ssm_b_re[l], ssm_b_im[l], ssm_c_re[l],
                         ssm_c_im[l])

        def mixer_tail(us, rest, yf, yb, xx, mod, as_batch, as_rows):
            p, q = _channel_dft(rest)
            yq = as_rows(_position_dft(as_batch(p), as_batch(q)))
            s = _glu(us, as_rows(yf), as_rows(yb), d_skip, w_glu, l)
            merged = _merge(s, yq, rest, w_ps, w_pf, l)
            return _tail(merged, xx, mod, w_o, w_up, w_down, ln1_g, ln1_b, ln2_g, ln2_b, l, need_ctx)

        us_lat, rest_lat = _in_proj(h_lat, w_in, l, True)
        if need_ctx:
            us_ctx, rest_ctx = _in_proj(h_ctx, w_in, l, True)
            yf_c, yb_c, h_t = _s5(_as_batch(us_ctx), h_zero, prm, True)
        else:
            (us_ctx,) = _in_proj(h_ctx, w_in, l, False)
            (h_t,) = _s5(_as_batch(us_ctx), h_zero, prm, False)
        yf, yb, _ = _s5(us_lat, h_t, prm, True)
        ident = lambda a: a
        x_lat, h_lat = mixer_tail(us_lat, rest_lat, yf, yb, x_lat, mod_lat, ident, ident)
        if need_ctx:
            x_ctx, h_ctx = mixer_tail(us_ctx, rest_ctx, yf_c, yb_c, x_ctx, mod_ctx, _as_batch, _as_slab)

    return x_lat
```

```python
import functools
import math

import numpy as np
import jax
import jax.numpy as jnp
from jax import lax
from jax.experimental import pallas as pl
from jax.experimental.pallas import tpu as pltpu

D_MODEL = 2048
BATCH = 4
SEQ = 2048
DEPTH = 2
GRID_W = 64
CTX_LEN = 256
SSM_WIDTH = D_MODEL // 2
SSM_GROUP = 16
SSM_GROUPS = SSM_WIDTH // SSM_GROUP
SSM_STATE = 64
FFT_WIDTH = D_MODEL - SSM_WIDTH
FFT_GROUPS = 4
FFT_GROUP = FFT_WIDTH // FFT_GROUPS
IN_WIDTH = SSM_WIDTH + FFT_WIDTH + 2 * D_MODEL
REST_WIDTH = IN_WIDTH - SSM_WIDTH
D_FF = 4 * D_MODEL
ALPHA = (2 * DEPTH) ** 0.25
LN_EPS = 1e-5
POS_BASE = 10000.0

F32 = jnp.float32
BF16 = jnp.bfloat16

SUBLANES = 8
LANES = 128
STATE_W = SSM_GROUPS * SSM_STATE
SCAN_T = 32
S5_KGROUPS = 8
S5_NK = SSM_GROUPS // S5_KGROUPS
S5_KSTATE = S5_KGROUPS * SSM_STATE
SCAN_COLS = 512
MIB = 1024 * 1024

SHIFT1, SCALE1, GATE1, SHIFT2, SCALE2, GATE2 = range(6)


def _params(sem, vmem_mib):
    return pltpu.CompilerParams(dimension_semantics=sem, vmem_limit_bytes=vmem_mib * MIB)


def _dot(a, b):
    return jnp.dot(a, b, preferred_element_type=F32)


def _ln(x):
    mu = jnp.mean(x, axis=-1, keepdims=True)
    xc = x - mu
    var = jnp.mean(xc * xc, axis=-1, keepdims=True)
    return xc * lax.rsqrt(var + LN_EPS)


def _row(ref, j):
    return ref[j:j + 1, :]


def _modulated_ln(x, mod_ref, shift, scale):
    return (_ln(x) * (1.0 + _row(mod_ref, scale)) + _row(mod_ref, shift)).astype(BF16)


EPILOGUE_ROWS = 128


def _for_row_chunks(rows, fn):
    def body(r, carry):
        fn(pl.ds(pl.multiple_of(r * EPILOGUE_ROWS, EPILOGUE_ROWS), EPILOGUE_ROWS))
        return carry

    lax.fori_loop(0, rows // EPILOGUE_ROWS, body, 0)


def _mod_spec(layer, nb):
    if nb == 1:
        return pl.BlockSpec((None, None, SUBLANES, D_MODEL), lambda b, *_: (layer, 0, 0, 0))
    return pl.BlockSpec((None, None, SUBLANES, D_MODEL), lambda b, *_: (layer, b, 0, 0))


def _mod_kernel(c_ref, w_ref, b_ref, o_ref):
    c = c_ref[...]
    sc = c * jax.nn.sigmoid(c)
    o_ref[...] = _dot(sc.astype(BF16), w_ref[...].astype(BF16)) + b_ref[...]


def _modulation(cond8, w_mod, b_mod):
    tn = 1024
    n = 6 * D_MODEL
    return pl.pallas_call(
        _mod_kernel,
        grid=(DEPTH, n // tn),
        in_specs=[
            pl.BlockSpec((SUBLANES, D_MODEL), lambda l, j: (0, 0)),
            pl.BlockSpec((None, D_MODEL, tn), lambda l, j: (l, 0, j)),
            pl.BlockSpec((None, 1, tn), lambda l, j: (l, 0, j)),
        ],
        out_specs=pl.BlockSpec((None, SUBLANES, tn), lambda l, j: (l, 0, j)),
        out_shape=jax.ShapeDtypeStruct((DEPTH, SUBLANES, n), F32),
        compiler_params=_params(("parallel", "parallel"), 40),
        name="adaln_modulation",
    )(cond8, w_mod, b_mod.reshape(DEPTH, 1, n))


def _entry_kernel(has_pos, *refs):
    if has_pos:
        x_ref, p_ref, mod_ref, xo_ref, h_ref = refs
        x = x_ref[...] + p_ref[...]
        xo_ref[...] = x
    else:
        x_ref, mod_ref, h_ref = refs
        x = x_ref[...]
    h_ref[...] = _modulated_ln(x, mod_ref, SHIFT1, SCALE1)


def _entry(x, pos, mod):
    nb, rows, _ = x.shape
    tm = 512
    has_pos = pos is not None
    blk = pl.BlockSpec((None, tm, D_MODEL), lambda i, b: (b, i, 0))
    in_specs = [blk]
    args = [x]
    if has_pos:
        in_specs.append(pl.BlockSpec((tm, D_MODEL), lambda i, b: (i, 0)))
        args.append(pos)
    in_specs.append(pl.BlockSpec((None, None, SUBLANES, D_MODEL), lambda i, b: (0, b, 0, 0)))
    args.append(mod)
    h_shape = jax.ShapeDtypeStruct(x.shape, BF16)
    out = pl.pallas_call(
        functools.partial(_entry_kernel, has_pos),
        grid=(rows // tm, nb),
        in_specs=in_specs,
        out_specs=[blk, blk] if has_pos else [blk],
        out_shape=[jax.ShapeDtypeStruct(x.shape, F32), h_shape] if has_pos else [h_shape],
        compiler_params=_params(("parallel", "parallel"), 40),
        name="entry_ln",
    )(*args)
    return out if has_pos else (x, out[0])


def _win_kernel(n_us, with_rest, h_ref, w_ref, us_ref, *rest):
    r = _dot(h_ref[...], w_ref[...].astype(BF16))
    if not with_rest:
        us_ref[...] = r
        return
    rest_ref, = rest
    j = pl.program_id(2)

    @pl.when(j < n_us)
    def _():
        us_ref[...] = r

    @pl.when(j >= n_us)
    def _():
        rest_ref[...] = r.astype(BF16)


def _in_proj(h, w_in, layer, with_rest):
    nb, rows, _ = h.shape
    tm = min(rows, 2048)
    tn = 512
    n_us = SSM_WIDTH // tn
    n_tiles = (IN_WIDTH if with_rest else SSM_WIDTH) // tn
    out_specs = [pl.BlockSpec((None, tm, tn), lambda b, i, j: (b, i, jnp.minimum(j, n_us - 1)))]
    out_shape = [jax.ShapeDtypeStruct((nb, rows, SSM_WIDTH), F32)]
    if with_rest:
        out_specs.append(pl.BlockSpec((None, tm, tn), lambda b, i, j: (b, i, jnp.maximum(j - n_us, 0))))
        out_shape.append(jax.ShapeDtypeStruct((nb, rows, REST_WIDTH), BF16))
    return pl.pallas_call(
        functools.partial(_win_kernel, n_us, with_rest),
        grid=(nb, rows // tm, n_tiles),
        in_specs=[
            pl.BlockSpec((None, tm, D_MODEL), lambda b, i, j: (b, i, 0)),
            pl.BlockSpec((None, D_MODEL, tn), lambda b, i, j: (layer, 0, j)),
        ],
        out_specs=out_specs,
        out_shape=out_shape,
        compiler_params=_params(("parallel", "parallel", "arbitrary"), 48),
        name="in_proj",
    )(h, w_in)


def _s5_kernel(n, need_y, uf_ref, ub_ref, h0_ref, pf_ref, pb_ref, pft_ref, pbt_ref, a_ref, wd_ref, wr_ref,
               *rest):
    if need_y:
        yf_ref, yb_ref, ht_ref = rest[:3]
        scr = rest[3:]
        v_re, v_im, h_re, h_im = scr[0:2], scr[2:4], scr[4:6], scr[6:8]
        hc_ref, ycf_ref, ycb_ref = scr[8:11]
    else:
        ht_ref = rest[0]
        scr = rest[1:]
        v_re, v_im = scr[0:2], scr[2:4]
        hc_ref = scr[4]
    g = pl.program_id(0)
    rows_in = BATCH * SCAN_T

    @pl.when(g == 0)
    def _():
        hc_ref[...] = h0_ref[...]
        for buf in scr[:8 if need_y else 4]:
            buf[...] = jnp.zeros_like(buf)

    scan_valid = jnp.logical_and(g >= 1, g <= n)

    def stages(par):
        uf = uf_ref[...].reshape(rows_in, SSM_WIDTH).astype(BF16)
        ub = ub_ref[...].reshape(rows_in, SSM_WIDTH).astype(BF16)
        up_f = _dot(pf_ref[...], uf).astype(BF16)
        up_b = _dot(pb_ref[...], ub).astype(BF16)
        for k in range(S5_NK):
            cs = slice(k * LANES, (k + 1) * LANES)
            ss = slice(k * S5_KSTATE, (k + 1) * S5_KSTATE)
            lhs = jnp.concatenate([up_f[:, cs], up_b[:, cs]], axis=1)
            v = _dot(lhs, wd_ref[k])
            v_re[par][:, ss] = v[:, :S5_KSTATE]
            v_im[par][:, ss] = v[:, S5_KSTATE:]
            ar = a_ref[0, :, ss]
            ai = a_ref[1, :, ss]
            hr0 = hc_ref[0, :, ss]
            hi0 = hc_ref[1, :, ss]
            hr, hi = hr0, hi0
            for s in range(SCAN_T):
                rs = slice(s * SUBLANES, (s + 1) * SUBLANES)
                nr = ar * hr - ai * hi + v_re[1 - par][rs, ss]
                ni = ar * hi + ai * hr + v_im[1 - par][rs, ss]
                if need_y:
                    h_re[1 - par][rs, ss] = nr
                    h_im[1 - par][rs, ss] = ni
                hr, hi = nr, ni
            hc_ref[0, :, ss] = jnp.where(scan_valid, hr, hr0)
            hc_ref[1, :, ss] = jnp.where(scan_valid, hi, hi0)
            if need_y:
                h = jnp.concatenate([h_re[par][:, ss].astype(BF16), h_im[par][:, ss].astype(BF16)], axis=1)
                y = _dot(h, wr_ref[k])
                ycf_ref[:, cs] = y[:, :LANES]
                ycb_ref[:, cs] = y[:, LANES:]
        if need_y:
            yf = _dot(pft_ref[...], ycf_ref[...].astype(BF16))
            yb = _dot(pbt_ref[...], ycb_ref[...].astype(BF16))
            yf_ref[...] = yf.astype(BF16).reshape(BATCH, SCAN_T, SSM_WIDTH)
            yb_ref[...] = yb.astype(BF16).reshape(BATCH, SCAN_T, SSM_WIDTH)

    @pl.when(g % 2 == 0)
    def _():
        stages(0)

    @pl.when(g % 2 == 1)
    def _():
        stages(1)

    @pl.when(g == pl.num_programs(0) - 1)
    def _():
        ht_ref[...] = hc_ref[...]


def _scan_perms():
    t = SCAN_T
    pf = np.zeros((2 * BATCH * t, BATCH * t), np.float32)
    pb = np.zeros((2 * BATCH * t, BATCH * t), np.float32)
    for s in range(t):
        for b in range(BATCH):
            pf[s * 2 * BATCH + b, b * t + s] = 1.0
            pb[s * 2 * BATCH + BATCH + b, b * t + (t - 1 - s)] = 1.0
    return pf, pb


_PF, _PB = _scan_perms()


def _s5(u, h0, prm, need_y):
    a, wd, wr = prm
    steps = u.shape[1]
    n = steps // SCAN_T
    rows_in = BATCH * SCAN_T
    rows_sc = 2 * rows_in
    pf = jnp.asarray(_PF, BF16)
    pb = jnp.asarray(_PB, BF16)
    const2 = lambda g: (0, 0)
    once = pl.Buffered(1)
    blk = (BATCH, SCAN_T, SSM_WIDTH)
    drive_f = pl.BlockSpec(blk, lambda g: (0, jnp.minimum(g, n - 1), 0))
    drive_b = pl.BlockSpec(blk, lambda g: (0, jnp.maximum(n - 1 - g, 0), 0))
    read_f = pl.BlockSpec(blk, lambda g: (0, jnp.clip(g - 2, 0, n - 1), 0))
    read_b = pl.BlockSpec(blk, lambda g: (0, jnp.clip(n + 1 - g, 0, n - 1), 0))
    state_spec = pl.BlockSpec((2, SUBLANES, STATE_W), lambda g: (0, 0, 0))
    in_specs = [
        drive_f, drive_b, state_spec,
        pl.BlockSpec((rows_sc, rows_in), const2),
        pl.BlockSpec((rows_sc, rows_in), const2),
        pl.BlockSpec((rows_in, rows_sc), const2),
        pl.BlockSpec((rows_in, rows_sc), const2),
        state_spec,
        pl.BlockSpec((S5_NK, 2 * LANES, 2 * S5_KSTATE), lambda g: (0, 0, 0), pipeline_mode=once),
        pl.BlockSpec((S5_NK, 2 * S5_KSTATE, 2 * LANES), lambda g: (0, 0, 0), pipeline_mode=once),
    ]
    state_shape = jax.ShapeDtypeStruct((2, SUBLANES, STATE_W), F32)
    chunk_buf = pltpu.VMEM((rows_sc, STATE_W), F32)
    carry_buf = pltpu.VMEM((2, SUBLANES, STATE_W), F32)
    if need_y:
        out_specs = [read_f, read_b, state_spec]
        y_shape = jax.ShapeDtypeStruct((BATCH, steps, SSM_WIDTH), BF16)
        out_shape = [y_shape, y_shape, state_shape]
        scratch = [chunk_buf] * 8 + [carry_buf, pltpu.VMEM((rows_sc, SSM_WIDTH), F32),
                                     pltpu.VMEM((rows_sc, SSM_WIDTH), F32)]
    else:
        out_specs = [state_spec]
        out_shape = [state_shape]
        scratch = [chunk_buf] * 4 + [carry_buf]
    return pl.pallas_call(
        functools.partial(_s5_kernel, n, need_y),
        grid=(n + 2,),
        in_specs=in_specs,
        out_specs=out_specs,
        out_shape=out_shape,
        scratch_shapes=scratch,
        compiler_params=_params(("arbitrary",), 52),
        name="s5_scan",
    )(u, u, h0, pf, pb, pf.T, pb.T, a, wd, wr)


def _s5_params(lam_re, lam_im, log_step, b_re, b_im, c_re, c_im):
    dt = jnp.exp(log_step)[..., None]
    mag = jnp.exp(lam_re * dt)
    ang = lam_im * dt
    abar_re, abar_im = mag * jnp.cos(ang), mag * jnp.sin(ang)
    den = lam_re * lam_re + lam_im * lam_im
    nr, ni = abar_re - 1.0, abar_im
    coef_re = (nr * lam_re + ni * lam_im) / den
    coef_im = (ni * lam_re - nr * lam_im) / den
    bb_re = coef_re[..., None] * b_re - coef_im[..., None] * b_im
    bb_im = coef_re[..., None] * b_im + coef_im[..., None] * b_re
    bb = jnp.stack([bb_re, bb_im])
    bb = jnp.swapaxes(bb, -1, -2).reshape(-1, SSM_STATE)
    own = ((jnp.arange(bb.shape[0]) // SSM_GROUP) % S5_KGROUPS)[:, None] == (
        jnp.arange(S5_KSTATE) // SSM_STATE)[None, :]
    bb = jnp.where(own, jnp.tile(bb, (1, S5_KGROUPS)), 0.0)
    bb = bb.reshape(2, 2, S5_NK, LANES, S5_KSTATE)
    wd = jnp.concatenate([bb[0], bb[1]], axis=-1)
    wd = jnp.transpose(wd, (1, 0, 2, 3)).reshape(S5_NK, 2 * LANES, 2 * S5_KSTATE).astype(BF16)
    cc = jnp.stack([c_re, -c_im])
    cc = jnp.swapaxes(cc, -1, -2).reshape(-1, SSM_GROUP)
    own = ((jnp.arange(cc.shape[0]) // SSM_STATE) % S5_KGROUPS)[:, None] == (
        jnp.arange(LANES) // SSM_GROUP)[None, :]
    cc = jnp.where(own, jnp.tile(cc, (1, S5_KGROUPS)), 0.0)
    cc = cc.reshape(2, 2, S5_NK, S5_KSTATE, LANES)
    wr = jnp.concatenate([cc[:, 0], cc[:, 1]], axis=-1)
    wr = jnp.transpose(wr, (1, 0, 2, 3)).reshape(S5_NK, 2 * S5_KSTATE, 2 * LANES).astype(BF16)
    a = jnp.stack([abar_re.reshape(2, STATE_W), abar_im.reshape(2, STATE_W)])
    a = jnp.repeat(a, BATCH, axis=1)
    return a, wd, wr


def _dft_tables(n):
    j = np.arange(n, dtype=np.int64)
    ang = 2.0 * np.pi * ((j[:, None] * j[None, :]) % n).astype(np.float64) / n
    s = 1.0 / math.sqrt(n)
    return (np.cos(ang) * s).astype(np.float32), (np.sin(ang) * s).astype(np.float32)


_CH_COS, _CH_SIN = _dft_tables(FFT_GROUP)
_CH_CS = np.concatenate([_CH_COS, _CH_SIN], axis=1)


def _chdft_kernel(u_ref, w_ref, p_ref, q_ref):
    r = _dot(u_ref[...], w_ref[...].astype(BF16))
    p_ref[...] = r[:, :FFT_GROUP].astype(BF16)
    q_ref[...] = r[:, FFT_GROUP:].astype(BF16)


def _channel_dft(rest):
    nb, rows, _ = rest.shape
    tm = min(rows, 2048)
    shape = jax.ShapeDtypeStruct((nb, rows, FFT_WIDTH), BF16)
    blk = pl.BlockSpec((None, tm, FFT_GROUP), lambda b, i, g: (b, i, g))
    return pl.pallas_call(
        _chdft_kernel,
        grid=(nb, rows // tm, FFT_GROUPS),
        in_specs=[blk, pl.BlockSpec((FFT_GROUP, 2 * FFT_GROUP), lambda b, i, g: (0, 0))],
        out_specs=[blk, blk],
        out_shape=[shape, shape],
        compiler_params=_params(("parallel", "parallel", "parallel"), 32),
        name="channel_dft",
    )(rest, jnp.asarray(_CH_CS))


DFT_RADIX = 4
DFT_ILV = 256
_QUARTER_TURN = ((1, 0), (0, 1), (-1, 0), (0, -1))


def _radix_tables(n):
    nq = n // DFT_RADIX
    k = np.arange(nq, dtype=np.int64)
    s = 1.0 / math.sqrt(n)
    cos, sin = [], []
    for r in range(DFT_RADIX):
        j = DFT_RADIX * k + r
        ang = 2.0 * np.pi * ((j[:, None] * k[None, :]) % n).astype(np.float64) / n
        cos.append(np.cos(ang) * s)
        sin.append(np.sin(ang) * s)
    return np.stack(cos).astype(np.float32), np.stack(sin).astype(np.float32)


def _interleave_perm():
    per = DFT_ILV // DFT_RADIX
    perm = np.zeros((DFT_ILV, DFT_ILV), np.float32)
    for r in range(DFT_RADIX):
        for m in range(per):
            perm[DFT_RADIX * m + r, r * per + m] = 1.0
    return perm


_RADIX_TABLES = {n: _radix_tables(n) for n in (CTX_LEN, SEQ)}
_ILV_PERM = _interleave_perm()


def _posdft_kernel(n, p_ref, q_ref, c_ref, s_ref, perm_ref, o_ref, comb_ref, y_ref):
    nq = n // DFT_RADIX
    chunk = min(nq, EPILOGUE_ROWS)

    def combine(i, carry):
        rs = pl.ds(pl.multiple_of(i * chunk, chunk), chunk)
        p = [p_ref[pl.ds(pl.multiple_of(q * nq + i * chunk, chunk), chunk), :].astype(F32)
             for q in range(DFT_RADIX)]
        qq = [q_ref[pl.ds(pl.multiple_of(q * nq + i * chunk, chunk), chunk), :].astype(F32)
              for q in range(DFT_RADIX)]
        def signed_sum(terms):
            acc = None
            for sign, v in terms:
                if acc is None:
                    acc = v if sign > 0 else -v
                else:
                    acc = acc + v if sign > 0 else acc - v
            return acc

        for r in range(DFT_RADIX):
            pr, qr = [], []
            for q in range(DFT_RADIX):
                cs, sn = _QUARTER_TURN[(r * q) % DFT_RADIX]
                if cs:
                    pr.append((cs, p[q]))
                    qr.append((cs, qq[q]))
                if sn:
                    pr.append((-sn, qq[q]))
                    qr.append((sn, p[q]))
            comb_ref[2 * r, rs, :] = signed_sum(pr).astype(BF16)
            comb_ref[2 * r + 1, rs, :] = signed_sum(qr).astype(BF16)
        return carry

    lax.fori_loop(0, nq // chunk, combine, 0)
    for r in range(DFT_RADIX):
        y_ref[r] = (_dot(c_ref[r].astype(BF16), comb_ref[2 * r])
                    - _dot(s_ref[r].astype(BF16), comb_ref[2 * r + 1]))
    per = DFT_ILV // DFT_RADIX
    for blk in range(n // DFT_ILV):
        slab = jnp.concatenate([y_ref[r, blk * per:(blk + 1) * per, :] for r in range(DFT_RADIX)], axis=0)
        o_ref[blk * DFT_ILV:(blk + 1) * DFT_ILV, :] = _dot(perm_ref[...], slab.astype(BF16)).astype(BF16)


def _position_dft(p, q):
    nb, steps, _ = p.shape
    cos, sin = _RADIX_TABLES[steps]
    nq = steps // DFT_RADIX
    x_blk = pl.BlockSpec((None, steps, FFT_WIDTH), lambda b: (b, 0, 0))
    t_blk = pl.BlockSpec((DFT_RADIX, nq, nq), lambda b: (0, 0, 0), pipeline_mode=pl.Buffered(1))
    return pl.pallas_call(
        functools.partial(_posdft_kernel, steps),
        grid=(nb,),
        in_specs=[x_blk, x_blk, t_blk, t_blk, pl.BlockSpec((DFT_ILV, DFT_ILV), lambda b: (0, 0))],
        out_specs=x_blk,
        out_shape=jax.ShapeDtypeStruct((nb, steps, FFT_WIDTH), BF16),
        scratch_shapes=[pltpu.VMEM((2 * DFT_RADIX, nq, FFT_WIDTH), BF16),
                        pltpu.VMEM((DFT_RADIX, nq, FFT_WIDTH), F32)],
        compiler_params=_params(("parallel",), 56),
        name="position_dft",
    )(p, q, jnp.asarray(cos), jnp.asarray(sin), jnp.asarray(_ILV_PERM, BF16))


def _glu_kernel(us_ref, yf_ref, yb_ref, dsk_ref, w_ref, o_ref):
    ys = dsk_ref[...] * us_ref[...] + yf_ref[...].astype(F32) + yb_ref[...].astype(F32)
    g = jax.nn.gelu(ys)
    z = _dot(g.astype(BF16), w_ref[...].astype(BF16))
    o_ref[...] = (g * jax.nn.sigmoid(z)).astype(BF16)


def _glu(us, yf, yb, d_skip, w_glu, layer):
    nb, rows, _ = us.shape
    tm = 1024
    blk = pl.BlockSpec((None, tm, SSM_WIDTH), lambda b, i: (b, i, 0))
    return pl.pallas_call(
        _glu_kernel,
        grid=(nb, rows // tm),
        in_specs=[blk, blk, blk,
                  pl.BlockSpec((None, 1, SSM_WIDTH), lambda b, i: (layer, 0, 0)),
                  pl.BlockSpec((None, SSM_WIDTH, SSM_WIDTH), lambda b, i: (layer, 0, 0))],
        out_specs=blk,
        out_shape=jax.ShapeDtypeStruct((nb, rows, SSM_WIDTH), BF16),
        compiler_params=_params(("parallel", "parallel"), 40),
        name="s5_glu",
    )(us, yf, yb, d_skip.reshape(DEPTH, 1, SSM_WIDTH), w_glu)


def _merge_kernel(s_ref, yq_ref, gs_ref, gf_ref, wps_ref, wpf_ref, o_ref):
    ps = _dot(s_ref[...], wps_ref[...].astype(BF16))
    pf = _dot(yq_ref[...], wpf_ref[...].astype(BF16))
    o_ref[...] = (jax.nn.sigmoid(gs_ref[...].astype(F32)) * ps
                  + jax.nn.sigmoid(gf_ref[...].astype(F32)) * pf).astype(BF16)


def _merge(s, yq, rest, w_ps, w_pf, layer):
    nb, rows, _ = s.shape
    tm = min(rows, 2048)
    tn = 512
    gs_off = FFT_WIDTH // tn
    gf_off = (FFT_WIDTH + D_MODEL) // tn
    row_blk = pl.BlockSpec((None, tm, SSM_WIDTH), lambda b, i, j: (b, i, 0))
    w_blk = pl.BlockSpec((None, SSM_WIDTH, tn), lambda b, i, j: (layer, 0, j))
    return pl.pallas_call(
        _merge_kernel,
        grid=(nb, rows // tm, D_MODEL // tn),
        in_specs=[
            row_blk, row_blk,
            pl.BlockSpec((None, tm, tn), lambda b, i, j: (b, i, gs_off + j)),
            pl.BlockSpec((None, tm, tn), lambda b, i, j: (b, i, gf_off + j)),
            w_blk, w_blk,
        ],
        out_specs=pl.BlockSpec((None, tm, tn), lambda b, i, j: (b, i, j)),
        out_shape=jax.ShapeDtypeStruct((nb, rows, D_MODEL), BF16),
        compiler_params=_params(("parallel", "parallel", "arbitrary"), 48),
        name="gated_merge",
    )(s, yq, rest, rest, w_ps, w_pf)


def _wo_kernel(nk, mg_ref, wo_ref, x_ref, mod_ref, g_ref, b_ref, o_ref, h_ref):
    k = pl.program_id(2)

    @pl.when(k == 0)
    def _():
        o_ref[...] = jnp.zeros_like(o_ref)

    o_ref[...] += _dot(mg_ref[...], wo_ref[...].astype(BF16))

    @pl.when(k == nk - 1)
    def _():
        def finish(rs):
            y = ALPHA * x_ref[rs, :] + _row(mod_ref, GATE1) * o_ref[rs, :]
            x1 = _ln(y) * g_ref[...] + b_ref[...]
            o_ref[rs, :] = x1
            h_ref[rs, :] = _modulated_ln(x1, mod_ref, SHIFT2, SCALE2)

        _for_row_chunks(o_ref.shape[0], finish)


def _out_proj_ln(merged, w_o, x, mod, g, b, layer):
    nb, rows, _ = x.shape
    tm, tk = 1024, 512
    nk = D_MODEL // tk
    row_blk = pl.BlockSpec((None, tm, D_MODEL), lambda b_, i, k: (b_, i, 0))
    vec = pl.BlockSpec((None, 1, D_MODEL), lambda b_, i, k: (layer, 0, 0))
    return pl.pallas_call(
        functools.partial(_wo_kernel, nk),
        grid=(nb, rows // tm, nk),
        in_specs=[
            pl.BlockSpec((None, tm, tk), lambda b_, i, k: (b_, i, k)),
            pl.BlockSpec((None, tk, D_MODEL), lambda b_, i, k: (layer, k, 0)),
            pl.BlockSpec((None, tm, D_MODEL), lambda b_, i, k: (b_, i, 0), pipeline_mode=pl.Buffered(1)),
            _mod_spec(layer, nb), vec, vec,
        ],
        out_specs=[row_blk, row_blk],
        out_shape=[jax.ShapeDtypeStruct(x.shape, F32), jax.ShapeDtypeStruct(x.shape, BF16)],
        compiler_params=_params(("parallel", "parallel", "arbitrary"), 52),
        name="out_proj_ln",
    )(merged, w_o, x, mod, g.reshape(DEPTH, 1, D_MODEL), b.reshape(DEPTH, 1, D_MODEL))


def _mlp_kernel(nf, emit_next, h_ref, x_ref, mod_ref, *refs):
    if emit_next:
        modn_ref, wu_ref, wd_ref, g_ref, b_ref, o_ref, hn_ref = refs
    else:
        wu_ref, wd_ref, g_ref, b_ref, o_ref = refs
    f = pl.program_id(2)

    @pl.when(f == 0)
    def _():
        o_ref[...] = jnp.zeros_like(o_ref)

    a = jnp.maximum(_dot(h_ref[...], wu_ref[...].astype(BF16)), 0.0)
    o_ref[...] += _dot((a * a).astype(BF16), wd_ref[...].astype(BF16))

    @pl.when(f == nf - 1)
    def _():
        def finish(rs):
            y = ALPHA * x_ref[rs, :] + _row(mod_ref, GATE2) * o_ref[rs, :]
            x2 = _ln(y) * g_ref[...] + b_ref[...]
            o_ref[rs, :] = x2
            if emit_next:
                hn_ref[rs, :] = _modulated_ln(x2, modn_ref, SHIFT1, SCALE1)

        _for_row_chunks(o_ref.shape[0], finish)


def _mlp(h2, x1, mod, w_up, w_down, g, b, layer, emit_next):
    nb, rows, _ = x1.shape
    tm, tf = 1024, 512
    nf = D_FF // tf
    row_blk = pl.BlockSpec((None, tm, D_MODEL), lambda b_, i, f: (b_, i, 0))
    vec = pl.BlockSpec((None, 1, D_MODEL), lambda b_, i, f: (layer, 0, 0))
    in_specs = [
        row_blk,
        pl.BlockSpec((None, tm, D_MODEL), lambda b_, i, f: (b_, i, 0), pipeline_mode=pl.Buffered(1)),
        _mod_spec(layer, nb),
    ]
    args = [h2, x1, mod]
    if emit_next:
        in_specs.append(_mod_spec(layer + 1, nb))
        args.append(mod)
    in_specs += [
        pl.BlockSpec((None, D_MODEL, tf), lambda b_, i, f: (layer, 0, f)),
        pl.BlockSpec((None, tf, D_MODEL), lambda b_, i, f: (layer, f, 0)),
        vec, vec,
    ]
    args += [w_up, w_down, g.reshape(DEPTH, 1, D_MODEL), b.reshape(DEPTH, 1, D_MODEL)]
    out_specs = [pl.BlockSpec((None, tm, D_MODEL), lambda b_, i, f: (b_, i, 0), pipeline_mode=pl.Buffered(1))]
    out_shape = [jax.ShapeDtypeStruct(x1.shape, F32)]
    if emit_next:
        out_specs.append(row_blk)
        out_shape.append(jax.ShapeDtypeStruct(x1.shape, BF16))
    out = pl.pallas_call(
        functools.partial(_mlp_kernel, nf, emit_next),
        grid=(nb, rows // tm, nf),
        in_specs=in_specs,
        out_specs=out_specs,
        out_shape=out_shape,
        compiler_params=_params(("parallel", "parallel", "arbitrary"), 56),
        name="mlp",
    )(*args)
    return out if emit_next else (out[0], None)


TAIL_TM = 1024
TAIL_TK = 512
TAIL_TF = 512


def _tail_kernel(layer, emit_next, merged_ref, x_hbm, wo_hbm, wu_hbm, wd_hbm, mod_ref, *refs):
    if emit_next:
        modn_ref, refs = refs[0], refs[1:]
    g1_ref, b1_ref, g2_ref, b2_ref, out_hbm = refs[:5]
    refs = refs[5:]
    if emit_next:
        hn_ref, refs = refs[0], refs[1:]
    xbuf, acc, h2, wa, wb, sem_a, sem_b, sem_x, sem_o = refs
    b = pl.program_id(0)
    i = pl.program_id(1)
    first = jnp.logical_and(b == 0, i == 0)
    last = jnp.logical_and(b == pl.num_programs(0) - 1, i == pl.num_programs(1) - 1)
    nk = D_MODEL // TAIL_TK
    nf = D_FF // TAIL_TF
    assert nk % 2 == 0
    rows = pl.ds(pl.multiple_of(i * TAIL_TM, TAIL_TM), TAIL_TM)

    def wo_copy(t, slot):
        return pltpu.make_async_copy(wo_hbm.at[layer, pl.ds(t * TAIL_TK, TAIL_TK), :], wa.at[slot], sem_a.at[slot])

    def wd_copy(f, slot):
        src = wd_hbm.at[layer, pl.ds(pl.multiple_of(f * TAIL_TK, TAIL_TK), TAIL_TK), :]
        return pltpu.make_async_copy(src, wa.at[slot], sem_a.at[slot])

    def wu_copy(f, slot):
        src = wu_hbm.at[layer, :, pl.ds(pl.multiple_of(f * TAIL_TF, TAIL_TF), TAIL_TF)]
        return pltpu.make_async_copy(src, wb.at[slot], sem_b.at[slot])

    def x_copy():
        return pltpu.make_async_copy(x_hbm.at[b, rows, :], xbuf, sem_x.at[0])

    def out_copy():
        return pltpu.make_async_copy(xbuf, out_hbm.at[b, rows, :], sem_o.at[0])

    wo_copy(0, 0).start()
    wo_copy(1, 1).start()
    wu_copy(0, 0).start()
    wu_copy(1, 1).start()

    for k in range(nk):
        slot = k % 2
        wo_copy(k, slot).wait()
        lhs = merged_ref[:, k * TAIL_TK:(k + 1) * TAIL_TK]
        if k == 0:
            acc[...] = _dot(lhs, wa[slot].astype(BF16))
        else:
            acc[...] += _dot(lhs, wa[slot].astype(BF16))
        if k + 2 < nk:
            wo_copy(k + 2, slot).start()
        else:
            wd_copy(k + 2 - nk, slot).start()
        if k == 0:
            @pl.when(jnp.logical_not(first))
            def _():
                out_copy().wait()

            x_copy().start()
    x_copy().wait()

    def finish1(rs):
        y = ALPHA * xbuf[rs, :] + _row(mod_ref, GATE1) * acc[rs, :]
        x1 = _ln(y) * g1_ref[...] + b1_ref[...]
        xbuf[rs, :] = x1
        h2[rs, :] = _modulated_ln(x1, mod_ref, SHIFT2, SCALE2)
        acc[rs, :] = jnp.zeros((EPILOGUE_ROWS, D_MODEL), F32)

    _for_row_chunks(TAIL_TM, finish1)

    def mlp_tile(f, carry):
        slot = f % 2
        wu_copy(f, slot).wait()
        a = jnp.maximum(_dot(h2[...], wb[slot].astype(BF16)), 0.0)
        a = (a * a).astype(BF16)

        @pl.when(f + 2 < nf)
        def _():
            wu_copy(f + 2, slot).start()

        wd_copy(f, slot).wait()
        acc[...] += _dot(a, wa[slot].astype(BF16))

        @pl.when(f + 2 < nf)
        def _():
            wd_copy(f + 2, slot).start()

        return carry

    lax.fori_loop(0, nf, mlp_tile, 0)

    def finish2(rs):
        y = ALPHA * xbuf[rs, :] + _row(mod_ref, GATE2) * acc[rs, :]
        x2 = _ln(y) * g2_ref[...] + b2_ref[...]
        xbuf[rs, :] = x2
        if emit_next:
            hn_ref[rs, :] = _modulated_ln(x2, modn_ref, SHIFT1, SCALE1)

    _for_row_chunks(TAIL_TM, finish2)
    out_copy().start()

    @pl.when(last)
    def _():
        out_copy().wait()


def _tail(merged, x, mod, w_o, w_up, w_down, g1, b1, g2, b2, layer, emit_next):
    nb, rows, _ = x.shape
    tm = TAIL_TM
    row_blk = pl.BlockSpec((None, tm, D_MODEL), lambda b_, i: (b_, i, 0))
    vec = pl.BlockSpec((None, 1, D_MODEL), lambda b_, i: (layer, 0, 0))
    hbm = pl.BlockSpec(memory_space=pl.ANY)
    in_specs = [row_blk, hbm, hbm, hbm, hbm, _mod_spec(layer, nb)]
    args = [merged, x, w_o, w_up, w_down, mod]
    if emit_next:
        in_specs.append(_mod_spec(layer + 1, nb))
        args.append(mod)
    in_specs += [vec, vec, vec, vec]
    args += [v.reshape(DEPTH, 1, D_MODEL) for v in (g1, b1, g2, b2)]
    out_specs = [hbm]
    out_shape = [jax.ShapeDtypeStruct(x.shape, F32)]
    if emit_next:
        out_specs.append(row_blk)
        out_shape.append(jax.ShapeDtypeStruct(x.shape, BF16))
    out = pl.pallas_call(
        functools.partial(_tail_kernel, layer, emit_next),
        grid=(nb, rows // tm),
        in_specs=in_specs,
        out_specs=out_specs,
        out_shape=out_shape,
        scratch_shapes=[
            pltpu.VMEM((tm, D_MODEL), F32),
            pltpu.VMEM((tm, D_MODEL), F32),
            pltpu.VMEM((tm, D_MODEL), BF16),
            pltpu.VMEM((2, TAIL_TK, D_MODEL), F32),
            pltpu.VMEM((2, D_MODEL, TAIL_TF), F32),
            pltpu.SemaphoreType.DMA((2,)),
            pltpu.SemaphoreType.DMA((2,)),
            pltpu.SemaphoreType.DMA((1,)),
            pltpu.SemaphoreType.DMA((1,)),
        ],
        compiler_params=_params(("arbitrary", "arbitrary"), 58),
        name="layer_tail",
    )(*args)
    return out if emit_next else (out[0], None)


def _pos_table():
    quarter = D_MODEL // 4
    omega = 1.0 / (POS_BASE ** (np.arange(quarter, dtype=np.float64) / quarter))
    t = np.arange(SEQ)
    ar = (t // GRID_W).astype(np.float64)[:, None] * omega
    ac = (t % GRID_W).astype(np.float64)[:, None] * omega
    return np.concatenate([np.sin(ar), np.cos(ar), np.sin(ac), np.cos(ac)], axis=-1).astype(np.float32)


_POS = _pos_table()


def _mod_tables(m):
    m = m.reshape(DEPTH, SUBLANES, 6, D_MODEL)
    m = jnp.pad(m, ((0, 0), (0, 0), (0, SUBLANES - 6), (0, 0)))
    return m[:, :BATCH], m[:, BATCH:BATCH + 1]


def _as_batch(a):
    return a.reshape(BATCH, CTX_LEN, a.shape[-1])


def _as_slab(a):
    return a.reshape(1, BATCH * CTX_LEN, a.shape[-1])


def kernel(x, c, ctx, c_ctx, w_mod, b_mod, w_in, lam_re, lam_im, log_step, ssm_b_re, ssm_b_im, ssm_c_re,
           ssm_c_im, d_skip, w_glu, w_ps, w_pf, w_o, ln1_g, ln1_b, w_up, w_down, ln2_g, ln2_b):
    cond8 = jnp.concatenate([c, c_ctx[None], jnp.zeros((SUBLANES - BATCH - 1, D_MODEL), F32)], axis=0)
    mod_lat, mod_ctx = _mod_tables(_modulation(cond8, w_mod, b_mod))

    x_lat, h_lat = _entry(x, jnp.asarray(_POS), mod_lat)
    x_ctx, h_ctx = _entry(_as_slab(ctx), None, mod_ctx)
    h_zero = jnp.zeros((2, SUBLANES, STATE_W), F32)

    for l in range(DEPTH):
        need_ctx = l < DEPTH - 1
        prm = _s5_params(lam_re[l], lam_im[l], log_step[l], ssm_b_re[l], ssm_b_im[l], ssm_c_re[l],
                         ssm_c_im[l])

        def mixer_tail(us, rest, yf, yb, xx, mod, as_batch, as_rows):
            p, q = _channel_dft(rest)
            yq = as_rows(_position_dft(as_batch(p), as_batch(q)))
            s = _glu(us, as_rows(yf), as_rows(yb), d_skip, w_glu, l)
            merged = _merge(s, yq, rest, w_ps, w_pf, l)
            return _tail(merged, xx, mod, w_o, w_up, w_down, ln1_g, ln1_b, ln2_g, ln2_b, l, need_ctx)

        us_lat, rest_lat = _in_proj(h_lat, w_in, l, True)
        if need_ctx:
            us_ctx, rest_ctx = _in_proj(h_ctx, w_in, l, True)
            yf_c, yb_c, h_t = _s5(_as_batch(us_ctx), h_zero, prm, True)
        else:
            (us_ctx,) = _in_proj(h_ctx, w_in, l, False)
            (h_t,) = _s5(_as_batch(us_ctx), h_zero, prm, False)
        yf, yb, _ = _s5(us_lat, h_t, prm, True)
        ident = lambda a: a
        x_lat, h_lat = mixer_tail(us_lat, rest_lat, yf, yb, x_lat, mod_lat, ident, ident)
        if need_ctx:
            x_ctx, h_ctx = mixer_tail(us_ctx, rest_ctx, yf_c, yb_c, x_ctx, mod_ctx, _as_batch, _as_slab)

    return x_lat
```

```python
import functools
import math

import numpy as np
import jax
import jax.numpy as jnp
from jax import lax
from jax.experimental import pallas as pl
from jax.experimental.pallas import tpu as pltpu

D_MODEL = 2048
BATCH = 4
SEQ = 2048
DEPTH = 2
GRID_W = 64
CTX_LEN = 256
SSM_WIDTH = D_MODEL // 2
SSM_GROUP = 16
SSM_GROUPS = SSM_WIDTH // SSM_GROUP
SSM_STATE = 64
FFT_WIDTH = D_MODEL - SSM_WIDTH
FFT_GROUPS = 4
FFT_GROUP = FFT_WIDTH // FFT_GROUPS
IN_WIDTH = SSM_WIDTH + FFT_WIDTH + 2 * D_MODEL
REST_WIDTH = IN_WIDTH - SSM_WIDTH
D_FF = 4 * D_MODEL
ALPHA = (2 * DEPTH) ** 0.25
LN_EPS = 1e-5
POS_BASE = 10000.0

F32 = jnp.float32
BF16 = jnp.bfloat16

SUBLANES = 8
LANES = 128
STATE_W = SSM_GROUPS * SSM_STATE
SCAN_T = 64
S5_KGROUPS = 8
S5_NK = SSM_GROUPS // S5_KGROUPS
S5_KSTATE = S5_KGROUPS * SSM_STATE
SCAN_COLS = 512
MIB = 1024 * 1024

SHIFT1, SCALE1, GATE1, SHIFT2, SCALE2, GATE2 = range(6)


def _params(sem, vmem_mib):
    return pltpu.CompilerParams(dimension_semantics=sem, vmem_limit_bytes=vmem_mib * MIB)


def _dot(a, b):
    return jnp.dot(a, b, preferred_element_type=F32)


def _ln(x):
    mu = jnp.mean(x, axis=-1, keepdims=True)
    xc = x - mu
    var = jnp.mean(xc * xc, axis=-1, keepdims=True)
    return xc * lax.rsqrt(var + LN_EPS)


def _row(ref, j):
    return ref[j:j + 1, :]


def _modulated_ln(x, mod_ref, shift, scale):
    return (_ln(x) * (1.0 + _row(mod_ref, scale)) + _row(mod_ref, shift)).astype(BF16)


EPILOGUE_ROWS = 128


def _for_row_chunks(rows, fn):
    def body(r, carry):
        fn(pl.ds(pl.multiple_of(r * EPILOGUE_ROWS, EPILOGUE_ROWS), EPILOGUE_ROWS))
        return carry

    lax.fori_loop(0, rows // EPILOGUE_ROWS, body, 0)


def _mod_spec(layer, nb):
    if nb == 1:
        return pl.BlockSpec((None, None, SUBLANES, D_MODEL), lambda b, *_: (layer, 0, 0, 0))
    return pl.BlockSpec((None, None, SUBLANES, D_MODEL), lambda b, *_: (layer, b, 0, 0))


def _mod_kernel(c_ref, w_ref, b_ref, o_ref):
    c = c_ref[...]
    sc = c * jax.nn.sigmoid(c)
    o_ref[...] = _dot(sc.astype(BF16), w_ref[...].astype(BF16)) + b_ref[...]


def _modulation(cond8, w_mod, b_mod):
    tn = 1024
    n = 6 * D_MODEL
    return pl.pallas_call(
        _mod_kernel,
        grid=(DEPTH, n // tn),
        in_specs=[
            pl.BlockSpec((SUBLANES, D_MODEL), lambda l, j: (0, 0)),
            pl.BlockSpec((None, D_MODEL, tn), lambda l, j: (l, 0, j)),
            pl.BlockSpec((None, 1, tn), lambda l, j: (l, 0, j)),
        ],
        out_specs=pl.BlockSpec((None, SUBLANES, tn), lambda l, j: (l, 0, j)),
        out_shape=jax.ShapeDtypeStruct((DEPTH, SUBLANES, n), F32),
        compiler_params=_params(("parallel", "parallel"), 40),
        name="adaln_modulation",
    )(cond8, w_mod, b_mod.reshape(DEPTH, 1, n))


def _entry_kernel(has_pos, *refs):
    if has_pos:
        x_ref, p_ref, mod_ref, xo_ref, h_ref = refs
        x = x_ref[...] + p_ref[...]
        xo_ref[...] = x
    else:
        x_ref, mod_ref, h_ref = refs
        x = x_ref[...]
    h_ref[...] = _modulated_ln(x, mod_ref, SHIFT1, SCALE1)


def _entry(x, pos, mod):
    nb, rows, _ = x.shape
    tm = 512
    has_pos = pos is not None
    blk = pl.BlockSpec((None, tm, D_MODEL), lambda i, b: (b, i, 0))
    in_specs = [blk]
    args = [x]
    if has_pos:
        in_specs.append(pl.BlockSpec((tm, D_MODEL), lambda i, b: (i, 0)))
        args.append(pos)
    in_specs.append(pl.BlockSpec((None, None, SUBLANES, D_MODEL), lambda i, b: (0, b, 0, 0)))
    args.append(mod)
    h_shape = jax.ShapeDtypeStruct(x.shape, BF16)
    out = pl.pallas_call(
        functools.partial(_entry_kernel, has_pos),
        grid=(rows // tm, nb),
        in_specs=in_specs,
        out_specs=[blk, blk] if has_pos else [blk],
        out_shape=[jax.ShapeDtypeStruct(x.shape, F32), h_shape] if has_pos else [h_shape],
        compiler_params=_params(("parallel", "parallel"), 40),
        name="entry_ln",
    )(*args)
    return out if has_pos else (x, out[0])


def _win_kernel(n_us, with_rest, h_ref, w_ref, us_ref, *rest):
    r = _dot(h_ref[...], w_ref[...].astype(BF16))
    if not with_rest:
        us_ref[...] = r
        return
    rest_ref, = rest
    j = pl.program_id(2)

    @pl.when(j < n_us)
    def _():
        us_ref[...] = r

    @pl.when(j >= n_us)
    def _():
        rest_ref[...] = r.astype(BF16)


def _in_proj(h, w_in, layer, with_rest):
    nb, rows, _ = h.shape
    tm = min(rows, 2048)
    tn = 512
    n_us = SSM_WIDTH // tn
    n_tiles = (IN_WIDTH if with_rest else SSM_WIDTH) // tn
    out_specs = [pl.BlockSpec((None, tm, tn), lambda b, i, j: (b, i, jnp.minimum(j, n_us - 1)))]
    out_shape = [jax.ShapeDtypeStruct((nb, rows, SSM_WIDTH), F32)]
    if with_rest:
        out_specs.append(pl.BlockSpec((None, tm, tn), lambda b, i, j: (b, i, jnp.maximum(j - n_us, 0))))
        out_shape.append(jax.ShapeDtypeStruct((nb, rows, REST_WIDTH), BF16))
    return pl.pallas_call(
        functools.partial(_win_kernel, n_us, with_rest),
        grid=(nb, rows // tm, n_tiles),
        in_specs=[
            pl.BlockSpec((None, tm, D_MODEL), lambda b, i, j: (b, i, 0)),
            pl.BlockSpec((None, D_MODEL, tn), lambda b, i, j: (layer, 0, j)),
        ],
        out_specs=out_specs,
        out_shape=out_shape,
        compiler_params=_params(("parallel", "parallel", "arbitrary"), 48),
        name="in_proj",
    )(h, w_in)


def _s5_kernel(need_y, uf_ref, ub_ref, h0_ref, pf_ref, pb_ref, pft_ref, pbt_ref, a_ref, wd_ref, wr_ref,
               *rest):
    if need_y:
        yf_ref, yb_ref, ht_ref, v_ref, hc_ref, h_ref, ycf_ref, ycb_ref = rest
    else:
        ht_ref, v_ref, hc_ref = rest
    g = pl.program_id(0)
    rows_in = BATCH * SCAN_T
    pair = 2 * SUBLANES

    @pl.when(g == 0)
    def _():
        hc_ref[...] = h0_ref[...]

    uf = uf_ref[...].reshape(rows_in, SSM_WIDTH).astype(BF16)
    ub = ub_ref[...].reshape(rows_in, SSM_WIDTH).astype(BF16)
    up_f = _dot(pf_ref[...], uf).astype(BF16)
    up_b = _dot(pb_ref[...], ub).astype(BF16)
    for k in range(S5_NK):
        cs = slice(k * LANES, (k + 1) * LANES)
        ss = slice(k * S5_KSTATE, (k + 1) * S5_KSTATE)
        re_cols = slice(2 * k * S5_KSTATE, (2 * k + 1) * S5_KSTATE)
        im_cols = slice((2 * k + 1) * S5_KSTATE, (2 * k + 2) * S5_KSTATE)
        lhs = jnp.concatenate([up_f[:, cs], up_b[:, cs]], axis=1)
        v_ref[:, 2 * k * S5_KSTATE:(2 * k + 2) * S5_KSTATE] = _dot(lhs, wd_ref[k])
        ar = a_ref[0, :, ss]
        ai = a_ref[1, :, ss]
        hr = hc_ref[0, :, ss]
        hi = hc_ref[1, :, ss]
        for s2 in range(SCAN_T // 2):
            rows = []
            for s in (2 * s2, 2 * s2 + 1):
                rs = slice(s * SUBLANES, (s + 1) * SUBLANES)
                nr = ar * hr - ai * hi + v_ref[rs, re_cols]
                ni = ar * hi + ai * hr + v_ref[rs, im_cols]
                hr, hi = nr, ni
                rows.append((nr, ni))
            if need_y:
                ps = slice(s2 * pair, (s2 + 1) * pair)
                h_ref[ps, re_cols] = jnp.concatenate([rows[0][0], rows[1][0]], axis=0).astype(BF16)
                h_ref[ps, im_cols] = jnp.concatenate([rows[0][1], rows[1][1]], axis=0).astype(BF16)
        hc_ref[0, :, ss] = hr
        hc_ref[1, :, ss] = hi
        if need_y:
            y = _dot(h_ref[:, 2 * k * S5_KSTATE:(2 * k + 2) * S5_KSTATE], wr_ref[k])
            ycf_ref[:, cs] = y[:, :LANES]
            ycb_ref[:, cs] = y[:, LANES:]
    if need_y:
        yf = _dot(pft_ref[...], ycf_ref[...].astype(BF16))
        yb = _dot(pbt_ref[...], ycb_ref[...].astype(BF16))
        yf_ref[...] = yf.astype(BF16).reshape(BATCH, SCAN_T, SSM_WIDTH)
        yb_ref[...] = yb.astype(BF16).reshape(BATCH, SCAN_T, SSM_WIDTH)

    @pl.when(g == pl.num_programs(0) - 1)
    def _():
        ht_ref[...] = hc_ref[...]


def _scan_perms():
    t = SCAN_T
    pf = np.zeros((2 * BATCH * t, BATCH * t), np.float32)
    pb = np.zeros((2 * BATCH * t, BATCH * t), np.float32)
    for s in range(t):
        for b in range(BATCH):
            pf[s * 2 * BATCH + b, b * t + s] = 1.0
            pb[s * 2 * BATCH + BATCH + b, b * t + (t - 1 - s)] = 1.0
    return pf, pb


_PF, _PB = _scan_perms()


def _s5(u, h0, prm, need_y):
    a, wd, wr = prm
    steps = u.shape[1]
    n = steps // SCAN_T
    rows_in = BATCH * SCAN_T
    rows_sc = 2 * rows_in
    pf = jnp.asarray(_PF, BF16)
    pb = jnp.asarray(_PB, BF16)
    const2 = lambda g: (0, 0)
    once = pl.Buffered(1)
    blk = (BATCH, SCAN_T, SSM_WIDTH)
    fwd_blk = pl.BlockSpec(blk, lambda g: (0, g, 0))
    bwd_blk = pl.BlockSpec(blk, lambda g: (0, n - 1 - g, 0))
    state_spec = pl.BlockSpec((2, SUBLANES, STATE_W), lambda g: (0, 0, 0))
    in_specs = [
        fwd_blk, bwd_blk, state_spec,
        pl.BlockSpec((rows_sc, rows_in), const2),
        pl.BlockSpec((rows_sc, rows_in), const2),
        pl.BlockSpec((rows_in, rows_sc), const2),
        pl.BlockSpec((rows_in, rows_sc), const2),
        state_spec,
        pl.BlockSpec((S5_NK, 2 * LANES, 2 * S5_KSTATE), lambda g: (0, 0, 0), pipeline_mode=once),
        pl.BlockSpec((S5_NK, 2 * S5_KSTATE, 2 * LANES), lambda g: (0, 0, 0), pipeline_mode=once),
    ]
    state_shape = jax.ShapeDtypeStruct((2, SUBLANES, STATE_W), F32)
    scratch = [pltpu.VMEM((rows_sc, 2 * STATE_W), F32), pltpu.VMEM((2, SUBLANES, STATE_W), F32)]
    if need_y:
        out_specs = [fwd_blk, bwd_blk, state_spec]
        y_shape = jax.ShapeDtypeStruct((BATCH, steps, SSM_WIDTH), BF16)
        out_shape = [y_shape, y_shape, state_shape]
        scratch += [pltpu.VMEM((rows_sc, 2 * STATE_W), BF16), pltpu.VMEM((rows_sc, SSM_WIDTH), F32),
                    pltpu.VMEM((rows_sc, SSM_WIDTH), F32)]
    else:
        out_specs = [state_spec]
        out_shape = [state_shape]
    return pl.pallas_call(
        functools.partial(_s5_kernel, need_y),
        grid=(n,),
        in_specs=in_specs,
        out_specs=out_specs,
        out_shape=out_shape,
        scratch_shapes=scratch,
        compiler_params=_params(("arbitrary",), 52),
        name="s5_scan",
    )(u, u, h0, pf, pb, pf.T, pb.T, a, wd, wr)


def _s5_params(lam_re, lam_im, log_step, b_re, b_im, c_re, c_im):
    dt = jnp.exp(log_step)[..., None]
    mag = jnp.exp(lam_re * dt)
    ang = lam_im * dt
    abar_re, abar_im = mag * jnp.cos(ang), mag * jnp.sin(ang)
    den = lam_re * lam_re + lam_im * lam_im
    nr, ni = abar_re - 1.0, abar_im
    coef_re = (nr * lam_re + ni * lam_im) / den
    coef_im = (ni * lam_re - nr * lam_im) / den
    bb_re = coef_re[..., None] * b_re - coef_im[..., None] * b_im
    bb_im = coef_re[..., None] * b_im + coef_im[..., None] * b_re
    bb = jnp.stack([bb_re, bb_im])
    bb = jnp.swapaxes(bb, -1, -2).reshape(-1, SSM_STATE)
    own = ((jnp.arange(bb.shape[0]) // SSM_GROUP) % S5_KGROUPS)[:, None] == (
        jnp.arange(S5_KSTATE) // SSM_STATE)[None, :]
    bb = jnp.where(own, jnp.tile(bb, (1, S5_KGROUPS)), 0.0)
    bb = bb.reshape(2, 2, S5_NK, LANES, S5_KSTATE)
    wd = jnp.concatenate([bb[0], bb[1]], axis=-1)
    wd = jnp.transpose(wd, (1, 0, 2, 3)).reshape(S5_NK, 2 * LANES, 2 * S5_KSTATE).astype(BF16)
    cc = jnp.stack([c_re, -c_im])
    cc = jnp.swapaxes(cc, -1, -2).reshape(-1, SSM_GROUP)
    own = ((jnp.arange(cc.shape[0]) // SSM_STATE) % S5_KGROUPS)[:, None] == (
        jnp.arange(LANES) // SSM_GROUP)[None, :]
    cc = jnp.where(own, jnp.tile(cc, (1, S5_KGROUPS)), 0.0)
    cc = cc.reshape(2, 2, S5_NK, S5_KSTATE, LANES)
    wr = jnp.concatenate([cc[:, 0], cc[:, 1]], axis=-1)
    wr = jnp.transpose(wr, (1, 0, 2, 3)).reshape(S5_NK, 2 * S5_KSTATE, 2 * LANES).astype(BF16)
    a = jnp.stack([abar_re.reshape(2, STATE_W), abar_im.reshape(2, STATE_W)])
    a = jnp.repeat(a, BATCH, axis=1)
    return a, wd, wr


def _dft_tables(n):
    j = np.arange(n, dtype=np.int64)
    ang = 2.0 * np.pi * ((j[:, None] * j[None, :]) % n).astype(np.float64) / n
    s = 1.0 / math.sqrt(n)
    return (np.cos(ang) * s).astype(np.float32), (np.sin(ang) * s).astype(np.float32)


_CH_COS, _CH_SIN = _dft_tables(FFT_GROUP)
_CH_CS = np.concatenate([_CH_COS, _CH_SIN], axis=1)


def _chdft_kernel(u_ref, w_ref, p_ref, q_ref):
    r = _dot(u_ref[...], w_ref[...].astype(BF16))
    p_ref[...] = r[:, :FFT_GROUP].astype(BF16)
    q_ref[...] = r[:, FFT_GROUP:].astype(BF16)


def _channel_dft(rest):
    nb, rows, _ = rest.shape
    tm = min(rows, 2048)
    shape = jax.ShapeDtypeStruct((nb, rows, FFT_WIDTH), BF16)
    blk = pl.BlockSpec((None, tm, FFT_GROUP), lambda b, i, g: (b, i, g))
    return pl.pallas_call(
        _chdft_kernel,
        grid=(nb, rows // tm, FFT_GROUPS),
        in_specs=[blk, pl.BlockSpec((FFT_GROUP, 2 * FFT_GROUP), lambda b, i, g: (0, 0))],
        out_specs=[blk, blk],
        out_shape=[shape, shape],
        compiler_params=_params(("parallel", "parallel", "parallel"), 32),
        name="channel_dft",
    )(rest, jnp.asarray(_CH_CS))


DFT_RADIX = 4
DFT_ILV = 256
_QUARTER_TURN = ((1, 0), (0, 1), (-1, 0), (0, -1))


def _radix_tables(n):
    nq = n // DFT_RADIX
    k = np.arange(nq, dtype=np.int64)
    s = 1.0 / math.sqrt(n)
    cos, sin = [], []
    for r in range(DFT_RADIX):
        j = DFT_RADIX * k + r
        ang = 2.0 * np.pi * ((j[:, None] * k[None, :]) % n).astype(np.float64) / n
        cos.append(np.cos(ang) * s)
        sin.append(np.sin(ang) * s)
    return np.stack(cos).astype(np.float32), np.stack(sin).astype(np.float32)


def _interleave_perm():
    per = DFT_ILV // DFT_RADIX
    perm = np.zeros((DFT_ILV, DFT_ILV), np.float32)
    for r in range(DFT_RADIX):
        for m in range(per):
            perm[DFT_RADIX * m + r, r * per + m] = 1.0
    return perm


_RADIX_TABLES = {n: _radix_tables(n) for n in (CTX_LEN, SEQ)}
_ILV_PERM = _interleave_perm()


def _posdft_kernel(n, p_ref, q_ref, c_ref, s_ref, perm_ref, o_ref, comb_ref, y_ref):
    nq = n // DFT_RADIX
    chunk = min(nq, EPILOGUE_ROWS)

    def combine(i, carry):
        rs = pl.ds(pl.multiple_of(i * chunk, chunk), chunk)
        p = [p_ref[pl.ds(pl.multiple_of(q * nq + i * chunk, chunk), chunk), :].astype(F32)
             for q in range(DFT_RADIX)]
        qq = [q_ref[pl.ds(pl.multiple_of(q * nq + i * chunk, chunk), chunk), :].astype(F32)
              for q in range(DFT_RADIX)]
        def signed_sum(terms):
            acc = None
            for sign, v in terms:
                if acc is None:
                    acc = v if sign > 0 else -v
                else:
                    acc = acc + v if sign > 0 else acc - v
            return acc

        for r in range(DFT_RADIX):
            pr, qr = [], []
            for q in range(DFT_RADIX):
                cs, sn = _QUARTER_TURN[(r * q) % DFT_RADIX]
                if cs:
                    pr.append((cs, p[q]))
                    qr.append((cs, qq[q]))
                if sn:
                    pr.append((-sn, qq[q]))
                    qr.append((sn, p[q]))
            comb_ref[2 * r, rs, :] = signed_sum(pr).astype(BF16)
            comb_ref[2 * r + 1, rs, :] = signed_sum(qr).astype(BF16)
        return carry

    lax.fori_loop(0, nq // chunk, combine, 0)
    for r in range(DFT_RADIX):
        y_ref[r] = (_dot(c_ref[r].astype(BF16), comb_ref[2 * r])
                    - _dot(s_ref[r].astype(BF16), comb_ref[2 * r + 1]))
    per = DFT_ILV // DFT_RADIX
    for blk in range(n // DFT_ILV):
        slab = jnp.concatenate([y_ref[r, blk * per:(blk + 1) * per, :] for r in range(DFT_RADIX)], axis=0)
        o_ref[blk * DFT_ILV:(blk + 1) * DFT_ILV, :] = _dot(perm_ref[...], slab.astype(BF16)).astype(BF16)


def _position_dft(p, q):
    nb, steps, _ = p.shape
    cos, sin = _RADIX_TABLES[steps]
    nq = steps // DFT_RADIX
    x_blk = pl.BlockSpec((None, steps, FFT_WIDTH), lambda b: (b, 0, 0))
    t_blk = pl.BlockSpec((DFT_RADIX, nq, nq), lambda b: (0, 0, 0), pipeline_mode=pl.Buffered(1))
    return pl.pallas_call(
        functools.partial(_posdft_kernel, steps),
        grid=(nb,),
        in_specs=[x_blk, x_blk, t_blk, t_blk, pl.BlockSpec((DFT_ILV, DFT_ILV), lambda b: (0, 0))],
        out_specs=x_blk,
        out_shape=jax.ShapeDtypeStruct((nb, steps, FFT_WIDTH), BF16),
        scratch_shapes=[pltpu.VMEM((2 * DFT_RADIX, nq, FFT_WIDTH), BF16),
                        pltpu.VMEM((DFT_RADIX, nq, FFT_WIDTH), F32)],
        compiler_params=_params(("parallel",), 56),
        name="position_dft",
    )(p, q, jnp.asarray(cos), jnp.asarray(sin), jnp.asarray(_ILV_PERM, BF16))


def _glu_kernel(us_ref, yf_ref, yb_ref, dsk_ref, w_ref, o_ref):
    ys = dsk_ref[...] * us_ref[...] + yf_ref[...].astype(F32) + yb_ref[...].astype(F32)
    g = jax.nn.gelu(ys)
    z = _dot(g.astype(BF16), w_ref[...].astype(BF16))
    o_ref[...] = (g * jax.nn.sigmoid(z)).astype(BF16)


def _glu(us, yf, yb, d_skip, w_glu, layer):
    nb, rows, _ = us.shape
    tm = 1024
    blk = pl.BlockSpec((None, tm, SSM_WIDTH), lambda b, i: (b, i, 0))
    return pl.pallas_call(
        _glu_kernel,
        grid=(nb, rows // tm),
        in_specs=[blk, blk, blk,
                  pl.BlockSpec((None, 1, SSM_WIDTH), lambda b, i: (layer, 0, 0)),
                  pl.BlockSpec((None, SSM_WIDTH, SSM_WIDTH), lambda b, i: (layer, 0, 0))],
        out_specs=blk,
        out_shape=jax.ShapeDtypeStruct((nb, rows, SSM_WIDTH), BF16),
        compiler_params=_params(("parallel", "parallel"), 40),
        name="s5_glu",
    )(us, yf, yb, d_skip.reshape(DEPTH, 1, SSM_WIDTH), w_glu)


def _merge_kernel(s_ref, yq_ref, gs_ref, gf_ref, wps_ref, wpf_ref, o_ref):
    ps = _dot(s_ref[...], wps_ref[...].astype(BF16))
    pf = _dot(yq_ref[...], wpf_ref[...].astype(BF16))
    o_ref[...] = (jax.nn.sigmoid(gs_ref[...].astype(F32)) * ps
                  + jax.nn.sigmoid(gf_ref[...].astype(F32)) * pf).astype(BF16)


def _merge(s, yq, rest, w_ps, w_pf, layer):
    nb, rows, _ = s.shape
    tm = min(rows, 2048)
    tn = 512
    gs_off = FFT_WIDTH // tn
    gf_off = (FFT_WIDTH + D_MODEL) // tn
    row_blk = pl.BlockSpec((None, tm, SSM_WIDTH), lambda b, i, j: (b, i, 0))
    w_blk = pl.BlockSpec((None, SSM_WIDTH, tn), lambda b, i, j: (layer, 0, j))
    return pl.pallas_call(
        _merge_kernel,
        grid=(nb, rows // tm, D_MODEL // tn),
        in_specs=[
            row_blk, row_blk,
            pl.BlockSpec((None, tm, tn), lambda b, i, j: (b, i, gs_off + j)),
            pl.BlockSpec((None, tm, tn), lambda b, i, j: (b, i, gf_off + j)),
            w_blk, w_blk,
        ],
        out_specs=pl.BlockSpec((None, tm, tn), lambda b, i, j: (b, i, j)),
        out_shape=jax.ShapeDtypeStruct((nb, rows, D_MODEL), BF16),
        compiler_params=_params(("parallel", "parallel", "arbitrary"), 48),
        name="gated_merge",
    )(s, yq, rest, rest, w_ps, w_pf)


def _wo_kernel(nk, mg_ref, wo_ref, x_ref, mod_ref, g_ref, b_ref, o_ref, h_ref):
    k = pl.program_id(2)

    @pl.when(k == 0)
    def _():
        o_ref[...] = jnp.zeros_like(o_ref)

    o_ref[...] += _dot(mg_ref[...], wo_ref[...].astype(BF16))

    @pl.when(k == nk - 1)
    def _():
        def finish(rs):
            y = ALPHA * x_ref[rs, :] + _row(mod_ref, GATE1) * o_ref[rs, :]
            x1 = _ln(y) * g_ref[...] + b_ref[...]
            o_ref[rs, :] = x1
            h_ref[rs, :] = _modulated_ln(x1, mod_ref, SHIFT2, SCALE2)

        _for_row_chunks(o_ref.shape[0], finish)


def _out_proj_ln(merged, w_o, x, mod, g, b, layer):
    nb, rows, _ = x.shape
    tm, tk = 1024, 512
    nk = D_MODEL // tk
    row_blk = pl.BlockSpec((None, tm, D_MODEL), lambda b_, i, k: (b_, i, 0))
    vec = pl.BlockSpec((None, 1, D_MODEL), lambda b_, i, k: (layer, 0, 0))
    return pl.pallas_call(
        functools.partial(_wo_kernel, nk),
        grid=(nb, rows // tm, nk),
        in_specs=[
            pl.BlockSpec((None, tm, tk), lambda b_, i, k: (b_, i, k)),
            pl.BlockSpec((None, tk, D_MODEL), lambda b_, i, k: (layer, k, 0)),
            pl.BlockSpec((None, tm, D_MODEL), lambda b_, i, k: (b_, i, 0), pipeline_mode=pl.Buffered(1)),
            _mod_spec(layer, nb), vec, vec,
        ],
        out_specs=[row_blk, row_blk],
        out_shape=[jax.ShapeDtypeStruct(x.shape, F32), jax.ShapeDtypeStruct(x.shape, BF16)],
        compiler_params=_params(("parallel", "parallel", "arbitrary"), 52),
        name="out_proj_ln",
    )(merged, w_o, x, mod, g.reshape(DEPTH, 1, D_MODEL), b.reshape(DEPTH, 1, D_MODEL))


def _mlp_kernel(nf, emit_next, h_ref, x_ref, mod_ref, *refs):
    if emit_next:
        modn_ref, wu_ref, wd_ref, g_ref, b_ref, o_ref, hn_ref = refs
    else:
        wu_ref, wd_ref, g_ref, b_ref, o_ref = refs
    f = pl.program_id(2)

    @pl.when(f == 0)
    def _():
        o_ref[...] = jnp.zeros_like(o_ref)

    a = jnp.maximum(_dot(h_ref[...], wu_ref[...].astype(BF16)), 0.0)
    o_ref[...] += _dot((a * a).astype(BF16), wd_ref[...].astype(BF16))

    @pl.when(f == nf - 1)
    def _():
        def finish(rs):
            y = ALPHA * x_ref[rs, :] + _row(mod_ref, GATE2) * o_ref[rs, :]
            x2 = _ln(y) * g_ref[...] + b_ref[...]
            o_ref[rs, :] = x2
            if emit_next:
                hn_ref[rs, :] = _modulated_ln(x2, modn_ref, SHIFT1, SCALE1)

        _for_row_chunks(o_ref.shape[0], finish)


def _mlp(h2, x1, mod, w_up, w_down, g, b, layer, emit_next):
    nb, rows, _ = x1.shape
    tm, tf = 1024, 512
    nf = D_FF // tf
    row_blk = pl.BlockSpec((None, tm, D_MODEL), lambda b_, i, f: (b_, i, 0))
    vec = pl.BlockSpec((None, 1, D_MODEL), lambda b_, i, f: (layer, 0, 0))
    in_specs = [
        row_blk,
        pl.BlockSpec((None, tm, D_MODEL), lambda b_, i, f: (b_, i, 0), pipeline_mode=pl.Buffered(1)),
        _mod_spec(layer, nb),
    ]
    args = [h2, x1, mod]
    if emit_next:
        in_specs.append(_mod_spec(layer + 1, nb))
        args.append(mod)
    in_specs += [
        pl.BlockSpec((None, D_MODEL, tf), lambda b_, i, f: (layer, 0, f)),
        pl.BlockSpec((None, tf, D_MODEL), lambda b_, i, f: (layer, f, 0)),
        vec, vec,
    ]
    args += [w_up, w_down, g.reshape(DEPTH, 1, D_MODEL), b.reshape(DEPTH, 1, D_MODEL)]
    out_specs = [pl.BlockSpec((None, tm, D_MODEL), lambda b_, i, f: (b_, i, 0), pipeline_mode=pl.Buffered(1))]
    out_shape = [jax.ShapeDtypeStruct(x1.shape, F32)]
    if emit_next:
        out_specs.append(row_blk)
        out_shape.append(jax.ShapeDtypeStruct(x1.shape, BF16))
    out = pl.pallas_call(
        functools.partial(_mlp_kernel, nf, emit_next),
        grid=(nb, rows // tm, nf),
        in_specs=in_specs,
        out_specs=out_specs,
        out_shape=out_shape,
        compiler_params=_params(("parallel", "parallel", "arbitrary"), 56),
        name="mlp",
    )(*args)
    return out if emit_next else (out[0], None)


TAIL_TM = 1024
TAIL_TK = 512
TAIL_TF = 512


def _tail_kernel(layer, emit_next, merged_ref, x_hbm, wo_hbm, wu_hbm, wd_hbm, mod_ref, *refs):
    if emit_next:
        modn_ref, refs = refs[0], refs[1:]
    g1_ref, b1_ref, g2_ref, b2_ref, out_hbm = refs[:5]
    refs = refs[5:]
    if emit_next:
        hn_ref, refs = refs[0], refs[1:]
    xbuf, acc, h2, wa, wb, sem_a, sem_b, sem_x, sem_o = refs
    b = pl.program_id(0)
    i = pl.program_id(1)
    first = jnp.logical_and(b == 0, i == 0)
    last = jnp.logical_and(b == pl.num_programs(0) - 1, i == pl.num_programs(1) - 1)
    nk = D_MODEL // TAIL_TK
    nf = D_FF // TAIL_TF
    assert nk % 2 == 0
    rows = pl.ds(pl.multiple_of(i * TAIL_TM, TAIL_TM), TAIL_TM)

    def wo_copy(t, slot):
        return pltpu.make_async_copy(wo_hbm.at[layer, pl.ds(t * TAIL_TK, TAIL_TK), :], wa.at[slot], sem_a.at[slot])

    def wd_copy(f, slot):
        src = wd_hbm.at[layer, pl.ds(pl.multiple_of(f * TAIL_TK, TAIL_TK), TAIL_TK), :]
        return pltpu.make_async_copy(src, wa.at[slot], sem_a.at[slot])

    def wu_copy(f, slot):
        src = wu_hbm.at[layer, :, pl.ds(pl.multiple_of(f * TAIL_TF, TAIL_TF), TAIL_TF)]
        return pltpu.make_async_copy(src, wb.at[slot], sem_b.at[slot])

    def x_copy():
        return pltpu.make_async_copy(x_hbm.at[b, rows, :], xbuf, sem_x.at[0])

    def out_copy():
        return pltpu.make_async_copy(xbuf, out_hbm.at[b, rows, :], sem_o.at[0])

    wo_copy(0, 0).start()
    wo_copy(1, 1).start()
    wu_copy(0, 0).start()
    wu_copy(1, 1).start()

    for k in range(nk):
        slot = k % 2
        wo_copy(k, slot).wait()
        lhs = merged_ref[:, k * TAIL_TK:(k + 1) * TAIL_TK]
        if k == 0:
            acc[...] = _dot(lhs, wa[slot].astype(BF16))
        else:
            acc[...] += _dot(lhs, wa[slot].astype(BF16))
        if k + 2 < nk:
            wo_copy(k + 2, slot).start()
        else:
            wd_copy(k + 2 - nk, slot).start()
        if k == 0:
            @pl.when(jnp.logical_not(first))
            def _():
                out_copy().wait()

            x_copy().start()
    x_copy().wait()

    def finish1(rs):
        y = ALPHA * xbuf[rs, :] + _row(mod_ref, GATE1) * acc[rs, :]
        x1 = _ln(y) * g1_ref[...] + b1_ref[...]
        xbuf[rs, :] = x1
        h2[rs, :] = _modulated_ln(x1, mod_ref, SHIFT2, SCALE2)
        acc[rs, :] = jnp.zeros((EPILOGUE_ROWS, D_MODEL), F32)

    _for_row_chunks(TAIL_TM, finish1)

    def mlp_tile(f, carry):
        slot = f % 2
        wu_copy(f, slot).wait()
        a = jnp.maximum(_dot(h2[...], wb[slot].astype(BF16)), 0.0)
        a = (a * a).astype(BF16)

        @pl.when(f + 2 < nf)
        def _():
            wu_copy(f + 2, slot).start()

        wd_copy(f, slot).wait()
        acc[...] += _dot(a, wa[slot].astype(BF16))

        @pl.when(f + 2 < nf)
        def _():
            wd_copy(f + 2, slot).start()

        return carry

    lax.fori_loop(0, nf, mlp_tile, 0)

    def finish2(rs):
        y = ALPHA * xbuf[rs, :] + _row(mod_ref, GATE2) * acc[rs, :]
        x2 = _ln(y) * g2_ref[...] + b2_ref[...]
        xbuf[rs, :] = x2
        if emit_next:
            hn_ref[rs, :] = _modulated_ln(x2, modn_ref, SHIFT1, SCALE1)

    _for_row_chunks(TAIL_TM, finish2)
    out_copy().start()

    @pl.when(last)
    def _():
        out_copy().wait()


def _tail(merged, x, mod, w_o, w_up, w_down, g1, b1, g2, b2, layer, emit_next):
    nb, rows, _ = x.shape
    tm = TAIL_TM
    row_blk = pl.BlockSpec((None, tm, D_MODEL), lambda b_, i: (b_, i, 0))
    vec = pl.BlockSpec((None, 1, D_MODEL), lambda b_, i: (layer, 0, 0))
    hbm = pl.BlockSpec(memory_space=pl.ANY)
    in_specs = [row_blk, hbm, hbm, hbm, hbm, _mod_spec(layer, nb)]
    args = [merged, x, w_o, w_up, w_down, mod]
    if emit_next:
        in_specs.append(_mod_spec(layer + 1, nb))
        args.append(mod)
    in_specs += [vec, vec, vec, vec]
    args += [v.reshape(DEPTH, 1, D_MODEL) for v in (g1, b1, g2, b2)]
    out_specs = [hbm]
    out_shape = [jax.ShapeDtypeStruct(x.shape, F32)]
    if emit_next:
        out_specs.append(row_blk)
        out_shape.append(jax.ShapeDtypeStruct(x.shape, BF16))
    out = pl.pallas_call(
        functools.partial(_tail_kernel, layer, emit_next),
        grid=(nb, rows // tm),
        in_specs=in_specs,
        out_specs=out_specs,
        out_shape=out_shape,
        scratch_shapes=[
            pltpu.VMEM((tm, D_MODEL), F32),
            pltpu.VMEM((tm, D_MODEL), F32),
            pltpu.VMEM((tm, D_MODEL), BF16),
            pltpu.VMEM((2, TAIL_TK, D_MODEL), F32),
            pltpu.VMEM((2, D_MODEL, TAIL_TF), F32),
            pltpu.SemaphoreType.DMA((2,)),
            pltpu.SemaphoreType.DMA((2,)),
            pltpu.SemaphoreType.DMA((1,)),
            pltpu.SemaphoreType.DMA((1,)),
        ],
        compiler_params=_params(("arbitrary", "arbitrary"), 58),
        name="layer_tail",
    )(*args)
    return out if emit_next else (out[0], None)


def _pos_table():
    quarter = D_MODEL // 4
    omega = 1.0 / (POS_BASE ** (np.arange(quarter, dtype=np.float64) / quarter))
    t = np.arange(SEQ)
    ar = (t // GRID_W).astype(np.float64)[:, None] * omega
    ac = (t % GRID_W).astype(np.float64)[:, None] * omega
    return np.concatenate([np.sin(ar), np.cos(ar), np.sin(ac), np.cos(ac)], axis=-1).astype(np.float32)


_POS = _pos_table()


def _mod_tables(m):
    m = m.reshape(DEPTH, SUBLANES, 6, D_MODEL)
    m = jnp.pad(m, ((0, 0), (0, 0), (0, SUBLANES - 6), (0, 0)))
    return m[:, :BATCH], m[:, BATCH:BATCH + 1]


def _as_batch(a):
    return a.reshape(BATCH, CTX_LEN, a.shape[-1])


def _as_slab(a):
    return a.reshape(1, BATCH * CTX_LEN, a.shape[-1])


def kernel(x, c, ctx, c_ctx, w_mod, b_mod, w_in, lam_re, lam_im, log_step, ssm_b_re, ssm_b_im, ssm_c_re,
           ssm_c_im, d_skip, w_glu, w_ps, w_pf, w_o, ln1_g, ln1_b, w_up, w_down, ln2_g, ln2_b):
    cond8 = jnp.concatenate([c, c_ctx[None], jnp.zeros((SUBLANES - BATCH - 1, D_MODEL), F32)], axis=0)
    mod_lat, mod_ctx = _mod_tables(_modulation(cond8, w_mod, b_mod))

    x_lat, h_lat = _entry(x, jnp.asarray(_POS), mod_lat)
    x_ctx, h_ctx = _entry(_as_slab(ctx), None, mod_ctx)
    h_zero = jnp.zeros((2, SUBLANES, STATE_W), F32)

    for l in range(DEPTH):
        need_ctx = l < DEPTH - 1
        prm = _s5_params(lam_re[l], lam_im[l], log_step[l], ssm_b_re[l], ssm_b_im[l], ssm_c_re[l],
                         ssm_c_im[l])

        def mixer_tail(us, rest, yf, yb, xx, mod, as_batch, as_rows):
            p, q = _channel_dft(rest)
            yq = as_rows(_position_dft(as_batch(p), as_batch(q)))
            s = _glu(us, as_rows(yf), as_rows(yb), d_skip, w_glu, l)
            merged = _merge(s, yq, rest, w_ps, w_pf, l)
            return _tail(merged, xx, mod, w_o, w_up, w_down, ln1_g, ln1_b, ln2_g, ln2_b, l, need_ctx)

        us_lat, rest_lat = _in_proj(h_lat, w_in, l, True)
        if need_ctx:
            us_ctx, rest_ctx = _in_proj(h_ctx, w_in, l, True)
            yf_c, yb_c, h_t = _s5(_as_batch(us_ctx), h_zero, prm, True)
        else:
            (us_ctx,) = _in_proj(h_ctx, w_in, l, False)
            (h_t,) = _s5(_as_batch(us_ctx), h_zero, prm, False)
        yf, yb, _ = _s5(us_lat, h_t, prm, True)
        ident = lambda a: a
        x_lat, h_lat = mixer_tail(us_lat, rest_lat, yf, yb, x_lat, mod_lat, ident, ident)
        if need_ctx:
            x_ctx, h_ctx = mixer_tail(us_ctx, rest_ctx, yf_c, yb_c, x_ctx, mod_ctx, _as_batch, _as_slab)

    return x_lat
```

```python
import functools
import math

import numpy as np
import jax
import jax.numpy as jnp
from jax import lax
from jax.experimental import pallas as pl
from jax.experimental.pallas import tpu as pltpu

D_MODEL = 2048
BATCH = 4
SEQ = 2048
DEPTH = 2
GRID_W = 64
CTX_LEN = 256
SSM_WIDTH = D_MODEL // 2
SSM_GROUP = 16
SSM_GROUPS = SSM_WIDTH // SSM_GROUP
SSM_STATE = 64
FFT_WIDTH = D_MODEL - SSM_WIDTH
FFT_GROUPS = 4
FFT_GROUP = FFT_WIDTH // FFT_GROUPS
IN_WIDTH = SSM_WIDTH + FFT_WIDTH + 2 * D_MODEL
REST_WIDTH = IN_WIDTH - SSM_WIDTH
D_FF = 4 * D_MODEL
ALPHA = (2 * DEPTH) ** 0.25
LN_EPS = 1e-5
POS_BASE = 10000.0

F32 = jnp.float32
BF16 = jnp.bfloat16

SUBLANES = 8
LANES = 128
STATE_W = SSM_GROUPS * SSM_STATE
SCAN_T = 64
S5_KGROUPS = 8
S5_NK = SSM_GROUPS // S5_KGROUPS
S5_KSTATE = S5_KGROUPS * SSM_STATE
MIB = 1024 * 1024

SHIFT1, SCALE1, GATE1, SHIFT2, SCALE2, GATE2 = range(6)


def _params(sem, vmem_mib):
    return pltpu.CompilerParams(dimension_semantics=sem, vmem_limit_bytes=vmem_mib * MIB)


def _dot(a, b):
    return jnp.dot(a, b, preferred_element_type=F32)


def _ln(x):
    mu = jnp.mean(x, axis=-1, keepdims=True)
    xc = x - mu
    var = jnp.mean(xc * xc, axis=-1, keepdims=True)
    return xc * lax.rsqrt(var + LN_EPS)


def _row(ref, j):
    return ref[j:j + 1, :]


def _modulated_ln(x, mod_ref, shift, scale):
    return (_ln(x) * (1.0 + _row(mod_ref, scale)) + _row(mod_ref, shift)).astype(BF16)


EPILOGUE_ROWS = 128


def _for_row_chunks(rows, fn):
    def body(r, carry):
        fn(pl.ds(pl.multiple_of(r * EPILOGUE_ROWS, EPILOGUE_ROWS), EPILOGUE_ROWS))
        return carry

    lax.fori_loop(0, rows // EPILOGUE_ROWS, body, 0)


def _mod_spec(layer, nb):
    if nb == 1:
        return pl.BlockSpec((None, None, SUBLANES, D_MODEL), lambda b, *_: (layer, 0, 0, 0))
    return pl.BlockSpec((None, None, SUBLANES, D_MODEL), lambda b, *_: (layer, b, 0, 0))


def _mod_kernel(c_ref, w_ref, b_ref, o_ref):
    c = c_ref[...]
    sc = c * jax.nn.sigmoid(c)
    o_ref[...] = _dot(sc.astype(BF16), w_ref[...].astype(BF16)) + b_ref[...]


def _modulation(cond8, w_mod, b_mod):
    tn = 1024
    n = 6 * D_MODEL
    return pl.pallas_call(
        _mod_kernel,
        grid=(DEPTH, n // tn),
        in_specs=[
            pl.BlockSpec((SUBLANES, D_MODEL), lambda l, j: (0, 0)),
            pl.BlockSpec((None, D_MODEL, tn), lambda l, j: (l, 0, j)),
            pl.BlockSpec((None, 1, tn), lambda l, j: (l, 0, j)),
        ],
        out_specs=pl.BlockSpec((None, SUBLANES, tn), lambda l, j: (l, 0, j)),
        out_shape=jax.ShapeDtypeStruct((DEPTH, SUBLANES, n), F32),
        compiler_params=_params(("parallel", "parallel"), 40),
        name="adaln_modulation",
    )(cond8, w_mod, b_mod.reshape(DEPTH, 1, n))


def _entry_kernel(has_pos, *refs):
    if has_pos:
        x_ref, p_ref, mod_ref, xo_ref, h_ref = refs
        x = x_ref[...] + p_ref[...]
        xo_ref[...] = x
    else:
        x_ref, mod_ref, h_ref = refs
        x = x_ref[...]
    h_ref[...] = _modulated_ln(x, mod_ref, SHIFT1, SCALE1)


def _entry(x, pos, mod):
    nb, rows, _ = x.shape
    tm = 512
    has_pos = pos is not None
    blk = pl.BlockSpec((None, tm, D_MODEL), lambda i, b: (b, i, 0))
    in_specs = [blk]
    args = [x]
    if has_pos:
        in_specs.append(pl.BlockSpec((tm, D_MODEL), lambda i, b: (i, 0)))
        args.append(pos)
    in_specs.append(pl.BlockSpec((None, None, SUBLANES, D_MODEL), lambda i, b: (0, b, 0, 0)))
    args.append(mod)
    h_shape = jax.ShapeDtypeStruct(x.shape, BF16)
    out = pl.pallas_call(
        functools.partial(_entry_kernel, has_pos),
        grid=(rows // tm, nb),
        in_specs=in_specs,
        out_specs=[blk, blk] if has_pos else [blk],
        out_shape=[jax.ShapeDtypeStruct(x.shape, F32), h_shape] if has_pos else [h_shape],
        compiler_params=_params(("parallel", "parallel"), 40),
        name="entry_ln",
    )(*args)
    return out if has_pos else (x, out[0])


ROW_SPLIT = 2


def _win_kernel(n_us, with_rest, h_ref, w_ref, us_ref, *rest):
    piece = h_ref.shape[0] // ROW_SPLIT

    def project(o_ref):
        w = w_ref[...].astype(BF16)
        for r in range(ROW_SPLIT):
            rs = slice(r * piece, (r + 1) * piece)
            o_ref[rs, :] = _dot(h_ref[rs, :], w).astype(o_ref.dtype)

    if not with_rest:
        project(us_ref)
        return
    rest_ref, = rest
    j = pl.program_id(2)

    @pl.when(j < n_us)
    def _():
        project(us_ref)

    @pl.when(j >= n_us)
    def _():
        project(rest_ref)


def _in_proj(h, w_in, layer, with_rest):
    nb, rows, _ = h.shape
    tm = min(rows, 2048)
    tn = 512
    n_us = SSM_WIDTH // tn
    n_tiles = (IN_WIDTH if with_rest else SSM_WIDTH) // tn
    out_specs = [pl.BlockSpec((None, tm, tn), lambda b, i, j: (b, i, jnp.minimum(j, n_us - 1)))]
    out_shape = [jax.ShapeDtypeStruct((nb, rows, SSM_WIDTH), F32)]
    if with_rest:
        out_specs.append(pl.BlockSpec((None, tm, tn), lambda b, i, j: (b, i, jnp.maximum(j - n_us, 0))))
        out_shape.append(jax.ShapeDtypeStruct((nb, rows, REST_WIDTH), BF16))
    return pl.pallas_call(
        functools.partial(_win_kernel, n_us, with_rest),
        grid=(nb, rows // tm, n_tiles),
        in_specs=[
            pl.BlockSpec((None, tm, D_MODEL), lambda b, i, j: (b, i, 0)),
            pl.BlockSpec((None, D_MODEL, tn), lambda b, i, j: (layer, 0, j)),
        ],
        out_specs=out_specs,
        out_shape=out_shape,
        compiler_params=_params(("parallel", "parallel", "arbitrary"), 48),
        name="in_proj",
    )(h, w_in)


def _s5_kernel(need_y, uf_ref, ub_ref, h0_ref, pf_ref, pb_ref, pft_ref, pbt_ref, a_ref, wd_ref, wr_ref,
               *rest):
    if need_y:
        yf_ref, yb_ref, ht_ref, v_ref, hc_ref, h_ref, ycf_ref, ycb_ref = rest
    else:
        ht_ref, v_ref, hc_ref = rest
    g = pl.program_id(0)
    rows_in = BATCH * SCAN_T
    pair = 2 * SUBLANES

    @pl.when(g == 0)
    def _():
        hc_ref[...] = h0_ref[...]

    uf = uf_ref[...].reshape(rows_in, SSM_WIDTH).astype(BF16)
    ub = ub_ref[...].reshape(rows_in, SSM_WIDTH).astype(BF16)
    up_f = _dot(pf_ref[...], uf).astype(BF16)
    up_b = _dot(pb_ref[...], ub).astype(BF16)
    for k in range(S5_NK):
        cs = slice(k * LANES, (k + 1) * LANES)
        ss = slice(k * S5_KSTATE, (k + 1) * S5_KSTATE)
        re_cols = slice(2 * k * S5_KSTATE, (2 * k + 1) * S5_KSTATE)
        im_cols = slice((2 * k + 1) * S5_KSTATE, (2 * k + 2) * S5_KSTATE)
        lhs = jnp.concatenate([up_f[:, cs], up_b[:, cs]], axis=1)
        v_ref[:, 2 * k * S5_KSTATE:(2 * k + 2) * S5_KSTATE] = _dot(lhs, wd_ref[k])
        ar = a_ref[0, :, ss]
        ai = a_ref[1, :, ss]
        hr = hc_ref[0, :, ss]
        hi = hc_ref[1, :, ss]
        for s2 in range(SCAN_T // 2):
            rows = []
            for s in (2 * s2, 2 * s2 + 1):
                rs = slice(s * SUBLANES, (s + 1) * SUBLANES)
                nr = ar * hr - ai * hi + v_ref[rs, re_cols]
                ni = ar * hi + ai * hr + v_ref[rs, im_cols]
                hr, hi = nr, ni
                rows.append((nr, ni))
            if need_y:
                ps = slice(s2 * pair, (s2 + 1) * pair)
                h_ref[ps, re_cols] = jnp.concatenate([rows[0][0], rows[1][0]], axis=0).astype(BF16)
                h_ref[ps, im_cols] = jnp.concatenate([rows[0][1], rows[1][1]], axis=0).astype(BF16)
        hc_ref[0, :, ss] = hr
        hc_ref[1, :, ss] = hi
        if need_y:
            y = _dot(h_ref[:, 2 * k * S5_KSTATE:(2 * k + 2) * S5_KSTATE], wr_ref[k])
            ycf_ref[:, cs] = y[:, :LANES]
            ycb_ref[:, cs] = y[:, LANES:]
    if need_y:
        yf = _dot(pft_ref[...], ycf_ref[...].astype(BF16))
        yb = _dot(pbt_ref[...], ycb_ref[...].astype(BF16))
        yf_ref[...] = yf.astype(BF16).reshape(BATCH, SCAN_T, SSM_WIDTH)
        yb_ref[...] = yb.astype(BF16).reshape(BATCH, SCAN_T, SSM_WIDTH)

    @pl.when(g == pl.num_programs(0) - 1)
    def _():
        ht_ref[...] = hc_ref[...]


def _scan_perms():
    t = SCAN_T
    pf = np.zeros((2 * BATCH * t, BATCH * t), np.float32)
    pb = np.zeros((2 * BATCH * t, BATCH * t), np.float32)
    for s in range(t):
        for b in range(BATCH):
            pf[s * 2 * BATCH + b, b * t + s] = 1.0
            pb[s * 2 * BATCH + BATCH + b, b * t + (t - 1 - s)] = 1.0
    return pf, pb


_PF, _PB = _scan_perms()


def _s5(u, h0, prm, need_y):
    a, wd, wr = prm
    steps = u.shape[1]
    n = steps // SCAN_T
    rows_in = BATCH * SCAN_T
    rows_sc = 2 * rows_in
    pf = jnp.asarray(_PF, BF16)
    pb = jnp.asarray(_PB, BF16)
    const2 = lambda g: (0, 0)
    once = pl.Buffered(1)
    blk = (BATCH, SCAN_T, SSM_WIDTH)
    fwd_blk = pl.BlockSpec(blk, lambda g: (0, g, 0))
    bwd_blk = pl.BlockSpec(blk, lambda g: (0, n - 1 - g, 0))
    state_spec = pl.BlockSpec((2, SUBLANES, STATE_W), lambda g: (0, 0, 0))
    in_specs = [
        fwd_blk, bwd_blk, state_spec,
        pl.BlockSpec((rows_sc, rows_in), const2),
        pl.BlockSpec((rows_sc, rows_in), const2),
        pl.BlockSpec((rows_in, rows_sc), const2),
        pl.BlockSpec((rows_in, rows_sc), const2),
        state_spec,
        pl.BlockSpec((S5_NK, 2 * LANES, 2 * S5_KSTATE), lambda g: (0, 0, 0), pipeline_mode=once),
        pl.BlockSpec((S5_NK, 2 * S5_KSTATE, 2 * LANES), lambda g: (0, 0, 0), pipeline_mode=once),
    ]
    state_shape = jax.ShapeDtypeStruct((2, SUBLANES, STATE_W), F32)
    scratch = [pltpu.VMEM((rows_sc, 2 * STATE_W), F32), pltpu.VMEM((2, SUBLANES, STATE_W), F32)]
    if need_y:
        out_specs = [fwd_blk, bwd_blk, state_spec]
        y_shape = jax.ShapeDtypeStruct((BATCH, steps, SSM_WIDTH), BF16)
        out_shape = [y_shape, y_shape, state_shape]
        scratch += [pltpu.VMEM((rows_sc, 2 * STATE_W), BF16), pltpu.VMEM((rows_sc, SSM_WIDTH), F32),
                    pltpu.VMEM((rows_sc, SSM_WIDTH), F32)]
    else:
        out_specs = [state_spec]
        out_shape = [state_shape]
    return pl.pallas_call(
        functools.partial(_s5_kernel, need_y),
        grid=(n,),
        in_specs=in_specs,
        out_specs=out_specs,
        out_shape=out_shape,
        scratch_shapes=scratch,
        compiler_params=_params(("arbitrary",), 52),
        name="s5_scan",
    )(u, u, h0, pf, pb, pf.T, pb.T, a, wd, wr)


def _s5_params(lam_re, lam_im, log_step, b_re, b_im, c_re, c_im):
    dt = jnp.exp(log_step)[..., None]
    mag = jnp.exp(lam_re * dt)
    ang = lam_im * dt
    abar_re, abar_im = mag * jnp.cos(ang), mag * jnp.sin(ang)
    den = lam_re * lam_re + lam_im * lam_im
    nr, ni = abar_re - 1.0, abar_im
    coef_re = (nr * lam_re + ni * lam_im) / den
    coef_im = (ni * lam_re - nr * lam_im) / den
    bb_re = coef_re[..., None] * b_re - coef_im[..., None] * b_im
    bb_im = coef_re[..., None] * b_im + coef_im[..., None] * b_re
    bb = jnp.stack([bb_re, bb_im])
    bb = jnp.swapaxes(bb, -1, -2).reshape(-1, SSM_STATE)
    own = ((jnp.arange(bb.shape[0]) // SSM_GROUP) % S5_KGROUPS)[:, None] == (
        jnp.arange(S5_KSTATE) // SSM_STATE)[None, :]
    bb = jnp.where(own, jnp.tile(bb, (1, S5_KGROUPS)), 0.0)
    bb = bb.reshape(2, 2, S5_NK, LANES, S5_KSTATE)
    wd = jnp.concatenate([bb[0], bb[1]], axis=-1)
    wd = jnp.transpose(wd, (1, 0, 2, 3)).reshape(S5_NK, 2 * LANES, 2 * S5_KSTATE).astype(BF16)
    cc = jnp.stack([c_re, -c_im])
    cc = jnp.swapaxes(cc, -1, -2).reshape(-1, SSM_GROUP)
    own = ((jnp.arange(cc.shape[0]) // SSM_STATE) % S5_KGROUPS)[:, None] == (
        jnp.arange(LANES) // SSM_GROUP)[None, :]
    cc = jnp.where(own, jnp.tile(cc, (1, S5_KGROUPS)), 0.0)
    cc = cc.reshape(2, 2, S5_NK, S5_KSTATE, LANES)
    wr = jnp.concatenate([cc[:, 0], cc[:, 1]], axis=-1)
    wr = jnp.transpose(wr, (1, 0, 2, 3)).reshape(S5_NK, 2 * S5_KSTATE, 2 * LANES).astype(BF16)
    a = jnp.stack([abar_re.reshape(2, STATE_W), abar_im.reshape(2, STATE_W)])
    a = jnp.repeat(a, BATCH, axis=1)
    return a, wd, wr


def _dft_tables(n):
    j = np.arange(n, dtype=np.int64)
    ang = 2.0 * np.pi * ((j[:, None] * j[None, :]) % n).astype(np.float64) / n
    s = 1.0 / math.sqrt(n)
    return (np.cos(ang) * s).astype(np.float32), (np.sin(ang) * s).astype(np.float32)


_CH_COS, _CH_SIN = _dft_tables(FFT_GROUP)
_CH_CS = np.concatenate([_CH_COS, _CH_SIN], axis=1)


def _chdft_kernel(u_ref, w_ref, p_ref, q_ref):
    w = w_ref[...].astype(BF16)
    for g in range(FFT_GROUPS):
        cs = slice(g * FFT_GROUP, (g + 1) * FFT_GROUP)
        r = _dot(u_ref[:, cs], w)
        p_ref[:, cs] = r[:, :FFT_GROUP].astype(BF16)
        q_ref[:, cs] = r[:, FFT_GROUP:].astype(BF16)


def _channel_dft(rest):
    nb, rows, _ = rest.shape
    tm = min(rows, 2048)
    shape = jax.ShapeDtypeStruct((nb, rows, FFT_WIDTH), BF16)
    blk = pl.BlockSpec((None, tm, FFT_WIDTH), lambda b, i: (b, i, 0))
    return pl.pallas_call(
        _chdft_kernel,
        grid=(nb, rows // tm),
        in_specs=[blk, pl.BlockSpec((FFT_GROUP, 2 * FFT_GROUP), lambda b, i: (0, 0))],
        out_specs=[blk, blk],
        out_shape=[shape, shape],
        compiler_params=_params(("parallel", "parallel"), 40),
        name="channel_dft",
    )(rest, jnp.asarray(_CH_CS))


DFT_RADIX = 4
DFT_ILV = 256
_QUARTER_TURN = ((1, 0), (0, 1), (-1, 0), (0, -1))


def _radix_tables(n):
    nq = n // DFT_RADIX
    k = np.arange(nq, dtype=np.int64)
    s = 1.0 / math.sqrt(n)
    cos, sin = [], []
    for r in range(DFT_RADIX):
        j = DFT_RADIX * k + r
        ang = 2.0 * np.pi * ((j[:, None] * k[None, :]) % n).astype(np.float64) / n
        cos.append(np.cos(ang) * s)
        sin.append(np.sin(ang) * s)
    return np.stack(cos).astype(np.float32), np.stack(sin).astype(np.float32)


def _interleave_perm():
    per = DFT_ILV // DFT_RADIX
    perm = np.zeros((DFT_ILV, DFT_ILV), np.float32)
    for r in range(DFT_RADIX):
        for m in range(per):
            perm[DFT_RADIX * m + r, r * per + m] = 1.0
    return perm


_RADIX_TABLES = {n: _radix_tables(n) for n in (CTX_LEN, SEQ)}
_ILV_PERM = _interleave_perm()


def _posdft_kernel(n, p_ref, q_ref, c_ref, s_ref, perm_ref, o_ref, comb_ref, y_ref):
    nq = n // DFT_RADIX
    chunk = min(nq, EPILOGUE_ROWS)

    def combine(i, carry):
        rs = pl.ds(pl.multiple_of(i * chunk, chunk), chunk)
        p = [p_ref[pl.ds(pl.multiple_of(q * nq + i * chunk, chunk), chunk), :].astype(F32)
             for q in range(DFT_RADIX)]
        qq = [q_ref[pl.ds(pl.multiple_of(q * nq + i * chunk, chunk), chunk), :].astype(F32)
              for q in range(DFT_RADIX)]
        def signed_sum(terms):
            acc = None
            for sign, v in terms:
                if acc is None:
                    acc = v if sign > 0 else -v
                else:
                    acc = acc + v if sign > 0 else acc - v
            return acc

        for r in range(DFT_RADIX):
            pr, qr = [], []
            for q in range(DFT_RADIX):
                cs, sn = _QUARTER_TURN[(r * q) % DFT_RADIX]
                if cs:
                    pr.append((cs, p[q]))
                    qr.append((cs, qq[q]))
                if sn:
                    pr.append((-sn, qq[q]))
                    qr.append((sn, p[q]))
            comb_ref[2 * r, rs, :] = signed_sum(pr).astype(BF16)
            comb_ref[2 * r + 1, rs, :] = signed_sum(qr).astype(BF16)
        return carry

    lax.fori_loop(0, nq // chunk, combine, 0)
    for r in range(DFT_RADIX):
        y_ref[r] = (_dot(c_ref[r].astype(BF16), comb_ref[2 * r])
                    - _dot(s_ref[r].astype(BF16), comb_ref[2 * r + 1]))
    per = DFT_ILV // DFT_RADIX
    for blk in range(n // DFT_ILV):
        slab = jnp.concatenate([y_ref[r, blk * per:(blk + 1) * per, :] for r in range(DFT_RADIX)], axis=0)
        o_ref[blk * DFT_ILV:(blk + 1) * DFT_ILV, :] = _dot(perm_ref[...], slab.astype(BF16)).astype(BF16)


def _position_dft(p, q):
    nb, steps, _ = p.shape
    cos, sin = _RADIX_TABLES[steps]
    nq = steps // DFT_RADIX
    x_blk = pl.BlockSpec((None, steps, FFT_WIDTH), lambda b: (b, 0, 0))
    t_blk = pl.BlockSpec((DFT_RADIX, nq, nq), lambda b: (0, 0, 0), pipeline_mode=pl.Buffered(1))
    return pl.pallas_call(
        functools.partial(_posdft_kernel, steps),
        grid=(nb,),
        in_specs=[x_blk, x_blk, t_blk, t_blk, pl.BlockSpec((DFT_ILV, DFT_ILV), lambda b: (0, 0))],
        out_specs=x_blk,
        out_shape=jax.ShapeDtypeStruct((nb, steps, FFT_WIDTH), BF16),
        scratch_shapes=[pltpu.VMEM((2 * DFT_RADIX, nq, FFT_WIDTH), BF16),
                        pltpu.VMEM((DFT_RADIX, nq, FFT_WIDTH), F32)],
        compiler_params=_params(("parallel",), 56),
        name="position_dft",
    )(p, q, jnp.asarray(cos), jnp.asarray(sin), jnp.asarray(_ILV_PERM, BF16))


def _glu_kernel(us_ref, yf_ref, yb_ref, dsk_ref, w_ref, o_ref):
    ys = dsk_ref[...] * us_ref[...] + yf_ref[...].astype(F32) + yb_ref[...].astype(F32)
    g = jax.nn.gelu(ys)
    z = _dot(g.astype(BF16), w_ref[...].astype(BF16))
    o_ref[...] = (g * jax.nn.sigmoid(z)).astype(BF16)


def _glu(us, yf, yb, d_skip, w_glu, layer):
    nb, rows, _ = us.shape
    tm = 1024
    blk = pl.BlockSpec((None, tm, SSM_WIDTH), lambda b, i: (b, i, 0))
    return pl.pallas_call(
        _glu_kernel,
        grid=(nb, rows // tm),
        in_specs=[blk, blk, blk,
                  pl.BlockSpec((None, 1, SSM_WIDTH), lambda b, i: (layer, 0, 0)),
                  pl.BlockSpec((None, SSM_WIDTH, SSM_WIDTH), lambda b, i: (layer, 0, 0))],
        out_specs=blk,
        out_shape=jax.ShapeDtypeStruct((nb, rows, SSM_WIDTH), BF16),
        compiler_params=_params(("parallel", "parallel"), 40),
        name="s5_glu",
    )(us, yf, yb, d_skip.reshape(DEPTH, 1, SSM_WIDTH), w_glu)


MERGE_SPLIT = 4


def _merge_kernel(s_ref, yq_ref, gs_ref, gf_ref, wps_ref, wpf_ref, o_ref):
    wps = wps_ref[...].astype(BF16)
    wpf = wpf_ref[...].astype(BF16)
    piece = o_ref.shape[0] // MERGE_SPLIT
    for r in range(MERGE_SPLIT):
        rs = slice(r * piece, (r + 1) * piece)
        ps = _dot(s_ref[rs, :], wps)
        pf = _dot(yq_ref[rs, :], wpf)
        o_ref[rs, :] = (jax.nn.sigmoid(gs_ref[rs, :].astype(F32)) * ps
                        + jax.nn.sigmoid(gf_ref[rs, :].astype(F32)) * pf).astype(BF16)


def _merge(s, yq, rest, w_ps, w_pf, layer):
    nb, rows, _ = s.shape
    tm = min(rows, 2048)
    tn = 512
    gs_off = FFT_WIDTH // tn
    gf_off = (FFT_WIDTH + D_MODEL) // tn
    row_blk = pl.BlockSpec((None, tm, SSM_WIDTH), lambda b, i, j: (b, i, 0))
    w_blk = pl.BlockSpec((None, SSM_WIDTH, tn), lambda b, i, j: (layer, 0, j))
    return pl.pallas_call(
        _merge_kernel,
        grid=(nb, rows // tm, D_MODEL // tn),
        in_specs=[
            row_blk, row_blk,
            pl.BlockSpec((None, tm, tn), lambda b, i, j: (b, i, gs_off + j)),
            pl.BlockSpec((None, tm, tn), lambda b, i, j: (b, i, gf_off + j)),
            w_blk, w_blk,
        ],
        out_specs=pl.BlockSpec((None, tm, tn), lambda b, i, j: (b, i, j)),
        out_shape=jax.ShapeDtypeStruct((nb, rows, D_MODEL), BF16),
        compiler_params=_params(("parallel", "parallel", "arbitrary"), 48),
        name="gated_merge",
    )(s, yq, rest, rest, w_ps, w_pf)


TAIL_TM = 1024
TAIL_TK = 512
TAIL_TF = 512


def _tail_kernel(layer, emit_next, merged_ref, x_hbm, wo_hbm, wu_hbm, wd_hbm, mod_ref, *refs):
    if emit_next:
        modn_ref, refs = refs[0], refs[1:]
    g1_ref, b1_ref, g2_ref, b2_ref, out_hbm = refs[:5]
    refs = refs[5:]
    if emit_next:
        hn_ref, refs = refs[0], refs[1:]
    xbuf, acc, h2, wa, wb, sem_a, sem_b, sem_x, sem_o = refs
    b = pl.program_id(0)
    i = pl.program_id(1)
    first = jnp.logical_and(b == 0, i == 0)
    last = jnp.logical_and(b == pl.num_programs(0) - 1, i == pl.num_programs(1) - 1)
    nk = D_MODEL // TAIL_TK
    nf = D_FF // TAIL_TF
    assert nk % 2 == 0
    rows = pl.ds(pl.multiple_of(i * TAIL_TM, TAIL_TM), TAIL_TM)

    def wo_copy(t, slot):
        return pltpu.make_async_copy(wo_hbm.at[layer, pl.ds(t * TAIL_TK, TAIL_TK), :], wa.at[slot], sem_a.at[slot])

    def wd_copy(f, slot):
        src = wd_hbm.at[layer, pl.ds(pl.multiple_of(f * TAIL_TK, TAIL_TK), TAIL_TK), :]
        return pltpu.make_async_copy(src, wa.at[slot], sem_a.at[slot])

    def wu_copy(f, slot):
        src = wu_hbm.at[layer, :, pl.ds(pl.multiple_of(f * TAIL_TF, TAIL_TF), TAIL_TF)]
        return pltpu.make_async_copy(src, wb.at[slot], sem_b.at[slot])

    def x_copy():
        return pltpu.make_async_copy(x_hbm.at[b, rows, :], xbuf, sem_x.at[0])

    def out_copy():
        return pltpu.make_async_copy(xbuf, out_hbm.at[b, rows, :], sem_o.at[0])

    wo_copy(0, 0).start()
    wo_copy(1, 1).start()
    wu_copy(0, 0).start()
    wu_copy(1, 1).start()

    for k in range(nk):
        slot = k % 2
        wo_copy(k, slot).wait()
        lhs = merged_ref[:, k * TAIL_TK:(k + 1) * TAIL_TK]
        if k == 0:
            acc[...] = _dot(lhs, wa[slot].astype(BF16))
        else:
            acc[...] += _dot(lhs, wa[slot].astype(BF16))
        if k + 2 < nk:
            wo_copy(k + 2, slot).start()
        else:
            wd_copy(k + 2 - nk, slot).start()
        if k == 0:
            @pl.when(jnp.logical_not(first))
            def _():
                out_copy().wait()

            x_copy().start()
    x_copy().wait()

    def finish1(rs):
        y = ALPHA * xbuf[rs, :] + _row(mod_ref, GATE1) * acc[rs, :]
        x1 = _ln(y) * g1_ref[...] + b1_ref[...]
        xbuf[rs, :] = x1
        h2[rs, :] = _modulated_ln(x1, mod_ref, SHIFT2, SCALE2)
        acc[rs, :] = jnp.zeros((EPILOGUE_ROWS, D_MODEL), F32)

    _for_row_chunks(TAIL_TM, finish1)

    def mlp_tile(f, carry):
        slot = f % 2
        wu_copy(f, slot).wait()
        a = jnp.maximum(_dot(h2[...], wb[slot].astype(BF16)), 0.0)
        a = (a * a).astype(BF16)

        @pl.when(f + 2 < nf)
        def _():
            wu_copy(f + 2, slot).start()

        wd_copy(f, slot).wait()
        acc[...] += _dot(a, wa[slot].astype(BF16))

        @pl.when(f + 2 < nf)
        def _():
            wd_copy(f + 2, slot).start()

        return carry

    lax.fori_loop(0, nf, mlp_tile, 0)

    def finish2(rs):
        y = ALPHA * xbuf[rs, :] + _row(mod_ref, GATE2) * acc[rs, :]
        x2 = _ln(y) * g2_ref[...] + b2_ref[...]
        xbuf[rs, :] = x2
        if emit_next:
            hn_ref[rs, :] = _modulated_ln(x2, modn_ref, SHIFT1, SCALE1)

    _for_row_chunks(TAIL_TM, finish2)
    out_copy().start()

    @pl.when(last)
    def _():
        out_copy().wait()


def _tail(merged, x, mod, w_o, w_up, w_down, g1, b1, g2, b2, layer, emit_next):
    nb, rows, _ = x.shape
    tm = TAIL_TM
    row_blk = pl.BlockSpec((None, tm, D_MODEL), lambda b_, i: (b_, i, 0))
    vec = pl.BlockSpec((None, 1, D_MODEL), lambda b_, i: (layer, 0, 0))
    hbm = pl.BlockSpec(memory_space=pl.ANY)
    in_specs = [row_blk, hbm, hbm, hbm, hbm, _mod_spec(layer, nb)]
    args = [merged, x, w_o, w_up, w_down, mod]
    if emit_next:
        in_specs.append(_mod_spec(layer + 1, nb))
        args.append(mod)
    in_specs += [vec, vec, vec, vec]
    args += [v.reshape(DEPTH, 1, D_MODEL) for v in (g1, b1, g2, b2)]
    out_specs = [hbm]
    out_shape = [jax.ShapeDtypeStruct(x.shape, F32)]
    if emit_next:
        out_specs.append(row_blk)
        out_shape.append(jax.ShapeDtypeStruct(x.shape, BF16))
    out = pl.pallas_call(
        functools.partial(_tail_kernel, layer, emit_next),
        grid=(nb, rows // tm),
        in_specs=in_specs,
        out_specs=out_specs,
        out_shape=out_shape,
        scratch_shapes=[
            pltpu.VMEM((tm, D_MODEL), F32),
            pltpu.VMEM((tm, D_MODEL), F32),
            pltpu.VMEM((tm, D_MODEL), BF16),
            pltpu.VMEM((2, TAIL_TK, D_MODEL), F32),
            pltpu.VMEM((2, D_MODEL, TAIL_TF), F32),
            pltpu.SemaphoreType.DMA((2,)),
            pltpu.SemaphoreType.DMA((2,)),
            pltpu.SemaphoreType.DMA((1,)),
            pltpu.SemaphoreType.DMA((1,)),
        ],
        compiler_params=_params(("arbitrary", "arbitrary"), 58),
        name="layer_tail",
    )(*args)
    return out if emit_next else (out[0], None)


def _pos_table():
    quarter = D_MODEL // 4
    omega = 1.0 / (POS_BASE ** (np.arange(quarter, dtype=np.float64) / quarter))
    t = np.arange(SEQ)
    ar = (t // GRID_W).astype(np.float64)[:, None] * omega
    ac = (t % GRID_W).astype(np.float64)[:, None] * omega
    return np.concatenate([np.sin(ar), np.cos(ar), np.sin(ac), np.cos(ac)], axis=-1).astype(np.float32)


_POS = _pos_table()


def _mod_tables(m):
    m = m.reshape(DEPTH, SUBLANES, 6, D_MODEL)
    m = jnp.pad(m, ((0, 0), (0, 0), (0, SUBLANES - 6), (0, 0)))
    return m[:, :BATCH], m[:, BATCH:BATCH + 1]


def _as_batch(a):
    return a.reshape(BATCH, CTX_LEN, a.shape[-1])


def _as_slab(a):
    return a.reshape(1, BATCH * CTX_LEN, a.shape[-1])


def kernel(x, c, ctx, c_ctx, w_mod, b_mod, w_in, lam_re, lam_im, log_step, ssm_b_re, ssm_b_im, ssm_c_re,
           ssm_c_im, d_skip, w_glu, w_ps, w_pf, w_o, ln1_g, ln1_b, w_up, w_down, ln2_g, ln2_b):
    cond8 = jnp.concatenate([c, c_ctx[None], jnp.zeros((SUBLANES - BATCH - 1, D_MODEL), F32)], axis=0)
    mod_lat, mod_ctx = _mod_tables(_modulation(cond8, w_mod, b_mod))

    x_lat, h_lat = _entry(x, jnp.asarray(_POS), mod_lat)
    x_ctx, h_ctx = _entry(_as_slab(ctx), None, mod_ctx)
    h_zero = jnp.zeros((2, SUBLANES, STATE_W), F32)

    for l in range(DEPTH):
        need_ctx = l < DEPTH - 1
        prm = _s5_params(lam_re[l], lam_im[l], log_step[l], ssm_b_re[l], ssm_b_im[l], ssm_c_re[l],
                         ssm_c_im[l])

        def mixer_tail(us, rest, yf, yb, xx, mod, as_batch, as_rows):
            p, q = _channel_dft(rest)
            yq = as_rows(_position_dft(as_batch(p), as_batch(q)))
            s = _glu(us, as_rows(yf), as_rows(yb), d_skip, w_glu, l)
            merged = _merge(s, yq, rest, w_ps, w_pf, l)
            return _tail(merged, xx, mod, w_o, w_up, w_down, ln1_g, ln1_b, ln2_g, ln2_b, l, need_ctx)

        us_lat, rest_lat = _in_proj(h_lat, w_in, l, True)
        if need_ctx:
            us_ctx, rest_ctx = _in_proj(h_ctx, w_in, l, True)
            yf_c, yb_c, h_t = _s5(_as_batch(us_ctx), h_zero, prm, True)
        else:
            (us_ctx,) = _in_proj(h_ctx, w_in, l, False)
            (h_t,) = _s5(_as_batch(us_ctx), h_zero, prm, False)
        yf, yb, _ = _s5(us_lat, h_t, prm, True)
        ident = lambda a: a
        x_lat, h_lat = mixer_tail(us_lat, rest_lat, yf, yb, x_lat, mod_lat, ident, ident)
        if need_ctx:
            x_ctx, h_ctx = mixer_tail(us_ctx, rest_ctx, yf_c, yb_c, x_ctx, mod_ctx, _as_batch, _as_slab)

    return x_lat
```

```python
import functools
import math

import numpy as np
import jax
import jax.numpy as jnp
from jax import lax
from jax.experimental import pallas as pl
from jax.experimental.pallas import tpu as pltpu

D_MODEL = 2048
BATCH = 4
SEQ = 2048
DEPTH = 2
GRID_W = 64
CTX_LEN = 256
SSM_WIDTH = D_MODEL // 2
SSM_GROUP = 16
SSM_GROUPS = SSM_WIDTH // SSM_GROUP
SSM_STATE = 64
FFT_WIDTH = D_MODEL - SSM_WIDTH
FFT_GROUPS = 4
FFT_GROUP = FFT_WIDTH // FFT_GROUPS
IN_WIDTH = SSM_WIDTH + FFT_WIDTH + 2 * D_MODEL
REST_WIDTH = IN_WIDTH - SSM_WIDTH
D_FF = 4 * D_MODEL
ALPHA = (2 * DEPTH) ** 0.25
LN_EPS = 1e-5
POS_BASE = 10000.0

F32 = jnp.float32
BF16 = jnp.bfloat16

SUBLANES = 8
LANES = 128
STATE_W = SSM_GROUPS * SSM_STATE
SCAN_T = 64
S5_KGROUPS = 8
S5_NK = SSM_GROUPS // S5_KGROUPS
S5_KSTATE = S5_KGROUPS * SSM_STATE
MIB = 1024 * 1024

SHIFT1, SCALE1, GATE1, SHIFT2, SCALE2, GATE2 = range(6)


def _params(sem, vmem_mib):
    return pltpu.CompilerParams(dimension_semantics=sem, vmem_limit_bytes=vmem_mib * MIB)


def _dot(a, b):
    return jnp.dot(a, b, preferred_element_type=F32)


def _ln(x):
    mu = jnp.mean(x, axis=-1, keepdims=True)
    xc = x - mu
    var = jnp.mean(xc * xc, axis=-1, keepdims=True)
    return xc * lax.rsqrt(var + LN_EPS)


def _row(ref, j):
    return ref[j:j + 1, :]


def _modulated_ln(x, mod_ref, shift, scale):
    return (_ln(x) * (1.0 + _row(mod_ref, scale)) + _row(mod_ref, shift)).astype(BF16)


EPILOGUE_ROWS = 128


def _for_row_chunks(rows, fn):
    def body(r, carry):
        fn(pl.ds(pl.multiple_of(r * EPILOGUE_ROWS, EPILOGUE_ROWS), EPILOGUE_ROWS))
        return carry

    lax.fori_loop(0, rows // EPILOGUE_ROWS, body, 0)


def _mod_spec(layer, nb):
    if nb == 1:
        return pl.BlockSpec((None, None, SUBLANES, D_MODEL), lambda b, *_: (layer, 0, 0, 0))
    return pl.BlockSpec((None, None, SUBLANES, D_MODEL), lambda b, *_: (layer, b, 0, 0))


def _mod_kernel(c_ref, w_ref, b_ref, o_ref):
    c = c_ref[...]
    sc = c * jax.nn.sigmoid(c)
    o_ref[...] = _dot(sc.astype(BF16), w_ref[...].astype(BF16)) + b_ref[...]


def _modulation(cond8, w_mod, b_mod):
    tn = 1024
    n = 6 * D_MODEL
    return pl.pallas_call(
        _mod_kernel,
        grid=(DEPTH, n // tn),
        in_specs=[
            pl.BlockSpec((SUBLANES, D_MODEL), lambda l, j: (0, 0)),
            pl.BlockSpec((None, D_MODEL, tn), lambda l, j: (l, 0, j)),
            pl.BlockSpec((None, 1, tn), lambda l, j: (l, 0, j)),
        ],
        out_specs=pl.BlockSpec((None, SUBLANES, tn), lambda l, j: (l, 0, j)),
        out_shape=jax.ShapeDtypeStruct((DEPTH, SUBLANES, n), F32),
        compiler_params=_params(("parallel", "parallel"), 40),
        name="adaln_modulation",
    )(cond8, w_mod, b_mod.reshape(DEPTH, 1, n))


def _entry_kernel(has_pos, *refs):
    if has_pos:
        x_ref, p_ref, mod_ref, xo_ref, h_ref = refs
        x = x_ref[...] + p_ref[...]
        xo_ref[...] = x
    else:
        x_ref, mod_ref, h_ref = refs
        x = x_ref[...]
    h_ref[...] = _modulated_ln(x, mod_ref, SHIFT1, SCALE1)


def _entry(x, pos, mod):
    nb, rows, _ = x.shape
    tm = 512
    has_pos = pos is not None
    blk = pl.BlockSpec((None, tm, D_MODEL), lambda i, b: (b, i, 0))
    in_specs = [blk]
    args = [x]
    if has_pos:
        in_specs.append(pl.BlockSpec((tm, D_MODEL), lambda i, b: (i, 0)))
        args.append(pos)
    in_specs.append(pl.BlockSpec((None, None, SUBLANES, D_MODEL), lambda i, b: (0, b, 0, 0)))
    args.append(mod)
    h_shape = jax.ShapeDtypeStruct(x.shape, BF16)
    out = pl.pallas_call(
        functools.partial(_entry_kernel, has_pos),
        grid=(rows // tm, nb),
        in_specs=in_specs,
        out_specs=[blk, blk] if has_pos else [blk],
        out_shape=[jax.ShapeDtypeStruct(x.shape, F32), h_shape] if has_pos else [h_shape],
        compiler_params=_params(("parallel", "parallel"), 40),
        name="entry_ln",
    )(*args)
    return out if has_pos else (x, out[0])


ROW_SPLIT = 2


def _win_kernel(n_us, with_rest, h_ref, w_ref, us_ref, *rest):
    piece = h_ref.shape[0] // ROW_SPLIT

    def project(o_ref):
        w = w_ref[...].astype(BF16)
        for r in range(ROW_SPLIT):
            rs = slice(r * piece, (r + 1) * piece)
            o_ref[rs, :] = _dot(h_ref[rs, :], w).astype(o_ref.dtype)

    if not with_rest:
        project(us_ref)
        return
    rest_ref, = rest
    j = pl.program_id(2)

    @pl.when(j < n_us)
    def _():
        project(us_ref)

    @pl.when(j >= n_us)
    def _():
        project(rest_ref)


def _in_proj(h, w_in, layer, with_rest):
    nb, rows, _ = h.shape
    tm = min(rows, 2048)
    tn = 512
    n_us = SSM_WIDTH // tn
    n_tiles = (IN_WIDTH if with_rest else SSM_WIDTH) // tn
    out_specs = [pl.BlockSpec((None, tm, tn), lambda b, i, j: (b, i, jnp.minimum(j, n_us - 1)))]
    out_shape = [jax.ShapeDtypeStruct((nb, rows, SSM_WIDTH), F32)]
    if with_rest:
        out_specs.append(pl.BlockSpec((None, tm, tn), lambda b, i, j: (b, i, jnp.maximum(j - n_us, 0))))
        out_shape.append(jax.ShapeDtypeStruct((nb, rows, REST_WIDTH), BF16))
    return pl.pallas_call(
        functools.partial(_win_kernel, n_us, with_rest),
        grid=(nb, rows // tm, n_tiles),
        in_specs=[
            pl.BlockSpec((None, tm, D_MODEL), lambda b, i, j: (b, i, 0)),
            pl.BlockSpec((None, D_MODEL, tn), lambda b, i, j: (layer, 0, j)),
        ],
        out_specs=out_specs,
        out_shape=out_shape,
        compiler_params=_params(("parallel", "parallel", "arbitrary"), 48),
        name="in_proj",
    )(h, w_in)


def _s5_kernel(need_y, uf_ref, ub_ref, h0_ref, pf_ref, pb_ref, pft_ref, pbt_ref, a_ref, wd_ref, wr_ref,
               *rest):
    if need_y:
        yf_ref, yb_ref, ht_ref, v_ref, hc_ref, h_ref, ycf_ref, ycb_ref = rest
    else:
        ht_ref, v_ref, hc_ref = rest
    g = pl.program_id(0)
    rows_in = BATCH * SCAN_T
    pair = 2 * SUBLANES

    @pl.when(g == 0)
    def _():
        hc_ref[...] = h0_ref[...]

    uf = uf_ref[...].reshape(rows_in, SSM_WIDTH).astype(BF16)
    ub = ub_ref[...].reshape(rows_in, SSM_WIDTH).astype(BF16)
    up_f = _dot(pf_ref[...], uf).astype(BF16)
    up_b = _dot(pb_ref[...], ub).astype(BF16)
    for k in range(S5_NK):
        cs = slice(k * LANES, (k + 1) * LANES)
        ss = slice(k * S5_KSTATE, (k + 1) * S5_KSTATE)
        re_cols = slice(2 * k * S5_KSTATE, (2 * k + 1) * S5_KSTATE)
        im_cols = slice((2 * k + 1) * S5_KSTATE, (2 * k + 2) * S5_KSTATE)
        lhs = jnp.concatenate([up_f[:, cs], up_b[:, cs]], axis=1)
        v_ref[:, 2 * k * S5_KSTATE:(2 * k + 2) * S5_KSTATE] = _dot(lhs, wd_ref[k])
        ar = a_ref[0, :, ss]
        ai = a_ref[1, :, ss]
        hr = hc_ref[0, :, ss]
        hi = hc_ref[1, :, ss]
        for s2 in range(SCAN_T // 2):
            rows = []
            for s in (2 * s2, 2 * s2 + 1):
                rs = slice(s * SUBLANES, (s + 1) * SUBLANES)
                nr = ar * hr - ai * hi + v_ref[rs, re_cols]
                ni = ar * hi + ai * hr + v_ref[rs, im_cols]
                hr, hi = nr, ni
                rows.append((nr, ni))
            if need_y:
                ps = slice(s2 * pair, (s2 + 1) * pair)
                h_ref[ps, re_cols] = jnp.concatenate([rows[0][0], rows[1][0]], axis=0).astype(BF16)
                h_ref[ps, im_cols] = jnp.concatenate([rows[0][1], rows[1][1]], axis=0).astype(BF16)
        hc_ref[0, :, ss] = hr
        hc_ref[1, :, ss] = hi
        if need_y:
            y = _dot(h_ref[:, 2 * k * S5_KSTATE:(2 * k + 2) * S5_KSTATE], wr_ref[k])
            ycf_ref[:, cs] = y[:, :LANES]
            ycb_ref[:, cs] = y[:, LANES:]
    if need_y:
        yf = _dot(pft_ref[...], ycf_ref[...].astype(BF16))
        yb = _dot(pbt_ref[...], ycb_ref[...].astype(BF16))
        yf_ref[...] = yf.astype(BF16).reshape(BATCH, SCAN_T, SSM_WIDTH)
        yb_ref[...] = yb.astype(BF16).reshape(BATCH, SCAN_T, SSM_WIDTH)

    @pl.when(g == pl.num_programs(0) - 1)
    def _():
        ht_ref[...] = hc_ref[...]


def _scan_perms():
    t = SCAN_T
    pf = np.zeros((2 * BATCH * t, BATCH * t), np.float32)
    pb = np.zeros((2 * BATCH * t, BATCH * t), np.float32)
    for s in range(t):
        for b in range(BATCH):
            pf[s * 2 * BATCH + b, b * t + s] = 1.0
            pb[s * 2 * BATCH + BATCH + b, b * t + (t - 1 - s)] = 1.0
    return pf, pb


_PF, _PB = _scan_perms()


def _s5(u, h0, prm, need_y):
    a, wd, wr = prm
    steps = u.shape[1]
    n = steps // SCAN_T
    rows_in = BATCH * SCAN_T
    rows_sc = 2 * rows_in
    pf = jnp.asarray(_PF, BF16)
    pb = jnp.asarray(_PB, BF16)
    const2 = lambda g: (0, 0)
    once = pl.Buffered(1)
    blk = (BATCH, SCAN_T, SSM_WIDTH)
    fwd_blk = pl.BlockSpec(blk, lambda g: (0, g, 0))
    bwd_blk = pl.BlockSpec(blk, lambda g: (0, n - 1 - g, 0))
    state_spec = pl.BlockSpec((2, SUBLANES, STATE_W), lambda g: (0, 0, 0))
    in_specs = [
        fwd_blk, bwd_blk, state_spec,
        pl.BlockSpec((rows_sc, rows_in), const2),
        pl.BlockSpec((rows_sc, rows_in), const2),
        pl.BlockSpec((rows_in, rows_sc), const2),
        pl.BlockSpec((rows_in, rows_sc), const2),
        state_spec,
        pl.BlockSpec((S5_NK, 2 * LANES, 2 * S5_KSTATE), lambda g: (0, 0, 0), pipeline_mode=once),
        pl.BlockSpec((S5_NK, 2 * S5_KSTATE, 2 * LANES), lambda g: (0, 0, 0), pipeline_mode=once),
    ]
    state_shape = jax.ShapeDtypeStruct((2, SUBLANES, STATE_W), F32)
    scratch = [pltpu.VMEM((rows_sc, 2 * STATE_W), F32), pltpu.VMEM((2, SUBLANES, STATE_W), F32)]
    if need_y:
        out_specs = [fwd_blk, bwd_blk, state_spec]
        y_shape = jax.ShapeDtypeStruct((BATCH, steps, SSM_WIDTH), BF16)
        out_shape = [y_shape, y_shape, state_shape]
        scratch += [pltpu.VMEM((rows_sc, 2 * STATE_W), BF16), pltpu.VMEM((rows_sc, SSM_WIDTH), F32),
                    pltpu.VMEM((rows_sc, SSM_WIDTH), F32)]
    else:
        out_specs = [state_spec]
        out_shape = [state_shape]
    return pl.pallas_call(
        functools.partial(_s5_kernel, need_y),
        grid=(n,),
        in_specs=in_specs,
        out_specs=out_specs,
        out_shape=out_shape,
        scratch_shapes=scratch,
        compiler_params=_params(("arbitrary",), 52),
        name="s5_scan",
    )(u, u, h0, pf, pb, pf.T, pb.T, a, wd, wr)


def _s5_params(lam_re, lam_im, log_step, b_re, b_im, c_re, c_im):
    dt = jnp.exp(log_step)[..., None]
    mag = jnp.exp(lam_re * dt)
    ang = lam_im * dt
    abar_re, abar_im = mag * jnp.cos(ang), mag * jnp.sin(ang)
    den = lam_re * lam_re + lam_im * lam_im
    nr, ni = abar_re - 1.0, abar_im
    coef_re = (nr * lam_re + ni * lam_im) / den
    coef_im = (ni * lam_re - nr * lam_im) / den
    bb_re = coef_re[..., None] * b_re - coef_im[..., None] * b_im
    bb_im = coef_re[..., None] * b_im + coef_im[..., None] * b_re
    bb = jnp.stack([bb_re, bb_im])
    bb = jnp.swapaxes(bb, -1, -2).reshape(-1, SSM_STATE)
    own = ((jnp.arange(bb.shape[0]) // SSM_GROUP) % S5_KGROUPS)[:, None] == (
        jnp.arange(S5_KSTATE) // SSM_STATE)[None, :]
    bb = jnp.where(own, jnp.tile(bb, (1, S5_KGROUPS)), 0.0)
    bb = bb.reshape(2, 2, S5_NK, LANES, S5_KSTATE)
    wd = jnp.concatenate([bb[0], bb[1]], axis=-1)
    wd = jnp.transpose(wd, (1, 0, 2, 3)).reshape(S5_NK, 2 * LANES, 2 * S5_KSTATE).astype(BF16)
    cc = jnp.stack([c_re, -c_im])
    cc = jnp.swapaxes(cc, -1, -2).reshape(-1, SSM_GROUP)
    own = ((jnp.arange(cc.shape[0]) // SSM_STATE) % S5_KGROUPS)[:, None] == (
        jnp.arange(LANES) // SSM_GROUP)[None, :]
    cc = jnp.where(own, jnp.tile(cc, (1, S5_KGROUPS)), 0.0)
    cc = cc.reshape(2, 2, S5_NK, S5_KSTATE, LANES)
    wr = jnp.concatenate([cc[:, 0], cc[:, 1]], axis=-1)
    wr = jnp.transpose(wr, (1, 0, 2, 3)).reshape(S5_NK, 2 * S5_KSTATE, 2 * LANES).astype(BF16)
    a = jnp.stack([abar_re.reshape(2, STATE_W), abar_im.reshape(2, STATE_W)])
    a = jnp.repeat(a, BATCH, axis=1)
    return a, wd, wr


def _dft_tables(n):
    j = np.arange(n, dtype=np.int64)
    ang = 2.0 * np.pi * ((j[:, None] * j[None, :]) % n).astype(np.float64) / n
    s = 1.0 / math.sqrt(n)
    return (np.cos(ang) * s).astype(np.float32), (np.sin(ang) * s).astype(np.float32)


_CH_COS, _CH_SIN = _dft_tables(FFT_GROUP)
_CH_CS = np.concatenate([_CH_COS, _CH_SIN], axis=1)


def _chdft_kernel(u_ref, w_ref, p_ref, q_ref):
    w = w_ref[...].astype(BF16)
    for g in range(FFT_GROUPS):
        cs = slice(g * FFT_GROUP, (g + 1) * FFT_GROUP)
        r = _dot(u_ref[:, cs], w)
        p_ref[:, cs] = r[:, :FFT_GROUP].astype(BF16)
        q_ref[:, cs] = r[:, FFT_GROUP:].astype(BF16)


def _channel_dft(rest):
    nb, rows, _ = rest.shape
    tm = min(rows, 2048)
    shape = jax.ShapeDtypeStruct((nb, rows, FFT_WIDTH), BF16)
    blk = pl.BlockSpec((None, tm, FFT_WIDTH), lambda b, i: (b, i, 0))
    return pl.pallas_call(
        _chdft_kernel,
        grid=(nb, rows // tm),
        in_specs=[blk, pl.BlockSpec((FFT_GROUP, 2 * FFT_GROUP), lambda b, i: (0, 0))],
        out_specs=[blk, blk],
        out_shape=[shape, shape],
        compiler_params=_params(("parallel", "parallel"), 40),
        name="channel_dft",
    )(rest, jnp.asarray(_CH_CS))


DFT_RADIX = 4
DFT_ILV = 256
_QUARTER_TURN = ((1, 0), (0, 1), (-1, 0), (0, -1))


def _radix_tables(n):
    nq = n // DFT_RADIX
    k = np.arange(nq, dtype=np.int64)
    s = 1.0 / math.sqrt(n)
    cos, sin = [], []
    for r in range(DFT_RADIX):
        j = DFT_RADIX * k + r
        ang = 2.0 * np.pi * ((j[:, None] * k[None, :]) % n).astype(np.float64) / n
        cos.append(np.cos(ang) * s)
        sin.append(np.sin(ang) * s)
    return np.stack(cos).astype(np.float32), np.stack(sin).astype(np.float32)


def _interleave_perm():
    per = DFT_ILV // DFT_RADIX
    perm = np.zeros((DFT_ILV, DFT_ILV), np.float32)
    for r in range(DFT_RADIX):
        for m in range(per):
            perm[DFT_RADIX * m + r, r * per + m] = 1.0
    return perm


_RADIX_TABLES = {n: _radix_tables(n) for n in (CTX_LEN, SEQ)}
_ILV_PERM = _interleave_perm()


def _posdft_kernel(n, p_ref, q_ref, c_ref, s_ref, perm_ref, o_ref, comb_ref, y_ref):
    nq = n // DFT_RADIX
    chunk = min(nq, EPILOGUE_ROWS)

    def combine(i, carry):
        rs = pl.ds(pl.multiple_of(i * chunk, chunk), chunk)
        p = [p_ref[pl.ds(pl.multiple_of(q * nq + i * chunk, chunk), chunk), :].astype(F32)
             for q in range(DFT_RADIX)]
        qq = [q_ref[pl.ds(pl.multiple_of(q * nq + i * chunk, chunk), chunk), :].astype(F32)
              for q in range(DFT_RADIX)]
        def signed_sum(terms):
            acc = None
            for sign, v in terms:
                if acc is None:
                    acc = v if sign > 0 else -v
                else:
                    acc = acc + v if sign > 0 else acc - v
            return acc

        for r in range(DFT_RADIX):
            pr, qr = [], []
            for q in range(DFT_RADIX):
                cs, sn = _QUARTER_TURN[(r * q) % DFT_RADIX]
                if cs:
                    pr.append((cs, p[q]))
                    qr.append((cs, qq[q]))
                if sn:
                    pr.append((-sn, qq[q]))
                    qr.append((sn, p[q]))
            comb_ref[2 * r, rs, :] = signed_sum(pr).astype(BF16)
            comb_ref[2 * r + 1, rs, :] = signed_sum(qr).astype(BF16)
        return carry

    lax.fori_loop(0, nq // chunk, combine, 0)
    for r in range(DFT_RADIX):
        y_ref[r] = (_dot(c_ref[r].astype(BF16), comb_ref[2 * r])
                    - _dot(s_ref[r].astype(BF16), comb_ref[2 * r + 1]))
    per = DFT_ILV // DFT_RADIX
    for blk in range(n // DFT_ILV):
        slab = jnp.concatenate([y_ref[r, blk * per:(blk + 1) * per, :] for r in range(DFT_RADIX)], axis=0)
        o_ref[blk * DFT_ILV:(blk + 1) * DFT_ILV, :] = _dot(perm_ref[...], slab.astype(BF16)).astype(BF16)


def _position_dft(p, q):
    nb, steps, _ = p.shape
    cos, sin = _RADIX_TABLES[steps]
    nq = steps // DFT_RADIX
    x_blk = pl.BlockSpec((None, steps, FFT_WIDTH), lambda b: (b, 0, 0))
    t_blk = pl.BlockSpec((DFT_RADIX, nq, nq), lambda b: (0, 0, 0), pipeline_mode=pl.Buffered(1))
    return pl.pallas_call(
        functools.partial(_posdft_kernel, steps),
        grid=(nb,),
        in_specs=[x_blk, x_blk, t_blk, t_blk, pl.BlockSpec((DFT_ILV, DFT_ILV), lambda b: (0, 0))],
        out_specs=x_blk,
        out_shape=jax.ShapeDtypeStruct((nb, steps, FFT_WIDTH), BF16),
        scratch_shapes=[pltpu.VMEM((2 * DFT_RADIX, nq, FFT_WIDTH), BF16),
                        pltpu.VMEM((DFT_RADIX, nq, FFT_WIDTH), F32)],
        compiler_params=_params(("parallel",), 56),
        name="position_dft",
    )(p, q, jnp.asarray(cos), jnp.asarray(sin), jnp.asarray(_ILV_PERM, BF16))


def _glu_kernel(us_ref, yf_ref, yb_ref, dsk_ref, w_ref, o_ref):
    ys = dsk_ref[...] * us_ref[...] + yf_ref[...].astype(F32) + yb_ref[...].astype(F32)
    g = jax.nn.gelu(ys)
    z = _dot(g.astype(BF16), w_ref[...].astype(BF16))
    o_ref[...] = (g * jax.nn.sigmoid(z)).astype(BF16)


def _glu(us, yf, yb, d_skip, w_glu, layer):
    nb, rows, _ = us.shape
    tm = 1024
    blk = pl.BlockSpec((None, tm, SSM_WIDTH), lambda b, i: (b, i, 0))
    return pl.pallas_call(
        _glu_kernel,
        grid=(nb, rows // tm),
        in_specs=[blk, blk, blk,
                  pl.BlockSpec((None, 1, SSM_WIDTH), lambda b, i: (layer, 0, 0)),
                  pl.BlockSpec((None, SSM_WIDTH, SSM_WIDTH), lambda b, i: (layer, 0, 0))],
        out_specs=blk,
        out_shape=jax.ShapeDtypeStruct((nb, rows, SSM_WIDTH), BF16),
        compiler_params=_params(("parallel", "parallel"), 40),
        name="s5_glu",
    )(us, yf, yb, d_skip.reshape(DEPTH, 1, SSM_WIDTH), w_glu)


MERGE_SPLIT = 4


def _merge_kernel(s_ref, yq_ref, gs_ref, gf_ref, wps_ref, wpf_ref, o_ref):
    wps = wps_ref[...].astype(BF16)
    wpf = wpf_ref[...].astype(BF16)
    piece = o_ref.shape[0] // MERGE_SPLIT
    for r in range(MERGE_SPLIT):
        rs = slice(r * piece, (r + 1) * piece)
        ps = _dot(s_ref[rs, :], wps)
        pf = _dot(yq_ref[rs, :], wpf)
        o_ref[rs, :] = (jax.nn.sigmoid(gs_ref[rs, :].astype(F32)) * ps
                        + jax.nn.sigmoid(gf_ref[rs, :].astype(F32)) * pf).astype(BF16)


def _merge(s, yq, rest, w_ps, w_pf, layer):
    nb, rows, _ = s.shape
    tm = min(rows, 2048)
    tn = 512
    gs_off = FFT_WIDTH // tn
    gf_off = (FFT_WIDTH + D_MODEL) // tn
    row_blk = pl.BlockSpec((None, tm, SSM_WIDTH), lambda b, i, j: (b, i, 0))
    w_blk = pl.BlockSpec((None, SSM_WIDTH, tn), lambda b, i, j: (layer, 0, j))
    return pl.pallas_call(
        _merge_kernel,
        grid=(nb, rows // tm, D_MODEL // tn),
        in_specs=[
            row_blk, row_blk,
            pl.BlockSpec((None, tm, tn), lambda b, i, j: (b, i, gs_off + j)),
            pl.BlockSpec((None, tm, tn), lambda b, i, j: (b, i, gf_off + j)),
            w_blk, w_blk,
        ],
        out_specs=pl.BlockSpec((None, tm, tn), lambda b, i, j: (b, i, j)),
        out_shape=jax.ShapeDtypeStruct((nb, rows, D_MODEL), BF16),
        compiler_params=_params(("parallel", "parallel", "arbitrary"), 48),
        name="gated_merge",
    )(s, yq, rest, rest, w_ps, w_pf)


TAIL_TM = 1024
TAIL_TK = 512
TAIL_TF = 512


def _tail_kernel(layer, emit_next, merged_ref, x_hbm, wo_hbm, wu_hbm, wd_hbm, mod_ref, *refs):
    if emit_next:
        modn_ref, refs = refs[0], refs[1:]
    g1_ref, b1_ref, g2_ref, b2_ref, out_hbm = refs[:5]
    refs = refs[5:]
    if emit_next:
        hn_ref, refs = refs[0], refs[1:]
    xbuf, acc, h2, act, wa, wb, sem_a, sem_b, sem_x, sem_o = refs
    b = pl.program_id(0)
    i = pl.program_id(1)
    first = jnp.logical_and(b == 0, i == 0)
    last = jnp.logical_and(b == pl.num_programs(0) - 1, i == pl.num_programs(1) - 1)
    nk = D_MODEL // TAIL_TK
    nf = D_FF // TAIL_TF
    assert nk % 2 == 0
    rows = pl.ds(pl.multiple_of(i * TAIL_TM, TAIL_TM), TAIL_TM)

    def wo_copy(t, slot):
        return pltpu.make_async_copy(wo_hbm.at[layer, pl.ds(t * TAIL_TK, TAIL_TK), :], wa.at[slot], sem_a.at[slot])

    def wd_copy(f, slot):
        src = wd_hbm.at[layer, pl.ds(pl.multiple_of(f * TAIL_TK, TAIL_TK), TAIL_TK), :]
        return pltpu.make_async_copy(src, wa.at[slot], sem_a.at[slot])

    def wu_copy(f, slot):
        src = wu_hbm.at[layer, :, pl.ds(pl.multiple_of(f * TAIL_TF, TAIL_TF), TAIL_TF)]
        return pltpu.make_async_copy(src, wb.at[slot], sem_b.at[slot])

    def x_copy():
        return pltpu.make_async_copy(x_hbm.at[b, rows, :], xbuf, sem_x.at[0])

    def out_copy():
        return pltpu.make_async_copy(xbuf, out_hbm.at[b, rows, :], sem_o.at[0])

    wo_copy(0, 0).start()
    wo_copy(1, 1).start()
    wu_copy(0, 0).start()
    wu_copy(1, 1).start()

    for k in range(nk):
        slot = k % 2
        wo_copy(k, slot).wait()
        lhs = merged_ref[:, k * TAIL_TK:(k + 1) * TAIL_TK]
        if k == 0:
            acc[...] = _dot(lhs, wa[slot].astype(BF16))
        else:
            acc[...] += _dot(lhs, wa[slot].astype(BF16))
        if k + 2 < nk:
            wo_copy(k + 2, slot).start()
        else:
            wd_copy(k + 2 - nk, slot).start()
        if k == 0:
            @pl.when(jnp.logical_not(first))
            def _():
                out_copy().wait()

            x_copy().start()
    x_copy().wait()

    def finish1(rs):
        y = ALPHA * xbuf[rs, :] + _row(mod_ref, GATE1) * acc[rs, :]
        x1 = _ln(y) * g1_ref[...] + b1_ref[...]
        xbuf[rs, :] = x1
        h2[rs, :] = _modulated_ln(x1, mod_ref, SHIFT2, SCALE2)
        acc[rs, :] = jnp.zeros((EPILOGUE_ROWS, D_MODEL), F32)

    _for_row_chunks(TAIL_TM, finish1)

    def up_tile(slot):
        a = jnp.maximum(_dot(h2[...], wb[slot].astype(BF16)), 0.0)
        act[slot] = (a * a).astype(BF16)

    def down_tile(slot):
        acc[...] += _dot(act[slot], wa[slot].astype(BF16))

    def trip(f, slot):
        nxt = 1 - slot
        wu_copy(f + 1, nxt).wait()
        wd_copy(f, slot).wait()
        up_tile(nxt)
        down_tile(slot)
        if isinstance(f, int):
            if f + 3 < nf:
                wu_copy(f + 3, nxt).start()
            if f + 2 < nf:
                wd_copy(f + 2, slot).start()
        else:
            @pl.when(f + 3 < nf)
            def _():
                wu_copy(f + 3, nxt).start()

            @pl.when(f + 2 < nf)
            def _():
                wd_copy(f + 2, slot).start()

    wu_copy(0, 0).wait()
    up_tile(0)
    wu_copy(2, 0).start()

    def trip_pair(p, carry):
        trip(2 * p, 0)
        trip(2 * p + 1, 1)
        return carry

    lax.fori_loop(0, (nf - 1) // 2, trip_pair, 0)
    for f in range(2 * ((nf - 1) // 2), nf - 1):
        trip(f, f % 2)
    wd_copy(nf - 1, (nf - 1) % 2).wait()
    down_tile((nf - 1) % 2)

    def finish2(rs):
        y = ALPHA * xbuf[rs, :] + _row(mod_ref, GATE2) * acc[rs, :]
        x2 = _ln(y) * g2_ref[...] + b2_ref[...]
        xbuf[rs, :] = x2
        if emit_next:
            hn_ref[rs, :] = _modulated_ln(x2, modn_ref, SHIFT1, SCALE1)

    _for_row_chunks(TAIL_TM, finish2)
    out_copy().start()

    @pl.when(last)
    def _():
        out_copy().wait()


def _tail(merged, x, mod, w_o, w_up, w_down, g1, b1, g2, b2, layer, emit_next):
    nb, rows, _ = x.shape
    tm = TAIL_TM
    row_blk = pl.BlockSpec((None, tm, D_MODEL), lambda b_, i: (b_, i, 0))
    vec = pl.BlockSpec((None, 1, D_MODEL), lambda b_, i: (layer, 0, 0))
    hbm = pl.BlockSpec(memory_space=pl.ANY)
    in_specs = [row_blk, hbm, hbm, hbm, hbm, _mod_spec(layer, nb)]
    args = [merged, x, w_o, w_up, w_down, mod]
    if emit_next:
        in_specs.append(_mod_spec(layer + 1, nb))
        args.append(mod)
    in_specs += [vec, vec, vec, vec]
    args += [v.reshape(DEPTH, 1, D_MODEL) for v in (g1, b1, g2, b2)]
    out_specs = [hbm]
    out_shape = [jax.ShapeDtypeStruct(x.shape, F32)]
    if emit_next:
        out_specs.append(row_blk)
        out_shape.append(jax.ShapeDtypeStruct(x.shape, BF16))
    out = pl.pallas_call(
        functools.partial(_tail_kernel, layer, emit_next),
        grid=(nb, rows // tm),
        in_specs=in_specs,
        out_specs=out_specs,
        out_shape=out_shape,
        scratch_shapes=[
            pltpu.VMEM((tm, D_MODEL), F32),
            pltpu.VMEM((tm, D_MODEL), F32),
            pltpu.VMEM((tm, D_MODEL), BF16),
            pltpu.VMEM((2, tm, TAIL_TF), BF16),
            pltpu.VMEM((2, TAIL_TK, D_MODEL), F32),
            pltpu.VMEM((2, D_MODEL, TAIL_TF), F32),
            pltpu.SemaphoreType.DMA((2,)),
            pltpu.SemaphoreType.DMA((2,)),
            pltpu.SemaphoreType.DMA((1,)),
            pltpu.SemaphoreType.DMA((1,)),
        ],
        compiler_params=_params(("arbitrary", "arbitrary"), 58),
        name="layer_tail",
    )(*args)
    return out if emit_next else (out[0], None)


def _pos_table():
    quarter = D_MODEL // 4
    omega = 1.0 / (POS_BASE ** (np.arange(quarter, dtype=np.float64) / quarter))
    t = np.arange(SEQ)
    ar = (t // GRID_W).astype(np.float64)[:, None] * omega
    ac = (t % GRID_W).astype(np.float64)[:, None] * omega
    return np.concatenate([np.sin(ar), np.cos(ar), np.sin(ac), np.cos(ac)], axis=-1).astype(np.float32)


_POS = _pos_table()


def _mod_tables(m):
    m = m.reshape(DEPTH, SUBLANES, 6, D_MODEL)
    m = jnp.pad(m, ((0, 0), (0, 0), (0, SUBLANES - 6), (0, 0)))
    return m[:, :BATCH], m[:, BATCH:BATCH + 1]


def _as_batch(a):
    return a.reshape(BATCH, CTX_LEN, a.shape[-1])


def _as_slab(a):
    return a.reshape(1, BATCH * CTX_LEN, a.shape[-1])


def kernel(x, c, ctx, c_ctx, w_mod, b_mod, w_in, lam_re, lam_im, log_step, ssm_b_re, ssm_b_im, ssm_c_re,
           ssm_c_im, d_skip, w_glu, w_ps, w_pf, w_o, ln1_g, ln1_b, w_up, w_down, ln2_g, ln2_b):
    cond8 = jnp.concatenate([c, c_ctx[None], jnp.zeros((SUBLANES - BATCH - 1, D_MODEL), F32)], axis=0)
    mod_lat, mod_ctx = _mod_tables(_modulation(cond8, w_mod, b_mod))

    x_lat, h_lat = _entry(x, jnp.asarray(_POS), mod_lat)
    x_ctx, h_ctx = _entry(_as_slab(ctx), None, mod_ctx)
    h_zero = jnp.zeros((2, SUBLANES, STATE_W), F32)

    for l in range(DEPTH):
        need_ctx = l < DEPTH - 1
        prm = _s5_params(lam_re[l], lam_im[l], log_step[l], ssm_b_re[l], ssm_b_im[l], ssm_c_re[l],
                         ssm_c_im[l])

        def mixer_tail(us, rest, yf, yb, xx, mod, as_batch, as_rows):
            p, q = _channel_dft(rest)
            yq = as_rows(_position_dft(as_batch(p), as_batch(q)))
            s = _glu(us, as_rows(yf), as_rows(yb), d_skip, w_glu, l)
            merged = _merge(s, yq, rest, w_ps, w_pf, l)
            return _tail(merged, xx, mod, w_o, w_up, w_down, ln1_g, ln1_b, ln2_g, ln2_b, l, need_ctx)

        us_lat, rest_lat = _in_proj(h_lat, w_in, l, True)
        if need_ctx:
            us_ctx, rest_ctx = _in_proj(h_ctx, w_in, l, True)
            yf_c, yb_c, h_t = _s5(_as_batch(us_ctx), h_zero, prm, True)
        else:
            (us_ctx,) = _in_proj(h_ctx, w_in, l, False)
            (h_t,) = _s5(_as_batch(us_ctx), h_zero, prm, False)
        yf, yb, _ = _s5(us_lat, h_t, prm, True)
        ident = lambda a: a
        x_lat, h_lat = mixer_tail(us_lat, rest_lat, yf, yb, x_lat, mod_lat, ident, ident)
        if need_ctx:
            x_ctx, h_ctx = mixer_tail(us_ctx, rest_ctx, yf_c, yb_c, x_ctx, mod_ctx, _as_batch, _as_slab)

    return x_lat
```

```python
import functools
import math

import numpy as np
import jax
import jax.numpy as jnp
from jax import lax
from jax.experimental import pallas as pl
from jax.experimental.pallas import tpu as pltpu

D_MODEL = 2048
BATCH = 4
SEQ = 2048
DEPTH = 2
GRID_W = 64
CTX_LEN = 256
SSM_WIDTH = D_MODEL // 2
SSM_GROUP = 16
SSM_GROUPS = SSM_WIDTH // SSM_GROUP
SSM_STATE = 64
FFT_WIDTH = D_MODEL - SSM_WIDTH
FFT_GROUPS = 4
FFT_GROUP = FFT_WIDTH // FFT_GROUPS
IN_WIDTH = SSM_WIDTH + FFT_WIDTH + 2 * D_MODEL
REST_WIDTH = IN_WIDTH - SSM_WIDTH
D_FF = 4 * D_MODEL
ALPHA = (2 * DEPTH) ** 0.25
LN_EPS = 1e-5
POS_BASE = 10000.0

F32 = jnp.float32
BF16 = jnp.bfloat16

SUBLANES = 8
LANES = 128
STATE_W = SSM_GROUPS * SSM_STATE
SCAN_T = 64
S5_KGROUPS = 8
S5_NK = SSM_GROUPS // S5_KGROUPS
S5_KSTATE = S5_KGROUPS * SSM_STATE
MIB = 1024 * 1024

SHIFT1, SCALE1, GATE1, SHIFT2, SCALE2, GATE2 = range(6)


def _params(sem, vmem_mib):
    return pltpu.CompilerParams(dimension_semantics=sem, vmem_limit_bytes=vmem_mib * MIB)


def _dot(a, b):
    return jnp.dot(a, b, preferred_element_type=F32)


def _ln(x):
    mu = jnp.mean(x, axis=-1, keepdims=True)
    xc = x - mu
    var = jnp.mean(xc * xc, axis=-1, keepdims=True)
    return xc * lax.rsqrt(var + LN_EPS)


def _row(ref, j):
    return ref[j:j + 1, :]


def _modulated_ln(x, mod_ref, shift, scale):
    return (_ln(x) * (1.0 + _row(mod_ref, scale)) + _row(mod_ref, shift)).astype(BF16)


EPILOGUE_ROWS = 128


def _for_row_chunks(rows, fn):
    def body(r, carry):
        fn(pl.ds(pl.multiple_of(r * EPILOGUE_ROWS, EPILOGUE_ROWS), EPILOGUE_ROWS))
        return carry

    lax.fori_loop(0, rows // EPILOGUE_ROWS, body, 0)


def _mod_spec(layer, nb):
    if nb == 1:
        return pl.BlockSpec((None, None, SUBLANES, D_MODEL), lambda b, *_: (layer, 0, 0, 0))
    return pl.BlockSpec((None, None, SUBLANES, D_MODEL), lambda b, *_: (layer, b, 0, 0))


def _mod_kernel(c_ref, w_ref, b_ref, o_ref):
    c = c_ref[...]
    sc = c * jax.nn.sigmoid(c)
    o_ref[...] = _dot(sc.astype(BF16), w_ref[...].astype(BF16)) + b_ref[...]


def _modulation(cond8, w_mod, b_mod):
    tn = 1024
    n = 6 * D_MODEL
    return pl.pallas_call(
        _mod_kernel,
        grid=(DEPTH, n // tn),
        in_specs=[
            pl.BlockSpec((SUBLANES, D_MODEL), lambda l, j: (0, 0)),
            pl.BlockSpec((None, D_MODEL, tn), lambda l, j: (l, 0, j)),
            pl.BlockSpec((None, 1, tn), lambda l, j: (l, 0, j)),
        ],
        out_specs=pl.BlockSpec((None, SUBLANES, tn), lambda l, j: (l, 0, j)),
        out_shape=jax.ShapeDtypeStruct((DEPTH, SUBLANES, n), F32),
        compiler_params=_params(("parallel", "parallel"), 40),
        name="adaln_modulation",
    )(cond8, w_mod, b_mod.reshape(DEPTH, 1, n))


def _entry_kernel(has_pos, *refs):
    if has_pos:
        x_ref, p_ref, mod_ref, xo_ref, h_ref = refs
        x = x_ref[...] + p_ref[...]
        xo_ref[...] = x
    else:
        x_ref, mod_ref, h_ref = refs
        x = x_ref[...]
    h_ref[...] = _modulated_ln(x, mod_ref, SHIFT1, SCALE1)


def _entry(x, pos, mod):
    nb, rows, _ = x.shape
    tm = 512
    has_pos = pos is not None
    blk = pl.BlockSpec((None, tm, D_MODEL), lambda i, b: (b, i, 0))
    in_specs = [blk]
    args = [x]
    if has_pos:
        in_specs.append(pl.BlockSpec((tm, D_MODEL), lambda i, b: (i, 0)))
        args.append(pos)
    in_specs.append(pl.BlockSpec((None, None, SUBLANES, D_MODEL), lambda i, b: (0, b, 0, 0)))
    args.append(mod)
    h_shape = jax.ShapeDtypeStruct(x.shape, BF16)
    out = pl.pallas_call(
        functools.partial(_entry_kernel, has_pos),
        grid=(rows // tm, nb),
        in_specs=in_specs,
        out_specs=[blk, blk] if has_pos else [blk],
        out_shape=[jax.ShapeDtypeStruct(x.shape, F32), h_shape] if has_pos else [h_shape],
        compiler_params=_params(("parallel", "parallel"), 40),
        name="entry_ln",
    )(*args)
    return out if has_pos else (x, out[0])


ROW_SPLIT = 2


def _win_kernel(n_us, with_rest, h_ref, w_ref, us_ref, *rest):
    piece = h_ref.shape[0] // ROW_SPLIT

    def project(o_ref):
        w = w_ref[...].astype(BF16)
        for r in range(ROW_SPLIT):
            rs = slice(r * piece, (r + 1) * piece)
            o_ref[rs, :] = _dot(h_ref[rs, :], w).astype(o_ref.dtype)

    if not with_rest:
        project(us_ref)
        return
    rest_ref, = rest
    j = pl.program_id(2)

    @pl.when(j < n_us)
    def _():
        project(us_ref)

    @pl.when(j >= n_us)
    def _():
        project(rest_ref)


def _in_proj(h, w_in, layer, with_rest):
    nb, rows, _ = h.shape
    tm = min(rows, 2048)
    tn = 512
    n_us = SSM_WIDTH // tn
    n_tiles = (IN_WIDTH if with_rest else SSM_WIDTH) // tn
    out_specs = [pl.BlockSpec((None, tm, tn), lambda b, i, j: (b, i, jnp.minimum(j, n_us - 1)))]
    out_shape = [jax.ShapeDtypeStruct((nb, rows, SSM_WIDTH), F32)]
    if with_rest:
        out_specs.append(pl.BlockSpec((None, tm, tn), lambda b, i, j: (b, i, jnp.maximum(j - n_us, 0))))
        out_shape.append(jax.ShapeDtypeStruct((nb, rows, REST_WIDTH), BF16))
    return pl.pallas_call(
        functools.partial(_win_kernel, n_us, with_rest),
        grid=(nb, rows // tm, n_tiles),
        in_specs=[
            pl.BlockSpec((None, tm, D_MODEL), lambda b, i, j: (b, i, 0)),
            pl.BlockSpec((None, D_MODEL, tn), lambda b, i, j: (layer, 0, j)),
        ],
        out_specs=out_specs,
        out_shape=out_shape,
        compiler_params=_params(("parallel", "parallel", "arbitrary"), 48),
        name="in_proj",
    )(h, w_in)


def _s5_kernel(need_y, uf_ref, ub_ref, h0_ref, pf_ref, pb_ref, pft_ref, pbt_ref, a_ref, wd_ref, wr_ref,
               *rest):
    if need_y:
        yf_ref, yb_ref, ht_ref, v_ref, hc_ref, h_ref, ycf_ref, ycb_ref = rest
    else:
        ht_ref, v_ref, hc_ref = rest
    g = pl.program_id(0)
    rows_in = BATCH * SCAN_T
    pair = 2 * SUBLANES

    @pl.when(g == 0)
    def _():
        hc_ref[...] = h0_ref[...]

    uf = uf_ref[...].reshape(rows_in, SSM_WIDTH).astype(BF16)
    ub = ub_ref[...].reshape(rows_in, SSM_WIDTH).astype(BF16)
    up_f = _dot(pf_ref[...], uf).astype(BF16)
    up_b = _dot(pb_ref[...], ub).astype(BF16)

    def slab_cols(k):
        return slice(2 * k * S5_KSTATE, (2 * k + 2) * S5_KSTATE)

    def drive(k):
        cs = slice(k * LANES, (k + 1) * LANES)
        lhs = jnp.concatenate([up_f[:, cs], up_b[:, cs]], axis=1)
        v_ref[:, slab_cols(k)] = _dot(lhs, wd_ref[k])

    def scan(k):
        ss = slice(k * S5_KSTATE, (k + 1) * S5_KSTATE)
        re_cols = slice(2 * k * S5_KSTATE, (2 * k + 1) * S5_KSTATE)
        im_cols = slice((2 * k + 1) * S5_KSTATE, (2 * k + 2) * S5_KSTATE)
        ar = a_ref[0, :, ss]
        ai = a_ref[1, :, ss]
        hr = hc_ref[0, :, ss]
        hi = hc_ref[1, :, ss]
        for s2 in range(SCAN_T // 2):
            rows = []
            for s in (2 * s2, 2 * s2 + 1):
                rs = slice(s * SUBLANES, (s + 1) * SUBLANES)
                nr = ar * hr - ai * hi + v_ref[rs, re_cols]
                ni = ar * hi + ai * hr + v_ref[rs, im_cols]
                hr, hi = nr, ni
                rows.append((nr, ni))
            if need_y:
                ps = slice(s2 * pair, (s2 + 1) * pair)
                h_ref[ps, re_cols] = jnp.concatenate([rows[0][0], rows[1][0]], axis=0).astype(BF16)
                h_ref[ps, im_cols] = jnp.concatenate([rows[0][1], rows[1][1]], axis=0).astype(BF16)
        hc_ref[0, :, ss] = hr
        hc_ref[1, :, ss] = hi

    def readout(k):
        cs = slice(k * LANES, (k + 1) * LANES)
        y = _dot(h_ref[:, slab_cols(k)], wr_ref[k])
        ycf_ref[:, cs] = y[:, :LANES]
        ycb_ref[:, cs] = y[:, LANES:]

    lag = 1
    for k in range(S5_NK + lag):
        if k < S5_NK:
            drive(k)
            scan(k)
        if need_y and k >= lag:
            readout(k - lag)
    if need_y:
        yf = _dot(pft_ref[...], ycf_ref[...].astype(BF16))
        yb = _dot(pbt_ref[...], ycb_ref[...].astype(BF16))
        yf_ref[...] = yf.astype(BF16).reshape(BATCH, SCAN_T, SSM_WIDTH)
        yb_ref[...] = yb.astype(BF16).reshape(BATCH, SCAN_T, SSM_WIDTH)

    @pl.when(g == pl.num_programs(0) - 1)
    def _():
        ht_ref[...] = hc_ref[...]


def _scan_perms():
    t = SCAN_T
    pf = np.zeros((2 * BATCH * t, BATCH * t), np.float32)
    pb = np.zeros((2 * BATCH * t, BATCH * t), np.float32)
    for s in range(t):
        for b in range(BATCH):
            pf[s * 2 * BATCH + b, b * t + s] = 1.0
            pb[s * 2 * BATCH + BATCH + b, b * t + (t - 1 - s)] = 1.0
    return pf, pb


_PF, _PB = _scan_perms()


def _s5(u, h0, prm, layer, need_y):
    a, wd, wr = prm
    steps = u.shape[1]
    n = steps // SCAN_T
    rows_in = BATCH * SCAN_T
    rows_sc = 2 * rows_in
    pf = jnp.asarray(_PF, BF16)
    pb = jnp.asarray(_PB, BF16)
    const2 = lambda g: (0, 0)
    once = pl.Buffered(1)
    blk = (BATCH, SCAN_T, SSM_WIDTH)
    fwd_blk = pl.BlockSpec(blk, lambda g: (0, g, 0))
    bwd_blk = pl.BlockSpec(blk, lambda g: (0, n - 1 - g, 0))
    state_spec = pl.BlockSpec((2, SUBLANES, STATE_W), lambda g: (0, 0, 0))
    in_specs = [
        fwd_blk, bwd_blk, state_spec,
        pl.BlockSpec((rows_sc, rows_in), const2),
        pl.BlockSpec((rows_sc, rows_in), const2),
        pl.BlockSpec((rows_in, rows_sc), const2),
        pl.BlockSpec((rows_in, rows_sc), const2),
        pl.BlockSpec((None, 2, SUBLANES, STATE_W), lambda g: (layer, 0, 0, 0)),
        pl.BlockSpec((None, S5_NK, 2 * LANES, 2 * S5_KSTATE), lambda g: (layer, 0, 0, 0), pipeline_mode=once),
        pl.BlockSpec((None, S5_NK, 2 * S5_KSTATE, 2 * LANES), lambda g: (layer, 0, 0, 0), pipeline_mode=once),
    ]
    state_shape = jax.ShapeDtypeStruct((2, SUBLANES, STATE_W), F32)
    scratch = [pltpu.VMEM((rows_sc, 2 * STATE_W), F32), pltpu.VMEM((2, SUBLANES, STATE_W), F32)]
    if need_y:
        out_specs = [fwd_blk, bwd_blk, state_spec]
        y_shape = jax.ShapeDtypeStruct((BATCH, steps, SSM_WIDTH), BF16)
        out_shape = [y_shape, y_shape, state_shape]
        scratch += [pltpu.VMEM((rows_sc, 2 * STATE_W), BF16), pltpu.VMEM((rows_sc, SSM_WIDTH), F32),
                    pltpu.VMEM((rows_sc, SSM_WIDTH), F32)]
    else:
        out_specs = [state_spec]
        out_shape = [state_shape]
    return pl.pallas_call(
        functools.partial(_s5_kernel, need_y),
        grid=(n,),
        in_specs=in_specs,
        out_specs=out_specs,
        out_shape=out_shape,
        scratch_shapes=scratch,
        compiler_params=_params(("arbitrary",), 52),
        name="s5_scan",
    )(u, u, h0, pf, pb, pf.T, pb.T, a, wd, wr)


def _s5_params(lam_re, lam_im, log_step, b_re, b_im, c_re, c_im):
    dt = jnp.exp(log_step)[..., None]
    mag = jnp.exp(lam_re * dt)
    ang = lam_im * dt
    abar_re, abar_im = mag * jnp.cos(ang), mag * jnp.sin(ang)
    den = lam_re * lam_re + lam_im * lam_im
    nr, ni = abar_re - 1.0, abar_im
    coef_re = (nr * lam_re + ni * lam_im) / den
    coef_im = (ni * lam_re - nr * lam_im) / den
    bb_re = coef_re[..., None] * b_re - coef_im[..., None] * b_im
    bb_im = coef_re[..., None] * b_im + coef_im[..., None] * b_re
    bb = jnp.stack([bb_re, bb_im])
    bb = jnp.swapaxes(bb, -1, -2).reshape(-1, SSM_STATE)
    own = ((jnp.arange(bb.shape[0]) // SSM_GROUP) % S5_KGROUPS)[:, None] == (
        jnp.arange(S5_KSTATE) // SSM_STATE)[None, :]
    bb = jnp.where(own, jnp.tile(bb, (1, S5_KGROUPS)), 0.0)
    bb = bb.reshape(2, 2, S5_NK, LANES, S5_KSTATE)
    wd = jnp.concatenate([bb[0], bb[1]], axis=-1)
    wd = jnp.transpose(wd, (1, 0, 2, 3)).reshape(S5_NK, 2 * LANES, 2 * S5_KSTATE).astype(BF16)
    cc = jnp.stack([c_re, -c_im])
    cc = jnp.swapaxes(cc, -1, -2).reshape(-1, SSM_GROUP)
    own = ((jnp.arange(cc.shape[0]) // SSM_STATE) % S5_KGROUPS)[:, None] == (
        jnp.arange(LANES) // SSM_GROUP)[None, :]
    cc = jnp.where(own, jnp.tile(cc, (1, S5_KGROUPS)), 0.0)
    cc = cc.reshape(2, 2, S5_NK, S5_KSTATE, LANES)
    wr = jnp.concatenate([cc[:, 0], cc[:, 1]], axis=-1)
    wr = jnp.transpose(wr, (1, 0, 2, 3)).reshape(S5_NK, 2 * S5_KSTATE, 2 * LANES).astype(BF16)
    a = jnp.stack([abar_re.reshape(2, STATE_W), abar_im.reshape(2, STATE_W)])
    a = jnp.repeat(a, BATCH, axis=1)
    return a, wd, wr


def _dft_tables(n):
    j = np.arange(n, dtype=np.int64)
    ang = 2.0 * np.pi * ((j[:, None] * j[None, :]) % n).astype(np.float64) / n
    s = 1.0 / math.sqrt(n)
    return (np.cos(ang) * s).astype(np.float32), (np.sin(ang) * s).astype(np.float32)


_CH_COS, _CH_SIN = _dft_tables(FFT_GROUP)
_CH_CS = np.concatenate([_CH_COS, _CH_SIN], axis=1)


def _chdft_kernel(u_ref, w_ref, p_ref, q_ref):
    w = w_ref[...].astype(BF16)
    for g in range(FFT_GROUPS):
        cs = slice(g * FFT_GROUP, (g + 1) * FFT_GROUP)
        r = _dot(u_ref[:, cs], w)
        p_ref[:, cs] = r[:, :FFT_GROUP].astype(BF16)
        q_ref[:, cs] = r[:, FFT_GROUP:].astype(BF16)


def _channel_dft(rest):
    nb, rows, _ = rest.shape
    tm = min(rows, 2048)
    shape = jax.ShapeDtypeStruct((nb, rows, FFT_WIDTH), BF16)
    blk = pl.BlockSpec((None, tm, FFT_WIDTH), lambda b, i: (b, i, 0))
    return pl.pallas_call(
        _chdft_kernel,
        grid=(nb, rows // tm),
        in_specs=[blk, pl.BlockSpec((FFT_GROUP, 2 * FFT_GROUP), lambda b, i: (0, 0))],
        out_specs=[blk, blk],
        out_shape=[shape, shape],
        compiler_params=_params(("parallel", "parallel"), 40),
        name="channel_dft",
    )(rest, jnp.asarray(_CH_CS))


DFT_RADIX = 4
DFT_ILV = 256
_QUARTER_TURN = ((1, 0), (0, 1), (-1, 0), (0, -1))


def _radix_tables(n):
    nq = n // DFT_RADIX
    k = np.arange(nq, dtype=np.int64)
    s = 1.0 / math.sqrt(n)
    cos, sin = [], []
    for r in range(DFT_RADIX):
        j = DFT_RADIX * k + r
        ang = 2.0 * np.pi * ((j[:, None] * k[None, :]) % n).astype(np.float64) / n
        cos.append(np.cos(ang) * s)
        sin.append(np.sin(ang) * s)
    return np.stack(cos).astype(np.float32), np.stack(sin).astype(np.float32)


def _interleave_perm():
    per = DFT_ILV // DFT_RADIX
    perm = np.zeros((DFT_ILV, DFT_ILV), np.float32)
    for r in range(DFT_RADIX):
        for m in range(per):
            perm[DFT_RADIX * m + r, r * per + m] = 1.0
    return perm


_RADIX_TABLES = {n: _radix_tables(n) for n in (CTX_LEN, SEQ)}
_ILV_PERM = _interleave_perm()


def _posdft_kernel(n, p_ref, q_ref, c_ref, s_ref, perm_ref, o_ref, comb_ref, y_ref):
    nq = n // DFT_RADIX
    chunk = min(nq, EPILOGUE_ROWS)

    def combine(i, carry):
        rs = pl.ds(pl.multiple_of(i * chunk, chunk), chunk)
        p = [p_ref[pl.ds(pl.multiple_of(q * nq + i * chunk, chunk), chunk), :].astype(F32)
             for q in range(DFT_RADIX)]
        qq = [q_ref[pl.ds(pl.multiple_of(q * nq + i * chunk, chunk), chunk), :].astype(F32)
              for q in range(DFT_RADIX)]
        def signed_sum(terms):
            acc = None
            for sign, v in terms:
                if acc is None:
                    acc = v if sign > 0 else -v
                else:
                    acc = acc + v if sign > 0 else acc - v
            return acc

        for r in range(DFT_RADIX):
            pr, qr = [], []
            for q in range(DFT_RADIX):
                cs, sn = _QUARTER_TURN[(r * q) % DFT_RADIX]
                if cs:
                    pr.append((cs, p[q]))
                    qr.append((cs, qq[q]))
                if sn:
                    pr.append((-sn, qq[q]))
                    qr.append((sn, p[q]))
            comb_ref[2 * r, rs, :] = signed_sum(pr).astype(BF16)
            comb_ref[2 * r + 1, rs, :] = signed_sum(qr).astype(BF16)
        return carry

    lax.fori_loop(0, nq // chunk, combine, 0)
    for r in range(DFT_RADIX):
        y_ref[r] = (_dot(c_ref[r].astype(BF16), comb_ref[2 * r])
                    - _dot(s_ref[r].astype(BF16), comb_ref[2 * r + 1]))
    per = DFT_ILV // DFT_RADIX
    for blk in range(n // DFT_ILV):
        slab = jnp.concatenate([y_ref[r, blk * per:(blk + 1) * per, :] for r in range(DFT_RADIX)], axis=0)
        o_ref[blk * DFT_ILV:(blk + 1) * DFT_ILV, :] = _dot(perm_ref[...], slab.astype(BF16)).astype(BF16)


def _position_dft(p, q):
    nb, steps, _ = p.shape
    cos, sin = _RADIX_TABLES[steps]
    nq = steps // DFT_RADIX
    x_blk = pl.BlockSpec((None, steps, FFT_WIDTH), lambda b: (b, 0, 0))
    t_blk = pl.BlockSpec((DFT_RADIX, nq, nq), lambda b: (0, 0, 0), pipeline_mode=pl.Buffered(1))
    return pl.pallas_call(
        functools.partial(_posdft_kernel, steps),
        grid=(nb,),
        in_specs=[x_blk, x_blk, t_blk, t_blk, pl.BlockSpec((DFT_ILV, DFT_ILV), lambda b: (0, 0))],
        out_specs=x_blk,
        out_shape=jax.ShapeDtypeStruct((nb, steps, FFT_WIDTH), BF16),
        scratch_shapes=[pltpu.VMEM((2 * DFT_RADIX, nq, FFT_WIDTH), BF16),
                        pltpu.VMEM((DFT_RADIX, nq, FFT_WIDTH), F32)],
        compiler_params=_params(("parallel",), 56),
        name="position_dft",
    )(p, q, jnp.asarray(cos), jnp.asarray(sin), jnp.asarray(_ILV_PERM, BF16))


def _glu_kernel(us_ref, yf_ref, yb_ref, dsk_ref, w_ref, o_ref):
    ys = dsk_ref[...] * us_ref[...] + yf_ref[...].astype(F32) + yb_ref[...].astype(F32)
    g = jax.nn.gelu(ys)
    z = _dot(g.astype(BF16), w_ref[...].astype(BF16))
    o_ref[...] = (g * jax.nn.sigmoid(z)).astype(BF16)


def _glu(us, yf, yb, d_skip, w_glu, layer):
    nb, rows, _ = us.shape
    tm = 1024
    blk = pl.BlockSpec((None, tm, SSM_WIDTH), lambda b, i: (b, i, 0))
    return pl.pallas_call(
        _glu_kernel,
        grid=(nb, rows // tm),
        in_specs=[blk, blk, blk,
                  pl.BlockSpec((None, 1, SSM_WIDTH), lambda b, i: (layer, 0, 0)),
                  pl.BlockSpec((None, SSM_WIDTH, SSM_WIDTH), lambda b, i: (layer, 0, 0))],
        out_specs=blk,
        out_shape=jax.ShapeDtypeStruct((nb, rows, SSM_WIDTH), BF16),
        compiler_params=_params(("parallel", "parallel"), 40),
        name="s5_glu",
    )(us, yf, yb, d_skip.reshape(DEPTH, 1, SSM_WIDTH), w_glu)


MERGE_SPLIT = 4


def _merge_kernel(s_ref, yq_ref, gs_ref, gf_ref, wps_ref, wpf_ref, o_ref):
    wps = wps_ref[...].astype(BF16)
    wpf = wpf_ref[...].astype(BF16)
    piece = o_ref.shape[0] // MERGE_SPLIT
    for r in range(MERGE_SPLIT):
        rs = slice(r * piece, (r + 1) * piece)
        ps = _dot(s_ref[rs, :], wps)
        pf = _dot(yq_ref[rs, :], wpf)
        o_ref[rs, :] = (jax.nn.sigmoid(gs_ref[rs, :].astype(F32)) * ps
                        + jax.nn.sigmoid(gf_ref[rs, :].astype(F32)) * pf).astype(BF16)


def _merge(s, yq, rest, w_ps, w_pf, layer):
    nb, rows, _ = s.shape
    tm = min(rows, 2048)
    tn = 512
    gs_off = FFT_WIDTH // tn
    gf_off = (FFT_WIDTH + D_MODEL) // tn
    row_blk = pl.BlockSpec((None, tm, SSM_WIDTH), lambda b, i, j: (b, i, 0))
    w_blk = pl.BlockSpec((None, SSM_WIDTH, tn), lambda b, i, j: (layer, 0, j))
    return pl.pallas_call(
        _merge_kernel,
        grid=(nb, rows // tm, D_MODEL // tn),
        in_specs=[
            row_blk, row_blk,
            pl.BlockSpec((None, tm, tn), lambda b, i, j: (b, i, gs_off + j)),
            pl.BlockSpec((None, tm, tn), lambda b, i, j: (b, i, gf_off + j)),
            w_blk, w_blk,
        ],
        out_specs=pl.BlockSpec((None, tm, tn), lambda b, i, j: (b, i, j)),
        out_shape=jax.ShapeDtypeStruct((nb, rows, D_MODEL), BF16),
        compiler_params=_params(("parallel", "parallel", "arbitrary"), 48),
        name="gated_merge",
    )(s, yq, rest, rest, w_ps, w_pf)


TAIL_TM = 1024
TAIL_TK = 512
TAIL_TF = 512


def _tail_kernel(layer, emit_next, merged_ref, x_hbm, wo_hbm, wu_hbm, wd_hbm, mod_ref, *refs):
    if emit_next:
        modn_ref, refs = refs[0], refs[1:]
    g1_ref, b1_ref, g2_ref, b2_ref, out_hbm = refs[:5]
    refs = refs[5:]
    if emit_next:
        hn_ref, refs = refs[0], refs[1:]
    xbuf, acc, h2, act, wa, wb, sem_a, sem_b, sem_x, sem_o = refs
    b = pl.program_id(0)
    i = pl.program_id(1)
    first = jnp.logical_and(b == 0, i == 0)
    last = jnp.logical_and(b == pl.num_programs(0) - 1, i == pl.num_programs(1) - 1)
    nk = D_MODEL // TAIL_TK
    nf = D_FF // TAIL_TF
    assert nk % 2 == 0
    rows = pl.ds(pl.multiple_of(i * TAIL_TM, TAIL_TM), TAIL_TM)

    def wo_copy(t, slot):
        return pltpu.make_async_copy(wo_hbm.at[layer, pl.ds(t * TAIL_TK, TAIL_TK), :], wa.at[slot], sem_a.at[slot])

    def wd_copy(f, slot):
        src = wd_hbm.at[layer, pl.ds(pl.multiple_of(f * TAIL_TK, TAIL_TK), TAIL_TK), :]
        return pltpu.make_async_copy(src, wa.at[slot], sem_a.at[slot])

    def wu_copy(f, slot):
        src = wu_hbm.at[layer, :, pl.ds(pl.multiple_of(f * TAIL_TF, TAIL_TF), TAIL_TF)]
        return pltpu.make_async_copy(src, wb.at[slot], sem_b.at[slot])

    def x_copy():
        return pltpu.make_async_copy(x_hbm.at[b, rows, :], xbuf, sem_x.at[0])

    def out_copy():
        return pltpu.make_async_copy(xbuf, out_hbm.at[b, rows, :], sem_o.at[0])

    wo_copy(0, 0).start()
    wo_copy(1, 1).start()
    wu_copy(0, 0).start()
    wu_copy(1, 1).start()

    for k in range(nk):
        slot = k % 2
        wo_copy(k, slot).wait()
        lhs = merged_ref[:, k * TAIL_TK:(k + 1) * TAIL_TK]
        if k == 0:
            acc[...] = _dot(lhs, wa[slot].astype(BF16))
        else:
            acc[...] += _dot(lhs, wa[slot].astype(BF16))
        if k + 2 < nk:
            wo_copy(k + 2, slot).start()
        else:
            wd_copy(k + 2 - nk, slot).start()
        if k == 0:
            @pl.when(jnp.logical_not(first))
            def _():
                out_copy().wait()

            x_copy().start()
    x_copy().wait()

    def finish1(rs):
        y = ALPHA * xbuf[rs, :] + _row(mod_ref, GATE1) * acc[rs, :]
        x1 = _ln(y) * g1_ref[...] + b1_ref[...]
        xbuf[rs, :] = x1
        h2[rs, :] = _modulated_ln(x1, mod_ref, SHIFT2, SCALE2)
        acc[rs, :] = jnp.zeros((EPILOGUE_ROWS, D_MODEL), F32)

    _for_row_chunks(TAIL_TM, finish1)

    def up_tile(slot):
        a = jnp.maximum(_dot(h2[...], wb[slot].astype(BF16)), 0.0)
        act[slot] = (a * a).astype(BF16)

    def down_tile(slot):
        acc[...] += _dot(act[slot], wa[slot].astype(BF16))

    def trip(f, slot):
        nxt = 1 - slot
        wu_copy(f + 1, nxt).wait()
        wd_copy(f, slot).wait()
        up_tile(nxt)
        down_tile(slot)
        if isinstance(f, int):
            if f + 3 < nf:
                wu_copy(f + 3, nxt).start()
            if f + 2 < nf:
                wd_copy(f + 2, slot).start()
        else:
            @pl.when(f + 3 < nf)
            def _():
                wu_copy(f + 3, nxt).start()

            @pl.when(f + 2 < nf)
            def _():
                wd_copy(f + 2, slot).start()

    wu_copy(0, 0).wait()
    up_tile(0)
    wu_copy(2, 0).start()

    def trip_pair(p, carry):
        trip(2 * p, 0)
        trip(2 * p + 1, 1)
        return carry

    lax.fori_loop(0, (nf - 1) // 2, trip_pair, 0)
    for f in range(2 * ((nf - 1) // 2), nf - 1):
        trip(f, f % 2)
    wd_copy(nf - 1, (nf - 1) % 2).wait()
    down_tile((nf - 1) % 2)

    def finish2(rs):
        y = ALPHA * xbuf[rs, :] + _row(mod_ref, GATE2) * acc[rs, :]
        x2 = _ln(y) * g2_ref[...] + b2_ref[...]
        xbuf[rs, :] = x2
        if emit_next:
            hn_ref[rs, :] = _modulated_ln(x2, modn_ref, SHIFT1, SCALE1)

    _for_row_chunks(TAIL_TM, finish2)
    out_copy().start()

    @pl.when(last)
    def _():
        out_copy().wait()


def _tail(merged, x, mod, w_o, w_up, w_down, g1, b1, g2, b2, layer, emit_next):
    nb, rows, _ = x.shape
    tm = TAIL_TM
    row_blk = pl.BlockSpec((None, tm, D_MODEL), lambda b_, i: (b_, i, 0))
    vec = pl.BlockSpec((None, 1, D_MODEL), lambda b_, i: (layer, 0, 0))
    hbm = pl.BlockSpec(memory_space=pl.ANY)
    in_specs = [row_blk, hbm, hbm, hbm, hbm, _mod_spec(layer, nb)]
    args = [merged, x, w_o, w_up, w_down, mod]
    if emit_next:
        in_specs.append(_mod_spec(layer + 1, nb))
        args.append(mod)
    in_specs += [vec, vec, vec, vec]
    args += [v.reshape(DEPTH, 1, D_MODEL) for v in (g1, b1, g2, b2)]
    out_specs = [hbm]
    out_shape = [jax.ShapeDtypeStruct(x.shape, F32)]
    if emit_next:
        out_specs.append(row_blk)
        out_shape.append(jax.ShapeDtypeStruct(x.shape, BF16))
    out = pl.pallas_call(
        functools.partial(_tail_kernel, layer, emit_next),
        grid=(nb, rows // tm),
        in_specs=in_specs,
        out_specs=out_specs,
        out_shape=out_shape,
        scratch_shapes=[
            pltpu.VMEM((tm, D_MODEL), F32),
            pltpu.VMEM((tm, D_MODEL), F32),
            pltpu.VMEM((tm, D_MODEL), BF16),
            pltpu.VMEM((2, tm, TAIL_TF), BF16),
            pltpu.VMEM((2, TAIL_TK, D_MODEL), F32),
            pltpu.VMEM((2, D_MODEL, TAIL_TF), F32),
            pltpu.SemaphoreType.DMA((2,)),
            pltpu.SemaphoreType.DMA((2,)),
            pltpu.SemaphoreType.DMA((1,)),
            pltpu.SemaphoreType.DMA((1,)),
        ],
        compiler_params=_params(("arbitrary", "arbitrary"), 58),
        name="layer_tail",
    )(*args)
    return out if emit_next else (out[0], None)


def _pos_table():
    quarter = D_MODEL // 4
    omega = 1.0 / (POS_BASE ** (np.arange(quarter, dtype=np.float64) / quarter))
    t = np.arange(SEQ)
    ar = (t // GRID_W).astype(np.float64)[:, None] * omega
    ac = (t % GRID_W).astype(np.float64)[:, None] * omega
    return np.concatenate([np.sin(ar), np.cos(ar), np.sin(ac), np.cos(ac)], axis=-1).astype(np.float32)


_POS = _pos_table()


def _mod_tables(m):
    m = m.reshape(DEPTH, SUBLANES, 6, D_MODEL)
    m = jnp.pad(m, ((0, 0), (0, 0), (0, SUBLANES - 6), (0, 0)))
    return m[:, :BATCH], m[:, BATCH:BATCH + 1]


def _as_batch(a):
    return a.reshape(BATCH, CTX_LEN, a.shape[-1])


def _as_slab(a):
    return a.reshape(1, BATCH * CTX_LEN, a.shape[-1])


def kernel(x, c, ctx, c_ctx, w_mod, b_mod, w_in, lam_re, lam_im, log_step, ssm_b_re, ssm_b_im, ssm_c_re,
           ssm_c_im, d_skip, w_glu, w_ps, w_pf, w_o, ln1_g, ln1_b, w_up, w_down, ln2_g, ln2_b):
    cond8 = jnp.concatenate([c, c_ctx[None], jnp.zeros((SUBLANES - BATCH - 1, D_MODEL), F32)], axis=0)
    mod_lat, mod_ctx = _mod_tables(_modulation(cond8, w_mod, b_mod))

    x_lat, h_lat = _entry(x, jnp.asarray(_POS), mod_lat)
    x_ctx, h_ctx = _entry(_as_slab(ctx), None, mod_ctx)
    h_zero = jnp.zeros((2, SUBLANES, STATE_W), F32)
    prm = jax.vmap(_s5_params)(lam_re, lam_im, log_step, ssm_b_re, ssm_b_im, ssm_c_re, ssm_c_im)

    for l in range(DEPTH):
        need_ctx = l < DEPTH - 1

        def mixer_tail(us, rest, yf, yb, xx, mod, as_batch, as_rows):
            p, q = _channel_dft(rest)
            yq = as_rows(_position_dft(as_batch(p), as_batch(q)))
            s = _glu(us, as_rows(yf), as_rows(yb), d_skip, w_glu, l)
            merged = _merge(s, yq, rest, w_ps, w_pf, l)
            return _tail(merged, xx, mod, w_o, w_up, w_down, ln1_g, ln1_b, ln2_g, ln2_b, l, need_ctx)

        us_lat, rest_lat = _in_proj(h_lat, w_in, l, True)
        if need_ctx:
            us_ctx, rest_ctx = _in_proj(h_ctx, w_in, l, True)
            yf_c, yb_c, h_t = _s5(_as_batch(us_ctx), h_zero, prm, l, True)
        else:
            (us_ctx,) = _in_proj(h_ctx, w_in, l, False)
            (h_t,) = _s5(_as_batch(us_ctx), h_zero, prm, l, False)
        yf, yb, _ = _s5(us_lat, h_t, prm, l, True)
        ident = lambda a: a
        x_lat, h_lat = mixer_tail(us_lat, rest_lat, yf, yb, x_lat, mod_lat, ident, ident)
        if need_ctx:
            x_ctx, h_ctx = mixer_tail(us_ctx, rest_ctx, yf_c, yb_c, x_ctx, mod_ctx, _as_batch, _as_slab)

    return x_lat
```

```python
import functools
import math

import numpy as np
import jax
import jax.numpy as jnp
from jax import lax
from jax.experimental import pallas as pl
from jax.experimental.pallas import tpu as pltpu

D_MODEL = 2048
BATCH = 4
SEQ = 2048
DEPTH = 2
GRID_W = 64
CTX_LEN = 256
SSM_WIDTH = D_MODEL // 2
SSM_GROUP = 16
SSM_GROUPS = SSM_WIDTH // SSM_GROUP
SSM_STATE = 64
FFT_WIDTH = D_MODEL - SSM_WIDTH
FFT_GROUPS = 4
FFT_GROUP = FFT_WIDTH // FFT_GROUPS
IN_WIDTH = SSM_WIDTH + FFT_WIDTH + 2 * D_MODEL
REST_WIDTH = IN_WIDTH - SSM_WIDTH
D_FF = 4 * D_MODEL
ALPHA = (2 * DEPTH) ** 0.25
LN_EPS = 1e-5
POS_BASE = 10000.0

F32 = jnp.float32
BF16 = jnp.bfloat16

SUBLANES = 8
LANES = 128
STATE_W = SSM_GROUPS * SSM_STATE
SCAN_T = 64
S5_KGROUPS = 8
S5_NK = SSM_GROUPS // S5_KGROUPS
S5_KSTATE = S5_KGROUPS * SSM_STATE
MIB = 1024 * 1024

SHIFT1, SCALE1, GATE1, SHIFT2, SCALE2, GATE2 = range(6)


def _params(sem, vmem_mib):
    return pltpu.CompilerParams(dimension_semantics=sem, vmem_limit_bytes=vmem_mib * MIB)


def _dot(a, b):
    return jnp.dot(a, b, preferred_element_type=F32)


def _ln(x, eps=LN_EPS):
    mu = jnp.mean(x, axis=-1, keepdims=True)
    xc = x - mu
    var = jnp.mean(xc * xc, axis=-1, keepdims=True)
    return xc * lax.rsqrt(var + eps)


def _deepnorm_ln(x, gate, branch):
    return _ln(x + (gate * (1.0 / ALPHA)) * branch, LN_EPS / (ALPHA * ALPHA))


def _row(ref, j):
    return ref[j:j + 1, :]


def _modulated_ln(x, mod_ref, shift, scale):
    return (_ln(x) * (1.0 + _row(mod_ref, scale)) + _row(mod_ref, shift)).astype(BF16)


EPILOGUE_ROWS = 128


def _for_row_chunks(rows, fn):
    def body(r, carry):
        fn(pl.ds(pl.multiple_of(r * EPILOGUE_ROWS, EPILOGUE_ROWS), EPILOGUE_ROWS))
        return carry

    lax.fori_loop(0, rows // EPILOGUE_ROWS, body, 0)


def _mod_spec(layer, nb):
    if nb == 1:
        return pl.BlockSpec((None, None, SUBLANES, D_MODEL), lambda b, *_: (layer, 0, 0, 0))
    return pl.BlockSpec((None, None, SUBLANES, D_MODEL), lambda b, *_: (layer, b, 0, 0))


def _mod_kernel(c_ref, w_ref, b_ref, o_ref):
    c = c_ref[...]
    sc = c * jax.nn.sigmoid(c)
    o_ref[...] = _dot(sc.astype(BF16), w_ref[...].astype(BF16)) + b_ref[...]


def _modulation(cond8, w_mod, b_mod):
    tn = 1024
    n = 6 * D_MODEL
    return pl.pallas_call(
        _mod_kernel,
        grid=(DEPTH, n // tn),
        in_specs=[
            pl.BlockSpec((SUBLANES, D_MODEL), lambda l, j: (0, 0)),
            pl.BlockSpec((None, D_MODEL, tn), lambda l, j: (l, 0, j)),
            pl.BlockSpec((None, 1, tn), lambda l, j: (l, 0, j)),
        ],
        out_specs=pl.BlockSpec((None, SUBLANES, tn), lambda l, j: (l, 0, j)),
        out_shape=jax.ShapeDtypeStruct((DEPTH, SUBLANES, n), F32),
        compiler_params=_params(("parallel", "parallel"), 40),
        name="adaln_modulation",
    )(cond8, w_mod, b_mod.reshape(DEPTH, 1, n))


def _entry_kernel(has_pos, *refs):
    if has_pos:
        x_ref, p_ref, mod_ref, xo_ref, h_ref = refs
        x = x_ref[...] + p_ref[...]
        xo_ref[...] = x
    else:
        x_ref, mod_ref, h_ref = refs
        x = x_ref[...]
    h_ref[...] = _modulated_ln(x, mod_ref, SHIFT1, SCALE1)


def _entry(x, pos, mod):
    nb, rows, _ = x.shape
    tm = 512
    has_pos = pos is not None
    blk = pl.BlockSpec((None, tm, D_MODEL), lambda i, b: (b, i, 0))
    in_specs = [blk]
    args = [x]
    if has_pos:
        in_specs.append(pl.BlockSpec((tm, D_MODEL), lambda i, b: (i, 0)))
        args.append(pos)
    in_specs.append(pl.BlockSpec((None, None, SUBLANES, D_MODEL), lambda i, b: (0, b, 0, 0)))
    args.append(mod)
    h_shape = jax.ShapeDtypeStruct(x.shape, BF16)
    out = pl.pallas_call(
        functools.partial(_entry_kernel, has_pos),
        grid=(rows // tm, nb),
        in_specs=in_specs,
        out_specs=[blk, blk] if has_pos else [blk],
        out_shape=[jax.ShapeDtypeStruct(x.shape, F32), h_shape] if has_pos else [h_shape],
        compiler_params=_params(("parallel", "parallel"), 40),
        name="entry_ln",
    )(*args)
    return out if has_pos else (x, out[0])


ROW_SPLIT = 2


def _win_kernel(n_us, with_rest, h_ref, w_ref, us_ref, *rest):
    piece = h_ref.shape[0] // ROW_SPLIT

    def project(o_ref):
        w = w_ref[...].astype(BF16)
        for r in range(ROW_SPLIT):
            rs = slice(r * piece, (r + 1) * piece)
            o_ref[rs, :] = _dot(h_ref[rs, :], w).astype(o_ref.dtype)

    if not with_rest:
        project(us_ref)
        return
    rest_ref, = rest
    j = pl.program_id(2)

    @pl.when(j < n_us)
    def _():
        project(us_ref)

    @pl.when(j >= n_us)
    def _():
        project(rest_ref)


def _in_proj(h, w_in, layer, with_rest):
    nb, rows, _ = h.shape
    tm = min(rows, 2048)
    tn = 512
    n_us = SSM_WIDTH // tn
    n_tiles = (IN_WIDTH if with_rest else SSM_WIDTH) // tn
    out_specs = [pl.BlockSpec((None, tm, tn), lambda b, i, j: (b, i, jnp.minimum(j, n_us - 1)))]
    out_shape = [jax.ShapeDtypeStruct((nb, rows, SSM_WIDTH), F32)]
    if with_rest:
        out_specs.append(pl.BlockSpec((None, tm, tn), lambda b, i, j: (b, i, jnp.maximum(j - n_us, 0))))
        out_shape.append(jax.ShapeDtypeStruct((nb, rows, REST_WIDTH), BF16))
    return pl.pallas_call(
        functools.partial(_win_kernel, n_us, with_rest),
        grid=(nb, rows // tm, n_tiles),
        in_specs=[
            pl.BlockSpec((None, tm, D_MODEL), lambda b, i, j: (b, i, 0)),
            pl.BlockSpec((None, D_MODEL, tn), lambda b, i, j: (layer, 0, j)),
        ],
        out_specs=out_specs,
        out_shape=out_shape,
        compiler_params=_params(("parallel", "parallel", "arbitrary"), 48),
        name="in_proj",
    )(h, w_in)


def _s5_kernel(need_y, uf_ref, ub_ref, h0_ref, pf_ref, pb_ref, pft_ref, pbt_ref, a_ref, wd_ref, wr_ref,
               *rest):
    if need_y:
        yf_ref, yb_ref, ht_ref, v_ref, hc_ref, h_ref, ycf_ref, ycb_ref = rest
    else:
        ht_ref, v_ref, hc_ref = rest
    g = pl.program_id(0)
    rows_in = BATCH * SCAN_T
    pair = 2 * SUBLANES

    @pl.when(g == 0)
    def _():
        hc_ref[...] = h0_ref[...]

    uf = uf_ref[...].reshape(rows_in, SSM_WIDTH).astype(BF16)
    ub = ub_ref[...].reshape(rows_in, SSM_WIDTH).astype(BF16)
    up_f = _dot(pf_ref[...], uf).astype(BF16)
    up_b = _dot(pb_ref[...], ub).astype(BF16)

    def slab_cols(k):
        return slice(2 * k * S5_KSTATE, (2 * k + 2) * S5_KSTATE)

    def drive(k):
        cs = slice(k * LANES, (k + 1) * LANES)
        lhs = jnp.concatenate([up_f[:, cs], up_b[:, cs]], axis=1)
        v_ref[:, slab_cols(k)] = _dot(lhs, wd_ref[k])

    def scan(k):
        ss = slice(k * S5_KSTATE, (k + 1) * S5_KSTATE)
        re_cols = slice(2 * k * S5_KSTATE, (2 * k + 1) * S5_KSTATE)
        im_cols = slice((2 * k + 1) * S5_KSTATE, (2 * k + 2) * S5_KSTATE)
        ar = a_ref[0, :, ss]
        ai = a_ref[1, :, ss]
        hr = hc_ref[0, :, ss]
        hi = hc_ref[1, :, ss]
        for s2 in range(SCAN_T // 2):
            rows = []
            for s in (2 * s2, 2 * s2 + 1):
                rs = slice(s * SUBLANES, (s + 1) * SUBLANES)
                nr = ar * hr - ai * hi + v_ref[rs, re_cols]
                ni = ar * hi + ai * hr + v_ref[rs, im_cols]
                hr, hi = nr, ni
                rows.append((nr, ni))
            if need_y:
                ps = slice(s2 * pair, (s2 + 1) * pair)
                h_ref[ps, re_cols] = jnp.concatenate([rows[0][0], rows[1][0]], axis=0).astype(BF16)
                h_ref[ps, im_cols] = jnp.concatenate([rows[0][1], rows[1][1]], axis=0).astype(BF16)
        hc_ref[0, :, ss] = hr
        hc_ref[1, :, ss] = hi

    def readout(k):
        cs = slice(k * LANES, (k + 1) * LANES)
        y = _dot(h_ref[:, slab_cols(k)], wr_ref[k])
        ycf_ref[:, cs] = y[:, :LANES]
        ycb_ref[:, cs] = y[:, LANES:]

    lag = 1
    for k in range(S5_NK + lag):
        if k < S5_NK:
            drive(k)
            scan(k)
        if need_y and k >= lag:
            readout(k - lag)
    if need_y:
        yf = _dot(pft_ref[...], ycf_ref[...].astype(BF16))
        yb = _dot(pbt_ref[...], ycb_ref[...].astype(BF16))
        yf_ref[...] = yf.astype(BF16).reshape(BATCH, SCAN_T, SSM_WIDTH)
        yb_ref[...] = yb.astype(BF16).reshape(BATCH, SCAN_T, SSM_WIDTH)

    @pl.when(g == pl.num_programs(0) - 1)
    def _():
        ht_ref[...] = hc_ref[...]


def _scan_perms():
    t = SCAN_T
    pf = np.zeros((2 * BATCH * t, BATCH * t), np.float32)
    pb = np.zeros((2 * BATCH * t, BATCH * t), np.float32)
    for s in range(t):
        for b in range(BATCH):
            pf[s * 2 * BATCH + b, b * t + s] = 1.0
            pb[s * 2 * BATCH + BATCH + b, b * t + (t - 1 - s)] = 1.0
    return pf, pb


_PF, _PB = _scan_perms()


def _s5(u, h0, prm, layer, need_y):
    a, wd, wr = prm
    steps = u.shape[1]
    n = steps // SCAN_T
    rows_in = BATCH * SCAN_T
    rows_sc = 2 * rows_in
    pf = jnp.asarray(_PF, BF16)
    pb = jnp.asarray(_PB, BF16)
    const2 = lambda g: (0, 0)
    once = pl.Buffered(1)
    blk = (BATCH, SCAN_T, SSM_WIDTH)
    fwd_blk = pl.BlockSpec(blk, lambda g: (0, g, 0))
    bwd_blk = pl.BlockSpec(blk, lambda g: (0, n - 1 - g, 0))
    state_spec = pl.BlockSpec((2, SUBLANES, STATE_W), lambda g: (0, 0, 0))
    in_specs = [
        fwd_blk, bwd_blk, state_spec,
        pl.BlockSpec((rows_sc, rows_in), const2),
        pl.BlockSpec((rows_sc, rows_in), const2),
        pl.BlockSpec((rows_in, rows_sc), const2),
        pl.BlockSpec((rows_in, rows_sc), const2),
        pl.BlockSpec((None, 2, SUBLANES, STATE_W), lambda g: (layer, 0, 0, 0)),
        pl.BlockSpec((None, S5_NK, 2 * LANES, 2 * S5_KSTATE), lambda g: (layer, 0, 0, 0), pipeline_mode=once),
        pl.BlockSpec((None, S5_NK, 2 * S5_KSTATE, 2 * LANES), lambda g: (layer, 0, 0, 0), pipeline_mode=once),
    ]
    state_shape = jax.ShapeDtypeStruct((2, SUBLANES, STATE_W), F32)
    scratch = [pltpu.VMEM((rows_sc, 2 * STATE_W), F32), pltpu.VMEM((2, SUBLANES, STATE_W), F32)]
    if need_y:
        out_specs = [fwd_blk, bwd_blk, state_spec]
        y_shape = jax.ShapeDtypeStruct((BATCH, steps, SSM_WIDTH), BF16)
        out_shape = [y_shape, y_shape, state_shape]
        scratch += [pltpu.VMEM((rows_sc, 2 * STATE_W), BF16), pltpu.VMEM((rows_sc, SSM_WIDTH), F32),
                    pltpu.VMEM((rows_sc, SSM_WIDTH), F32)]
    else:
        out_specs = [state_spec]
        out_shape = [state_shape]
    return pl.pallas_call(
        functools.partial(_s5_kernel, need_y),
        grid=(n,),
        in_specs=in_specs,
        out_specs=out_specs,
        out_shape=out_shape,
        scratch_shapes=scratch,
        compiler_params=_params(("arbitrary",), 52),
        name="s5_scan",
    )(u, u, h0, pf, pb, pf.T, pb.T, a, wd, wr)


def _s5_params(lam_re, lam_im, log_step, b_re, b_im, c_re, c_im):
    dt = jnp.exp(log_step)[..., None]
    mag = jnp.exp(lam_re * dt)
    ang = lam_im * dt
    abar_re, abar_im = mag * jnp.cos(ang), mag * jnp.sin(ang)
    den = lam_re * lam_re + lam_im * lam_im
    nr, ni = abar_re - 1.0, abar_im
    coef_re = (nr * lam_re + ni * lam_im) / den
    coef_im = (ni * lam_re - nr * lam_im) / den
    bb_re = coef_re[..., None] * b_re - coef_im[..., None] * b_im
    bb_im = coef_re[..., None] * b_im + coef_im[..., None] * b_re
    bb = jnp.stack([bb_re, bb_im])
    bb = jnp.swapaxes(bb, -1, -2).reshape(-1, SSM_STATE)
    own = ((jnp.arange(bb.shape[0]) // SSM_GROUP) % S5_KGROUPS)[:, None] == (
        jnp.arange(S5_KSTATE) // SSM_STATE)[None, :]
    bb = jnp.where(own, jnp.tile(bb, (1, S5_KGROUPS)), 0.0)
    bb = bb.reshape(2, 2, S5_NK, LANES, S5_KSTATE)
    wd = jnp.concatenate([bb[0], bb[1]], axis=-1)
    wd = jnp.transpose(wd, (1, 0, 2, 3)).reshape(S5_NK, 2 * LANES, 2 * S5_KSTATE).astype(BF16)
    cc = jnp.stack([c_re, -c_im])
    cc = jnp.swapaxes(cc, -1, -2).reshape(-1, SSM_GROUP)
    own = ((jnp.arange(cc.shape[0]) // SSM_STATE) % S5_KGROUPS)[:, None] == (
        jnp.arange(LANES) // SSM_GROUP)[None, :]
    cc = jnp.where(own, jnp.tile(cc, (1, S5_KGROUPS)), 0.0)
    cc = cc.reshape(2, 2, S5_NK, S5_KSTATE, LANES)
    wr = jnp.concatenate([cc[:, 0], cc[:, 1]], axis=-1)
    wr = jnp.transpose(wr, (1, 0, 2, 3)).reshape(S5_NK, 2 * S5_KSTATE, 2 * LANES).astype(BF16)
    a = jnp.stack([abar_re.reshape(2, STATE_W), abar_im.reshape(2, STATE_W)])
    a = jnp.repeat(a, BATCH, axis=1)
    return a, wd, wr


def _dft_tables(n):
    j = np.arange(n, dtype=np.int64)
    ang = 2.0 * np.pi * ((j[:, None] * j[None, :]) % n).astype(np.float64) / n
    s = 1.0 / math.sqrt(n)
    return (np.cos(ang) * s).astype(np.float32), (np.sin(ang) * s).astype(np.float32)


_CH_COS, _CH_SIN = _dft_tables(FFT_GROUP)
_CH_CS = np.concatenate([_CH_COS, _CH_SIN], axis=1)


def _chdft_kernel(u_ref, w_ref, p_ref, q_ref):
    w = w_ref[...].astype(BF16)
    for g in range(FFT_GROUPS):
        cs = slice(g * FFT_GROUP, (g + 1) * FFT_GROUP)
        r = _dot(u_ref[:, cs], w)
        p_ref[:, cs] = r[:, :FFT_GROUP].astype(BF16)
        q_ref[:, cs] = r[:, FFT_GROUP:].astype(BF16)


def _channel_dft(rest):
    nb, rows, _ = rest.shape
    tm = min(rows, 2048)
    shape = jax.ShapeDtypeStruct((nb, rows, FFT_WIDTH), BF16)
    blk = pl.BlockSpec((None, tm, FFT_WIDTH), lambda b, i: (b, i, 0))
    return pl.pallas_call(
        _chdft_kernel,
        grid=(nb, rows // tm),
        in_specs=[blk, pl.BlockSpec((FFT_GROUP, 2 * FFT_GROUP), lambda b, i: (0, 0))],
        out_specs=[blk, blk],
        out_shape=[shape, shape],
        compiler_params=_params(("parallel", "parallel"), 40),
        name="channel_dft",
    )(rest, jnp.asarray(_CH_CS))


DFT_RADIX = 4
DFT_ILV = 256
_QUARTER_TURN = ((1, 0), (0, 1), (-1, 0), (0, -1))


def _radix_tables(n):
    nq = n // DFT_RADIX
    k = np.arange(nq, dtype=np.int64)
    s = 1.0 / math.sqrt(n)
    cos, sin = [], []
    for r in range(DFT_RADIX):
        j = DFT_RADIX * k + r
        ang = 2.0 * np.pi * ((j[:, None] * k[None, :]) % n).astype(np.float64) / n
        cos.append(np.cos(ang) * s)
        sin.append(np.sin(ang) * s)
    return np.stack(cos).astype(np.float32), np.stack(sin).astype(np.float32)


def _interleave_perm():
    per = DFT_ILV // DFT_RADIX
    perm = np.zeros((DFT_ILV, DFT_ILV), np.float32)
    for r in range(DFT_RADIX):
        for m in range(per):
            perm[DFT_RADIX * m + r, r * per + m] = 1.0
    return perm


_RADIX_TABLES = {n: _radix_tables(n) for n in (CTX_LEN, SEQ)}
_ILV_PERM = _interleave_perm()


def _posdft_kernel(n, p_ref, q_ref, c_ref, s_ref, perm_ref, o_ref, comb_ref, y_ref):
    nq = n // DFT_RADIX
    chunk = min(nq, EPILOGUE_ROWS)

    def combine(i, carry):
        rs = pl.ds(pl.multiple_of(i * chunk, chunk), chunk)
        p = [p_ref[pl.ds(pl.multiple_of(q * nq + i * chunk, chunk), chunk), :].astype(F32)
             for q in range(DFT_RADIX)]
        qq = [q_ref[pl.ds(pl.multiple_of(q * nq + i * chunk, chunk), chunk), :].astype(F32)
              for q in range(DFT_RADIX)]
        def signed_sum(terms):
            acc = None
            for sign, v in terms:
                if acc is None:
                    acc = v if sign > 0 else -v
                else:
                    acc = acc + v if sign > 0 else acc - v
            return acc

        for r in range(DFT_RADIX):
            pr, qr = [], []
            for q in range(DFT_RADIX):
                cs, sn = _QUARTER_TURN[(r * q) % DFT_RADIX]
                if cs:
                    pr.append((cs, p[q]))
                    qr.append((cs, qq[q]))
                if sn:
                    pr.append((-sn, qq[q]))
                    qr.append((sn, p[q]))
            comb_ref[2 * r, rs, :] = signed_sum(pr).astype(BF16)
            comb_ref[2 * r + 1, rs, :] = signed_sum(qr).astype(BF16)
        return carry

    lax.fori_loop(0, nq // chunk, combine, 0)
    for r in range(DFT_RADIX):
        y_ref[r] = (_dot(c_ref[r].astype(BF16), comb_ref[2 * r])
                    - _dot(s_ref[r].astype(BF16), comb_ref[2 * r + 1]))
    per = DFT_ILV // DFT_RADIX
    for blk in range(n // DFT_ILV):
        slab = jnp.concatenate([y_ref[r, blk * per:(blk + 1) * per, :] for r in range(DFT_RADIX)], axis=0)
        o_ref[blk * DFT_ILV:(blk + 1) * DFT_ILV, :] = _dot(perm_ref[...], slab.astype(BF16)).astype(BF16)


def _position_dft(p, q):
    nb, steps, _ = p.shape
    cos, sin = _RADIX_TABLES[steps]
    nq = steps // DFT_RADIX
    x_blk = pl.BlockSpec((None, steps, FFT_WIDTH), lambda b: (b, 0, 0))
    t_blk = pl.BlockSpec((DFT_RADIX, nq, nq), lambda b: (0, 0, 0), pipeline_mode=pl.Buffered(1))
    return pl.pallas_call(
        functools.partial(_posdft_kernel, steps),
        grid=(nb,),
        in_specs=[x_blk, x_blk, t_blk, t_blk, pl.BlockSpec((DFT_ILV, DFT_ILV), lambda b: (0, 0))],
        out_specs=x_blk,
        out_shape=jax.ShapeDtypeStruct((nb, steps, FFT_WIDTH), BF16),
        scratch_shapes=[pltpu.VMEM((2 * DFT_RADIX, nq, FFT_WIDTH), BF16),
                        pltpu.VMEM((DFT_RADIX, nq, FFT_WIDTH), F32)],
        compiler_params=_params(("parallel",), 56),
        name="position_dft",
    )(p, q, jnp.asarray(cos), jnp.asarray(sin), jnp.asarray(_ILV_PERM, BF16))


def _glu_kernel(us_ref, yf_ref, yb_ref, dsk_ref, w_ref, o_ref):
    ys = dsk_ref[...] * us_ref[...] + yf_ref[...].astype(F32) + yb_ref[...].astype(F32)
    g = jax.nn.gelu(ys)
    z = _dot(g.astype(BF16), w_ref[...].astype(BF16))
    o_ref[...] = (g * jax.nn.sigmoid(z)).astype(BF16)


def _glu(us, yf, yb, d_skip, w_glu, layer):
    nb, rows, _ = us.shape
    tm = 1024
    blk = pl.BlockSpec((None, tm, SSM_WIDTH), lambda b, i: (b, i, 0))
    return pl.pallas_call(
        _glu_kernel,
        grid=(nb, rows // tm),
        in_specs=[blk, blk, blk,
                  pl.BlockSpec((None, 1, SSM_WIDTH), lambda b, i: (layer, 0, 0)),
                  pl.BlockSpec((None, SSM_WIDTH, SSM_WIDTH), lambda b, i: (layer, 0, 0))],
        out_specs=blk,
        out_shape=jax.ShapeDtypeStruct((nb, rows, SSM_WIDTH), BF16),
        compiler_params=_params(("parallel", "parallel"), 40),
        name="s5_glu",
    )(us, yf, yb, d_skip.reshape(DEPTH, 1, SSM_WIDTH), w_glu)


MERGE_SPLIT = 8


def _merge_kernel(s_ref, yq_ref, gs_ref, gf_ref, wps_ref, wpf_ref, o_ref):
    wps = wps_ref[...].astype(BF16)
    wpf = wpf_ref[...].astype(BF16)
    piece = o_ref.shape[0] // MERGE_SPLIT
    for r in range(MERGE_SPLIT):
        rs = slice(r * piece, (r + 1) * piece)
        ps = _dot(s_ref[rs, :], wps)
        pf = _dot(yq_ref[rs, :], wpf)
        o_ref[rs, :] = (jax.nn.sigmoid(gs_ref[rs, :].astype(F32)) * ps
                        + jax.nn.sigmoid(gf_ref[rs, :].astype(F32)) * pf).astype(BF16)


def _merge(s, yq, rest, w_ps, w_pf, layer):
    nb, rows, _ = s.shape
    tm = min(rows, 2048)
    tn = 512
    gs_off = FFT_WIDTH // tn
    gf_off = (FFT_WIDTH + D_MODEL) // tn
    row_blk = pl.BlockSpec((None, tm, SSM_WIDTH), lambda b, i, j: (b, i, 0))
    w_blk = pl.BlockSpec((None, SSM_WIDTH, tn), lambda b, i, j: (layer, 0, j))
    return pl.pallas_call(
        _merge_kernel,
        grid=(nb, rows // tm, D_MODEL // tn),
        in_specs=[
            row_blk, row_blk,
            pl.BlockSpec((None, tm, tn), lambda b, i, j: (b, i, gs_off + j)),
            pl.BlockSpec((None, tm, tn), lambda b, i, j: (b, i, gf_off + j)),
            w_blk, w_blk,
        ],
        out_specs=pl.BlockSpec((None, tm, tn), lambda b, i, j: (b, i, j)),
        out_shape=jax.ShapeDtypeStruct((nb, rows, D_MODEL), BF16),
        compiler_params=_params(("parallel", "parallel", "arbitrary"), 48),
        name="gated_merge",
    )(s, yq, rest, rest, w_ps, w_pf)


TAIL_TM = 1024
TAIL_TK = 512
TAIL_TF = 512


def _tail_kernel(layer, emit_next, merged_ref, x_hbm, wo_hbm, wu_hbm, wd_hbm, mod_ref, *refs):
    if emit_next:
        modn_ref, refs = refs[0], refs[1:]
    g1_ref, b1_ref, g2_ref, b2_ref, out_hbm = refs[:5]
    refs = refs[5:]
    if emit_next:
        hn_ref, refs = refs[0], refs[1:]
    xbuf, acc, h2, act, wa, wb, sem_a, sem_b, sem_x, sem_o = refs
    b = pl.program_id(0)
    i = pl.program_id(1)
    first = jnp.logical_and(b == 0, i == 0)
    last = jnp.logical_and(b == pl.num_programs(0) - 1, i == pl.num_programs(1) - 1)
    nk = D_MODEL // TAIL_TK
    nf = D_FF // TAIL_TF
    assert nk % 2 == 0
    rows = pl.ds(pl.multiple_of(i * TAIL_TM, TAIL_TM), TAIL_TM)

    def wo_copy(t, slot):
        return pltpu.make_async_copy(wo_hbm.at[layer, pl.ds(t * TAIL_TK, TAIL_TK), :], wa.at[slot], sem_a.at[slot])

    def wd_copy(f, slot):
        src = wd_hbm.at[layer, pl.ds(pl.multiple_of(f * TAIL_TK, TAIL_TK), TAIL_TK), :]
        return pltpu.make_async_copy(src, wa.at[slot], sem_a.at[slot])

    def wu_copy(f, slot):
        src = wu_hbm.at[layer, :, pl.ds(pl.multiple_of(f * TAIL_TF, TAIL_TF), TAIL_TF)]
        return pltpu.make_async_copy(src, wb.at[slot], sem_b.at[slot])

    def x_copy():
        return pltpu.make_async_copy(x_hbm.at[b, rows, :], xbuf, sem_x.at[0])

    def out_copy():
        return pltpu.make_async_copy(xbuf, out_hbm.at[b, rows, :], sem_o.at[0])

    wo_copy(0, 0).start()
    wo_copy(1, 1).start()
    wu_copy(0, 0).start()
    wu_copy(1, 1).start()

    for k in range(nk):
        slot = k % 2
        wo_copy(k, slot).wait()
        lhs = merged_ref[:, k * TAIL_TK:(k + 1) * TAIL_TK]
        if k == 0:
            acc[...] = _dot(lhs, wa[slot].astype(BF16))
        else:
            acc[...] += _dot(lhs, wa[slot].astype(BF16))
        if k + 2 < nk:
            wo_copy(k + 2, slot).start()
        else:
            wd_copy(k + 2 - nk, slot).start()
        if k == 0:
            @pl.when(jnp.logical_not(first))
            def _():
                out_copy().wait()

            x_copy().start()
    x_copy().wait()

    def finish1(rs):
        x1 = _deepnorm_ln(xbuf[rs, :], _row(mod_ref, GATE1), acc[rs, :]) * g1_ref[...] + b1_ref[...]
        xbuf[rs, :] = x1
        h2[rs, :] = _modulated_ln(x1, mod_ref, SHIFT2, SCALE2)

    _for_row_chunks(TAIL_TM, finish1)

    def up_tile(slot):
        a = jnp.maximum(_dot(h2[...], wb[slot].astype(BF16)), 0.0)
        act[slot] = (a * a).astype(BF16)

    def down_tile(slot, first=False):
        if first:
            acc[...] = _dot(act[slot], wa[slot].astype(BF16))
        else:
            acc[...] += _dot(act[slot], wa[slot].astype(BF16))

    def trip(f, slot):
        nxt = 1 - slot
        wu_copy(f + 1, nxt).wait()
        wd_copy(f, slot).wait()
        up_tile(nxt)
        down_tile(slot, first=isinstance(f, int) and f == 0)
        if isinstance(f, int):
            if f + 3 < nf:
                wu_copy(f + 3, nxt).start()
            if f + 2 < nf:
                wd_copy(f + 2, slot).start()
        else:
            @pl.when(f + 3 < nf)
            def _():
                wu_copy(f + 3, nxt).start()

            @pl.when(f + 2 < nf)
            def _():
                wd_copy(f + 2, slot).start()

    wu_copy(0, 0).wait()
    up_tile(0)
    wu_copy(2, 0).start()

    trip(0, 0)
    assert nf % 2 == 0

    def trip_pair(p, carry):
        trip(2 * p + 1, 1)
        trip(2 * p + 2, 0)
        return carry

    lax.fori_loop(0, (nf - 2) // 2, trip_pair, 0)

    def finish2(rs):
        x2 = _deepnorm_ln(xbuf[rs, :], _row(mod_ref, GATE2), acc[rs, :]) * g2_ref[...] + b2_ref[...]
        xbuf[rs, :] = x2
        if emit_next:
            hn_ref[rs, :] = _modulated_ln(x2, modn_ref, SHIFT1, SCALE1)

    wd_copy(nf - 1, (nf - 1) % 2).wait()
    down_tile((nf - 1) % 2)
    _for_row_chunks(TAIL_TM, finish2)
    out_copy().start()

    @pl.when(last)
    def _():
        out_copy().wait()


def _tail(merged, x, mod, w_o, w_up, w_down, g1, b1, g2, b2, layer, emit_next):
    nb, rows, _ = x.shape
    tm = TAIL_TM
    row_blk = pl.BlockSpec((None, tm, D_MODEL), lambda b_, i: (b_, i, 0))
    vec = pl.BlockSpec((None, 1, D_MODEL), lambda b_, i: (layer, 0, 0))
    hbm = pl.BlockSpec(memory_space=pl.ANY)
    in_specs = [row_blk, hbm, hbm, hbm, hbm, _mod_spec(layer, nb)]
    args = [merged, x, w_o, w_up, w_down, mod]
    if emit_next:
        in_specs.append(_mod_spec(layer + 1, nb))
        args.append(mod)
    in_specs += [vec, vec, vec, vec]
    args += [v.reshape(DEPTH, 1, D_MODEL) for v in (g1, b1, g2, b2)]
    out_specs = [hbm]
    out_shape = [jax.ShapeDtypeStruct(x.shape, F32)]
    if emit_next:
        out_specs.append(row_blk)
        out_shape.append(jax.ShapeDtypeStruct(x.shape, BF16))
    out = pl.pallas_call(
        functools.partial(_tail_kernel, layer, emit_next),
        grid=(nb, rows // tm),
        in_specs=in_specs,
        out_specs=out_specs,
        out_shape=out_shape,
        scratch_shapes=[
            pltpu.VMEM((tm, D_MODEL), F32),
            pltpu.VMEM((tm, D_MODEL), F32),
            pltpu.VMEM((tm, D_MODEL), BF16),
            pltpu.VMEM((2, tm, TAIL_TF), BF16),
            pltpu.VMEM((2, TAIL_TK, D_MODEL), F32),
            pltpu.VMEM((2, D_MODEL, TAIL_TF), F32),
            pltpu.SemaphoreType.DMA((2,)),
            pltpu.SemaphoreType.DMA((2,)),
            pltpu.SemaphoreType.DMA((1,)),
            pltpu.SemaphoreType.DMA((1,)),
        ],
        compiler_params=_params(("arbitrary", "arbitrary"), 58),
        name="layer_tail",
    )(*args)
    return out if emit_next else (out[0], None)


def _pos_table():
    quarter = D_MODEL // 4
    omega = 1.0 / (POS_BASE ** (np.arange(quarter, dtype=np.float64) / quarter))
    t = np.arange(SEQ)
    ar = (t // GRID_W).astype(np.float64)[:, None] * omega
    ac = (t % GRID_W).astype(np.float64)[:, None] * omega
    return np.concatenate([np.sin(ar), np.cos(ar), np.sin(ac), np.cos(ac)], axis=-1).astype(np.float32)


_POS = _pos_table()


def _mod_tables(m):
    m = m.reshape(DEPTH, SUBLANES, 6, D_MODEL)
    m = jnp.pad(m, ((0, 0), (0, 0), (0, SUBLANES - 6), (0, 0)))
    return m[:, :BATCH], m[:, BATCH:BATCH + 1]


def _as_batch(a):
    return a.reshape(BATCH, CTX_LEN, a.shape[-1])


def _as_slab(a):
    return a.reshape(1, BATCH * CTX_LEN, a.shape[-1])


def kernel(x, c, ctx, c_ctx, w_mod, b_mod, w_in, lam_re, lam_im, log_step, ssm_b_re, ssm_b_im, ssm_c_re,
           ssm_c_im, d_skip, w_glu, w_ps, w_pf, w_o, ln1_g, ln1_b, w_up, w_down, ln2_g, ln2_b):
    cond8 = jnp.concatenate([c, c_ctx[None], jnp.zeros((SUBLANES - BATCH - 1, D_MODEL), F32)], axis=0)
    mod_lat, mod_ctx = _mod_tables(_modulation(cond8, w_mod, b_mod))

    x_lat, h_lat = _entry(x, jnp.asarray(_POS), mod_lat)
    x_ctx, h_ctx = _entry(_as_slab(ctx), None, mod_ctx)
    h_zero = jnp.zeros((2, SUBLANES, STATE_W), F32)
    prm = jax.vmap(_s5_params)(lam_re, lam_im, log_step, ssm_b_re, ssm_b_im, ssm_c_re, ssm_c_im)

    for l in range(DEPTH):
        need_ctx = l < DEPTH - 1

        def mixer_tail(us, rest, yf, yb, xx, mod, as_batch, as_rows):
            p, q = _channel_dft(rest)
            yq = as_rows(_position_dft(as_batch(p), as_batch(q)))
            s = _glu(us, as_rows(yf), as_rows(yb), d_skip, w_glu, l)
            merged = _merge(s, yq, rest, w_ps, w_pf, l)
            return _tail(merged, xx, mod, w_o, w_up, w_down, ln1_g, ln1_b, ln2_g, ln2_b, l, need_ctx)

        us_lat, rest_lat = _in_proj(h_lat, w_in, l, True)
        if need_ctx:
            us_ctx, rest_ctx = _in_proj(h_ctx, w_in, l, True)
            yf_c, yb_c, h_t = _s5(_as_batch(us_ctx), h_zero, prm, l, True)
        else:
            (us_ctx,) = _in_proj(h_ctx, w_in, l, False)
            (h_t,) = _s5(_as_batch(us_ctx), h_zero, prm, l, False)
        yf, yb, _ = _s5(us_lat, h_t, prm, l, True)
        ident = lambda a: a
        x_lat, h_lat = mixer_tail(us_lat, rest_lat, yf, yb, x_lat, mod_lat, ident, ident)
        if need_ctx:
            x_ctx, h_ctx = mixer_tail(us_ctx, rest_ctx, yf_c, yb_c, x_ctx, mod_ctx, _as_batch, _as_slab)

    return x_lat
```

```python
import functools
import math

import numpy as np
import jax
import jax.numpy as jnp
from jax import lax
from jax.experimental import pallas as pl
from jax.experimental.pallas import tpu as pltpu

D_MODEL = 2048
BATCH = 4
SEQ = 2048
DEPTH = 2
GRID_W = 64
CTX_LEN = 256
SSM_WIDTH = D_MODEL // 2
SSM_GROUP = 16
SSM_GROUPS = SSM_WIDTH // SSM_GROUP
SSM_STATE = 64
FFT_WIDTH = D_MODEL - SSM_WIDTH
FFT_GROUPS = 4
FFT_GROUP = FFT_WIDTH // FFT_GROUPS
IN_WIDTH = SSM_WIDTH + FFT_WIDTH + 2 * D_MODEL
REST_WIDTH = IN_WIDTH - SSM_WIDTH
D_FF = 4 * D_MODEL
ALPHA = (2 * DEPTH) ** 0.25
LN_EPS = 1e-5
POS_BASE = 10000.0

F32 = jnp.float32
BF16 = jnp.bfloat16

SUBLANES = 8
LANES = 128
STATE_W = SSM_GROUPS * SSM_STATE
SCAN_T = 64
S5_KGROUPS = 8
S5_NK = SSM_GROUPS // S5_KGROUPS
S5_KSTATE = S5_KGROUPS * SSM_STATE
MIB = 1024 * 1024

SHIFT1, SCALE1, GATE1, SHIFT2, SCALE2, GATE2 = range(6)


def _params(sem, vmem_mib):
    return pltpu.CompilerParams(dimension_semantics=sem, vmem_limit_bytes=vmem_mib * MIB)


def _dot(a, b):
    return jnp.dot(a, b, preferred_element_type=F32)


def _ln(x, eps=LN_EPS):
    mu = jnp.mean(x, axis=-1, keepdims=True)
    xc = x - mu
    var = jnp.mean(xc * xc, axis=-1, keepdims=True)
    return xc * lax.rsqrt(var + eps)


def _deepnorm_ln(x, gate, branch):
    return _ln(x + (gate * (1.0 / ALPHA)) * branch, LN_EPS / (ALPHA * ALPHA))


def _row(ref, j):
    return ref[j:j + 1, :]


def _modulated_ln(x, mod_ref, shift, scale):
    return (_ln(x) * (1.0 + _row(mod_ref, scale)) + _row(mod_ref, shift)).astype(BF16)


EPILOGUE_ROWS = 128


def _for_row_chunks(rows, fn):
    def body(r, carry):
        fn(pl.ds(pl.multiple_of(r * EPILOGUE_ROWS, EPILOGUE_ROWS), EPILOGUE_ROWS))
        return carry

    lax.fori_loop(0, rows // EPILOGUE_ROWS, body, 0)


def _mod_spec(layer, nb):
    if nb == 1:
        return pl.BlockSpec((None, None, SUBLANES, D_MODEL), lambda b, *_: (layer, 0, 0, 0))
    return pl.BlockSpec((None, None, SUBLANES, D_MODEL), lambda b, *_: (layer, b, 0, 0))


def _mod_kernel(c_ref, w_ref, b_ref, o_ref):
    c = c_ref[...]
    sc = c * jax.nn.sigmoid(c)
    o_ref[...] = _dot(sc.astype(BF16), w_ref[...].astype(BF16)) + b_ref[...]


def _modulation(cond8, w_mod, b_mod):
    tn = 1024
    n = 6 * D_MODEL
    return pl.pallas_call(
        _mod_kernel,
        grid=(DEPTH, n // tn),
        in_specs=[
            pl.BlockSpec((SUBLANES, D_MODEL), lambda l, j: (0, 0)),
            pl.BlockSpec((None, D_MODEL, tn), lambda l, j: (l, 0, j)),
            pl.BlockSpec((None, 1, tn), lambda l, j: (l, 0, j)),
        ],
        out_specs=pl.BlockSpec((None, SUBLANES, tn), lambda l, j: (l, 0, j)),
        out_shape=jax.ShapeDtypeStruct((DEPTH, SUBLANES, n), F32),
        compiler_params=_params(("parallel", "parallel"), 40),
        name="adaln_modulation",
    )(cond8, w_mod, b_mod.reshape(DEPTH, 1, n))


def _entry_kernel(has_pos, *refs):
    if has_pos:
        x_ref, p_ref, mod_ref, xo_ref, h_ref = refs
        x = x_ref[...] + p_ref[...]
        xo_ref[...] = x
    else:
        x_ref, mod_ref, h_ref = refs
        x = x_ref[...]
    h_ref[...] = _modulated_ln(x, mod_ref, SHIFT1, SCALE1)


def _entry(x, pos, mod):
    nb, rows, _ = x.shape
    tm = 512
    has_pos = pos is not None
    blk = pl.BlockSpec((None, tm, D_MODEL), lambda i, b: (b, i, 0))
    in_specs = [blk]
    args = [x]
    if has_pos:
        in_specs.append(pl.BlockSpec((tm, D_MODEL), lambda i, b: (i, 0)))
        args.append(pos)
    in_specs.append(pl.BlockSpec((None, None, SUBLANES, D_MODEL), lambda i, b: (0, b, 0, 0)))
    args.append(mod)
    h_shape = jax.ShapeDtypeStruct(x.shape, BF16)
    out = pl.pallas_call(
        functools.partial(_entry_kernel, has_pos),
        grid=(rows // tm, nb),
        in_specs=in_specs,
        out_specs=[blk, blk] if has_pos else [blk],
        out_shape=[jax.ShapeDtypeStruct(x.shape, F32), h_shape] if has_pos else [h_shape],
        compiler_params=_params(("parallel", "parallel"), 40),
        name="entry_ln",
    )(*args)
    return out if has_pos else (x, out[0])


ROW_SPLIT = 2


def _win_kernel(n_us, with_rest, h_ref, w_ref, us_ref, *rest):
    piece = h_ref.shape[0] // ROW_SPLIT

    def project(o_ref):
        w = w_ref[...].astype(BF16)
        for r in range(ROW_SPLIT):
            rs = slice(r * piece, (r + 1) * piece)
            o_ref[rs, :] = _dot(h_ref[rs, :], w).astype(o_ref.dtype)

    if not with_rest:
        project(us_ref)
        return
    rest_ref, = rest
    j = pl.program_id(2)

    @pl.when(j < n_us)
    def _():
        project(us_ref)

    @pl.when(j >= n_us)
    def _():
        project(rest_ref)


def _in_proj(h, w_in, layer, with_rest):
    nb, rows, _ = h.shape
    tm = min(rows, 2048)
    tn = 512
    n_us = SSM_WIDTH // tn
    n_tiles = (IN_WIDTH if with_rest else SSM_WIDTH) // tn
    out_specs = [pl.BlockSpec((None, tm, tn), lambda b, i, j: (b, i, jnp.minimum(j, n_us - 1)))]
    out_shape = [jax.ShapeDtypeStruct((nb, rows, SSM_WIDTH), F32)]
    if with_rest:
        out_specs.append(pl.BlockSpec((None, tm, tn), lambda b, i, j: (b, i, jnp.maximum(j - n_us, 0))))
        out_shape.append(jax.ShapeDtypeStruct((nb, rows, REST_WIDTH), BF16))
    return pl.pallas_call(
        functools.partial(_win_kernel, n_us, with_rest),
        grid=(nb, rows // tm, n_tiles),
        in_specs=[
            pl.BlockSpec((None, tm, D_MODEL), lambda b, i, j: (b, i, 0)),
            pl.BlockSpec((None, D_MODEL, tn), lambda b, i, j: (layer, 0, j)),
        ],
        out_specs=out_specs,
        out_shape=out_shape,
        compiler_params=_params(("parallel", "parallel", "arbitrary"), 48),
        name="in_proj",
    )(h, w_in)


def _s5_kernel(need_y, uf_ref, ub_ref, h0_ref, pf_ref, pb_ref, pft_ref, pbt_ref, a_ref, wd_ref, wr_ref,
               *rest):
    if need_y:
        yf_ref, yb_ref, ht_ref, v_ref, hc_ref, h_ref, ycf_ref, ycb_ref = rest
    else:
        ht_ref, v_ref, hc_ref = rest
    g = pl.program_id(0)
    rows_in = BATCH * SCAN_T
    pair = 2 * SUBLANES

    @pl.when(g == 0)
    def _():
        hc_ref[...] = h0_ref[...]

    uf = uf_ref[...].reshape(rows_in, SSM_WIDTH).astype(BF16)
    ub = ub_ref[...].reshape(rows_in, SSM_WIDTH).astype(BF16)
    up_f = _dot(pf_ref[...], uf).astype(BF16)
    up_b = _dot(pb_ref[...], ub).astype(BF16)

    def slab_cols(k):
        return slice(2 * k * S5_KSTATE, (2 * k + 2) * S5_KSTATE)

    def drive(k):
        cs = slice(k * LANES, (k + 1) * LANES)
        lhs = jnp.concatenate([up_f[:, cs], up_b[:, cs]], axis=1)
        v_ref[:, slab_cols(k)] = _dot(lhs, wd_ref[k])

    def scan(k):
        ss = slice(k * S5_KSTATE, (k + 1) * S5_KSTATE)
        re_cols = slice(2 * k * S5_KSTATE, (2 * k + 1) * S5_KSTATE)
        im_cols = slice((2 * k + 1) * S5_KSTATE, (2 * k + 2) * S5_KSTATE)
        ar = a_ref[0, :, ss]
        ai = a_ref[1, :, ss]
        hr = hc_ref[0, :, ss]
        hi = hc_ref[1, :, ss]
        for s2 in range(SCAN_T // 2):
            rows = []
            for s in (2 * s2, 2 * s2 + 1):
                rs = slice(s * SUBLANES, (s + 1) * SUBLANES)
                nr = ar * hr - ai * hi + v_ref[rs, re_cols]
                ni = ar * hi + ai * hr + v_ref[rs, im_cols]
                hr, hi = nr, ni
                rows.append((nr, ni))
            if need_y:
                ps = slice(s2 * pair, (s2 + 1) * pair)
                h_ref[ps, re_cols] = jnp.concatenate([rows[0][0], rows[1][0]], axis=0).astype(BF16)
                h_ref[ps, im_cols] = jnp.concatenate([rows[0][1], rows[1][1]], axis=0).astype(BF16)
        hc_ref[0, :, ss] = hr
        hc_ref[1, :, ss] = hi

    def readout(k):
        cs = slice(k * LANES, (k + 1) * LANES)
        y = _dot(h_ref[:, slab_cols(k)], wr_ref[k])
        ycf_ref[:, cs] = y[:, :LANES]
        ycb_ref[:, cs] = y[:, LANES:]

    lag = 1
    for k in range(S5_NK + lag):
        if k < S5_NK:
            drive(k)
            scan(k)
        if need_y and k >= lag:
            readout(k - lag)
    if need_y:
        yf = _dot(pft_ref[...], ycf_ref[...].astype(BF16))
        yb = _dot(pbt_ref[...], ycb_ref[...].astype(BF16))
        yf_ref[...] = yf.astype(BF16).reshape(BATCH, SCAN_T, SSM_WIDTH)
        yb_ref[...] = yb.astype(BF16).reshape(BATCH, SCAN_T, SSM_WIDTH)

    @pl.when(g == pl.num_programs(0) - 1)
    def _():
        ht_ref[...] = hc_ref[...]


def _scan_perms():
    t = SCAN_T
    pf = np.zeros((2 * BATCH * t, BATCH * t), np.float32)
    pb = np.zeros((2 * BATCH * t, BATCH * t), np.float32)
    for s in range(t):
        for b in range(BATCH):
            pf[s * 2 * BATCH + b, b * t + s] = 1.0
            pb[s * 2 * BATCH + BATCH + b, b * t + (t - 1 - s)] = 1.0
    return pf, pb


_PF, _PB = _scan_perms()


def _s5(u, h0, prm, layer, need_y):
    a, wd, wr = prm
    steps = u.shape[1]
    n = steps // SCAN_T
    rows_in = BATCH * SCAN_T
    rows_sc = 2 * rows_in
    pf = jnp.asarray(_PF, BF16)
    pb = jnp.asarray(_PB, BF16)
    const2 = lambda g: (0, 0)
    once = pl.Buffered(1)
    blk = (BATCH, SCAN_T, SSM_WIDTH)
    fwd_blk = pl.BlockSpec(blk, lambda g: (0, g, 0))
    bwd_blk = pl.BlockSpec(blk, lambda g: (0, n - 1 - g, 0))
    state_spec = pl.BlockSpec((2, SUBLANES, STATE_W), lambda g: (0, 0, 0))
    in_specs = [
        fwd_blk, bwd_blk, state_spec,
        pl.BlockSpec((rows_sc, rows_in), const2),
        pl.BlockSpec((rows_sc, rows_in), const2),
        pl.BlockSpec((rows_in, rows_sc), const2),
        pl.BlockSpec((rows_in, rows_sc), const2),
        pl.BlockSpec((None, 2, SUBLANES, STATE_W), lambda g: (layer, 0, 0, 0)),
        pl.BlockSpec((None, S5_NK, 2 * LANES, 2 * S5_KSTATE), lambda g: (layer, 0, 0, 0), pipeline_mode=once),
        pl.BlockSpec((None, S5_NK, 2 * S5_KSTATE, 2 * LANES), lambda g: (layer, 0, 0, 0), pipeline_mode=once),
    ]
    state_shape = jax.ShapeDtypeStruct((2, SUBLANES, STATE_W), F32)
    scratch = [pltpu.VMEM((rows_sc, 2 * STATE_W), F32), pltpu.VMEM((2, SUBLANES, STATE_W), F32)]
    if need_y:
        out_specs = [fwd_blk, bwd_blk, state_spec]
        y_shape = jax.ShapeDtypeStruct((BATCH, steps, SSM_WIDTH), BF16)
        out_shape = [y_shape, y_shape, state_shape]
        scratch += [pltpu.VMEM((rows_sc, 2 * STATE_W), BF16), pltpu.VMEM((rows_sc, SSM_WIDTH), F32),
                    pltpu.VMEM((rows_sc, SSM_WIDTH), F32)]
    else:
        out_specs = [state_spec]
        out_shape = [state_shape]
    return pl.pallas_call(
        functools.partial(_s5_kernel, need_y),
        grid=(n,),
        in_specs=in_specs,
        out_specs=out_specs,
        out_shape=out_shape,
        scratch_shapes=scratch,
        compiler_params=_params(("arbitrary",), 52),
        name="s5_scan",
    )(u, u, h0, pf, pb, pf.T, pb.T, a, wd, wr)


def _s5_params(lam_re, lam_im, log_step, b_re, b_im, c_re, c_im):
    dt = jnp.exp(log_step)[..., None]
    mag = jnp.exp(lam_re * dt)
    ang = lam_im * dt
    abar_re, abar_im = mag * jnp.cos(ang), mag * jnp.sin(ang)
    den = lam_re * lam_re + lam_im * lam_im
    nr, ni = abar_re - 1.0, abar_im
    coef_re = (nr * lam_re + ni * lam_im) / den
    coef_im = (ni * lam_re - nr * lam_im) / den
    bb_re = coef_re[..., None] * b_re - coef_im[..., None] * b_im
    bb_im = coef_re[..., None] * b_im + coef_im[..., None] * b_re
    bb = jnp.stack([bb_re, bb_im])
    bb = jnp.swapaxes(bb, -1, -2).reshape(-1, SSM_STATE)
    own = ((np.arange(bb.shape[0]) // SSM_GROUP) % S5_KGROUPS)[:, None] == (
        np.arange(S5_KSTATE) // SSM_STATE)[None, :]
    bb = jnp.where(own, jnp.tile(bb, (1, S5_KGROUPS)), 0.0)
    bb = bb.reshape(2, 2, S5_NK, LANES, S5_KSTATE)
    wd = jnp.concatenate([bb[0], bb[1]], axis=-1)
    wd = jnp.transpose(wd, (1, 0, 2, 3)).reshape(S5_NK, 2 * LANES, 2 * S5_KSTATE).astype(BF16)
    cc = jnp.stack([c_re, -c_im])
    cc = jnp.swapaxes(cc, -1, -2).reshape(-1, SSM_GROUP)
    own = ((np.arange(cc.shape[0]) // SSM_STATE) % S5_KGROUPS)[:, None] == (
        np.arange(LANES) // SSM_GROUP)[None, :]
    cc = jnp.where(own, jnp.tile(cc, (1, S5_KGROUPS)), 0.0)
    cc = cc.reshape(2, 2, S5_NK, S5_KSTATE, LANES)
    wr = jnp.concatenate([cc[:, 0], cc[:, 1]], axis=-1)
    wr = jnp.transpose(wr, (1, 0, 2, 3)).reshape(S5_NK, 2 * S5_KSTATE, 2 * LANES).astype(BF16)
    a = jnp.stack([abar_re.reshape(2, STATE_W), abar_im.reshape(2, STATE_W)])
    a = jnp.repeat(a, BATCH, axis=1)
    return a, wd, wr


def _dft_tables(n):
    j = np.arange(n, dtype=np.int64)
    ang = 2.0 * np.pi * ((j[:, None] * j[None, :]) % n).astype(np.float64) / n
    s = 1.0 / math.sqrt(n)
    return (np.cos(ang) * s).astype(np.float32), (np.sin(ang) * s).astype(np.float32)


_CH_COS, _CH_SIN = _dft_tables(FFT_GROUP)
_CH_CS = np.concatenate([_CH_COS, _CH_SIN], axis=1)


def _chdft_kernel(u_ref, w_ref, p_ref, q_ref):
    w = w_ref[...].astype(BF16)
    for g in range(FFT_GROUPS):
        cs = slice(g * FFT_GROUP, (g + 1) * FFT_GROUP)
        r = _dot(u_ref[:, cs], w)
        p_ref[:, cs] = r[:, :FFT_GROUP].astype(BF16)
        q_ref[:, cs] = r[:, FFT_GROUP:].astype(BF16)


def _channel_dft(rest):
    nb, rows, _ = rest.shape
    tm = min(rows, 2048)
    shape = jax.ShapeDtypeStruct((nb, rows, FFT_WIDTH), BF16)
    blk = pl.BlockSpec((None, tm, FFT_WIDTH), lambda b, i: (b, i, 0))
    return pl.pallas_call(
        _chdft_kernel,
        grid=(nb, rows // tm),
        in_specs=[blk, pl.BlockSpec((FFT_GROUP, 2 * FFT_GROUP), lambda b, i: (0, 0))],
        out_specs=[blk, blk],
        out_shape=[shape, shape],
        compiler_params=_params(("parallel", "parallel"), 40),
        name="channel_dft",
    )(rest, jnp.asarray(_CH_CS))


DFT_RADIX = 4
DFT_ILV = 256
_QUARTER_TURN = ((1, 0), (0, 1), (-1, 0), (0, -1))


def _radix_tables(n):
    nq = n // DFT_RADIX
    k = np.arange(nq, dtype=np.int64)
    s = 1.0 / math.sqrt(n)
    cos, sin = [], []
    for r in range(DFT_RADIX):
        j = DFT_RADIX * k + r
        ang = 2.0 * np.pi * ((j[:, None] * k[None, :]) % n).astype(np.float64) / n
        cos.append(np.cos(ang) * s)
        sin.append(np.sin(ang) * s)
    return np.stack(cos).astype(np.float32), np.stack(sin).astype(np.float32)


def _interleave_perm():
    per = DFT_ILV // DFT_RADIX
    perm = np.zeros((DFT_ILV, DFT_ILV), np.float32)
    for r in range(DFT_RADIX):
        for m in range(per):
            perm[DFT_RADIX * m + r, r * per + m] = 1.0
    return perm


_RADIX_TABLES = {n: _radix_tables(n) for n in (CTX_LEN, SEQ)}
_ILV_PERM = _interleave_perm()


def _posdft_kernel(n, p_ref, q_ref, c_ref, s_ref, perm_ref, o_ref, comb_ref, y_ref):
    nq = n // DFT_RADIX
    chunk = min(nq, EPILOGUE_ROWS)

    def combine(i, carry):
        rs = pl.ds(pl.multiple_of(i * chunk, chunk), chunk)
        p = [p_ref[pl.ds(pl.multiple_of(q * nq + i * chunk, chunk), chunk), :].astype(F32)
             for q in range(DFT_RADIX)]
        qq = [q_ref[pl.ds(pl.multiple_of(q * nq + i * chunk, chunk), chunk), :].astype(F32)
              for q in range(DFT_RADIX)]
        def signed_sum(terms):
            acc = None
            for sign, v in terms:
                if acc is None:
                    acc = v if sign > 0 else -v
                else:
                    acc = acc + v if sign > 0 else acc - v
            return acc

        for r in range(DFT_RADIX):
            pr, qr = [], []
            for q in range(DFT_RADIX):
                cs, sn = _QUARTER_TURN[(r * q) % DFT_RADIX]
                if cs:
                    pr.append((cs, p[q]))
                    qr.append((cs, qq[q]))
                if sn:
                    pr.append((-sn, qq[q]))
                    qr.append((sn, p[q]))
            comb_ref[2 * r, rs, :] = signed_sum(pr).astype(BF16)
            comb_ref[2 * r + 1, rs, :] = signed_sum(qr).astype(BF16)
        return carry

    lax.fori_loop(0, nq // chunk, combine, 0)
    for r in range(DFT_RADIX):
        y_ref[r] = (_dot(c_ref[r].astype(BF16), comb_ref[2 * r])
                    - _dot(s_ref[r].astype(BF16), comb_ref[2 * r + 1]))
    per = DFT_ILV // DFT_RADIX
    for blk in range(n // DFT_ILV):
        slab = jnp.concatenate([y_ref[r, blk * per:(blk + 1) * per, :] for r in range(DFT_RADIX)], axis=0)
        o_ref[blk * DFT_ILV:(blk + 1) * DFT_ILV, :] = _dot(perm_ref[...], slab.astype(BF16)).astype(BF16)


def _position_dft(p, q):
    nb, steps, _ = p.shape
    cos, sin = _RADIX_TABLES[steps]
    nq = steps // DFT_RADIX
    x_blk = pl.BlockSpec((None, steps, FFT_WIDTH), lambda b: (b, 0, 0))
    t_blk = pl.BlockSpec((DFT_RADIX, nq, nq), lambda b: (0, 0, 0), pipeline_mode=pl.Buffered(1))
    return pl.pallas_call(
        functools.partial(_posdft_kernel, steps),
        grid=(nb,),
        in_specs=[x_blk, x_blk, t_blk, t_blk, pl.BlockSpec((DFT_ILV, DFT_ILV), lambda b: (0, 0))],
        out_specs=x_blk,
        out_shape=jax.ShapeDtypeStruct((nb, steps, FFT_WIDTH), BF16),
        scratch_shapes=[pltpu.VMEM((2 * DFT_RADIX, nq, FFT_WIDTH), BF16),
                        pltpu.VMEM((DFT_RADIX, nq, FFT_WIDTH), F32)],
        compiler_params=_params(("parallel",), 56),
        name="position_dft",
    )(p, q, jnp.asarray(cos), jnp.asarray(sin), jnp.asarray(_ILV_PERM, BF16))


def _glu_kernel(us_ref, yf_ref, yb_ref, dsk_ref, w_ref, o_ref):
    ys = dsk_ref[...] * us_ref[...] + yf_ref[...].astype(F32) + yb_ref[...].astype(F32)
    g = jax.nn.gelu(ys)
    z = _dot(g.astype(BF16), w_ref[...].astype(BF16))
    o_ref[...] = (g * jax.nn.sigmoid(z)).astype(BF16)


def _glu(us, yf, yb, d_skip, w_glu, layer):
    nb, rows, _ = us.shape
    tm = 1024
    blk = pl.BlockSpec((None, tm, SSM_WIDTH), lambda b, i: (b, i, 0))
    return pl.pallas_call(
        _glu_kernel,
        grid=(nb, rows // tm),
        in_specs=[blk, blk, blk,
                  pl.BlockSpec((None, 1, SSM_WIDTH), lambda b, i: (layer, 0, 0)),
                  pl.BlockSpec((None, SSM_WIDTH, SSM_WIDTH), lambda b, i: (layer, 0, 0))],
        out_specs=blk,
        out_shape=jax.ShapeDtypeStruct((nb, rows, SSM_WIDTH), BF16),
        compiler_params=_params(("parallel", "parallel"), 40),
        name="s5_glu",
    )(us, yf, yb, d_skip.reshape(DEPTH, 1, SSM_WIDTH), w_glu)


MERGE_SPLIT = 8


def _merge_kernel(s_ref, yq_ref, gs_ref, gf_ref, wps_ref, wpf_ref, o_ref):
    wps = wps_ref[...].astype(BF16)
    wpf = wpf_ref[...].astype(BF16)
    piece = o_ref.shape[0] // MERGE_SPLIT
    for r in range(MERGE_SPLIT):
        rs = slice(r * piece, (r + 1) * piece)
        ps = _dot(s_ref[rs, :], wps)
        pf = _dot(yq_ref[rs, :], wpf)
        o_ref[rs, :] = (jax.nn.sigmoid(gs_ref[rs, :].astype(F32)) * ps
                        + jax.nn.sigmoid(gf_ref[rs, :].astype(F32)) * pf).astype(BF16)


def _merge(s, yq, rest, w_ps, w_pf, layer):
    nb, rows, _ = s.shape
    tm = min(rows, 2048)
    tn = 512
    gs_off = FFT_WIDTH // tn
    gf_off = (FFT_WIDTH + D_MODEL) // tn
    row_blk = pl.BlockSpec((None, tm, SSM_WIDTH), lambda b, i, j: (b, i, 0))
    w_blk = pl.BlockSpec((None, SSM_WIDTH, tn), lambda b, i, j: (layer, 0, j))
    return pl.pallas_call(
        _merge_kernel,
        grid=(nb, rows // tm, D_MODEL // tn),
        in_specs=[
            row_blk, row_blk,
            pl.BlockSpec((None, tm, tn), lambda b, i, j: (b, i, gs_off + j)),
            pl.BlockSpec((None, tm, tn), lambda b, i, j: (b, i, gf_off + j)),
            w_blk, w_blk,
        ],
        out_specs=pl.BlockSpec((None, tm, tn), lambda b, i, j: (b, i, j)),
        out_shape=jax.ShapeDtypeStruct((nb, rows, D_MODEL), BF16),
        compiler_params=_params(("parallel", "parallel", "arbitrary"), 48),
        name="gated_merge",
    )(s, yq, rest, rest, w_ps, w_pf)


TAIL_TM = 1024
TAIL_TK = 512
TAIL_TF = 512


def _tail_kernel(layer, emit_next, merged_ref, x_hbm, wo_hbm, wu_hbm, wd_hbm, mod_ref, *refs):
    if emit_next:
        modn_ref, refs = refs[0], refs[1:]
    g1_ref, b1_ref, g2_ref, b2_ref, out_hbm = refs[:5]
    refs = refs[5:]
    if emit_next:
        hn_ref, refs = refs[0], refs[1:]
    xbuf, acc, h2, act, wa, wb, sem_a, sem_b, sem_x, sem_o = refs
    b = pl.program_id(0)
    i = pl.program_id(1)
    first = jnp.logical_and(b == 0, i == 0)
    last = jnp.logical_and(b == pl.num_programs(0) - 1, i == pl.num_programs(1) - 1)
    nk = D_MODEL // TAIL_TK
    nf = D_FF // TAIL_TF
    assert nk % 2 == 0
    rows = pl.ds(pl.multiple_of(i * TAIL_TM, TAIL_TM), TAIL_TM)

    def wo_copy(t, slot):
        return pltpu.make_async_copy(wo_hbm.at[layer, pl.ds(t * TAIL_TK, TAIL_TK), :], wa.at[slot], sem_a.at[slot])

    def wd_copy(f, slot):
        src = wd_hbm.at[layer, pl.ds(pl.multiple_of(f * TAIL_TK, TAIL_TK), TAIL_TK), :]
        return pltpu.make_async_copy(src, wa.at[slot], sem_a.at[slot])

    def wu_copy(f, slot):
        src = wu_hbm.at[layer, :, pl.ds(pl.multiple_of(f * TAIL_TF, TAIL_TF), TAIL_TF)]
        return pltpu.make_async_copy(src, wb.at[slot], sem_b.at[slot])

    def x_copy():
        return pltpu.make_async_copy(x_hbm.at[b, rows, :], xbuf, sem_x.at[0])

    def out_copy():
        return pltpu.make_async_copy(xbuf, out_hbm.at[b, rows, :], sem_o.at[0])

    def start_first_tiles():
        wo_copy(0, 0).start()
        wo_copy(1, 1).start()
        wu_copy(0, 0).start()
        wu_copy(1, 1).start()

    @pl.when(first)
    def _():
        start_first_tiles()

    for k in range(nk):
        slot = k % 2
        wo_copy(k, slot).wait()
        lhs = merged_ref[:, k * TAIL_TK:(k + 1) * TAIL_TK]
        if k == 0:
            acc[...] = _dot(lhs, wa[slot].astype(BF16))
        else:
            acc[...] += _dot(lhs, wa[slot].astype(BF16))
        if k + 2 < nk:
            wo_copy(k + 2, slot).start()
        else:
            wd_copy(k + 2 - nk, slot).start()
        if k == 0:
            @pl.when(jnp.logical_not(first))
            def _():
                out_copy().wait()

            x_copy().start()
    x_copy().wait()

    def finish1(rs):
        x1 = _deepnorm_ln(xbuf[rs, :], _row(mod_ref, GATE1), acc[rs, :]) * g1_ref[...] + b1_ref[...]
        xbuf[rs, :] = x1
        h2[rs, :] = _modulated_ln(x1, mod_ref, SHIFT2, SCALE2)

    _for_row_chunks(TAIL_TM, finish1)

    def up_tile(slot):
        a = jnp.maximum(_dot(h2[...], wb[slot].astype(BF16)), 0.0)
        act[slot] = (a * a).astype(BF16)

    def down_tile(slot, first=False):
        if first:
            acc[...] = _dot(act[slot], wa[slot].astype(BF16))
        else:
            acc[...] += _dot(act[slot], wa[slot].astype(BF16))

    def trip(f, slot):
        nxt = 1 - slot
        wu_copy(f + 1, nxt).wait()
        wd_copy(f, slot).wait()
        up_tile(nxt)
        down_tile(slot, first=isinstance(f, int) and f == 0)
        if isinstance(f, int):
            if f + 3 < nf:
                wu_copy(f + 3, nxt).start()
            if f + 2 < nf:
                wd_copy(f + 2, slot).start()
        else:
            @pl.when(f + 3 < nf)
            def _():
                wu_copy(f + 3, nxt).start()

            @pl.when(f + 2 < nf)
            def _():
                wd_copy(f + 2, slot).start()

    wu_copy(0, 0).wait()
    up_tile(0)
    wu_copy(2, 0).start()

    trip(0, 0)
    assert nf % 2 == 0

    def trip_pair(p, carry):
        trip(2 * p + 1, 1)
        trip(2 * p + 2, 0)
        return carry

    lax.fori_loop(0, (nf - 2) // 2, trip_pair, 0)

    def finish2(rs):
        x2 = _deepnorm_ln(xbuf[rs, :], _row(mod_ref, GATE2), acc[rs, :]) * g2_ref[...] + b2_ref[...]
        xbuf[rs, :] = x2
        if emit_next:
            hn_ref[rs, :] = _modulated_ln(x2, modn_ref, SHIFT1, SCALE1)

    wd_copy(nf - 1, (nf - 1) % 2).wait()
    down_tile((nf - 1) % 2)

    @pl.when(jnp.logical_not(last))
    def _():
        start_first_tiles()

    _for_row_chunks(TAIL_TM, finish2)
    out_copy().start()

    @pl.when(last)
    def _():
        out_copy().wait()


def _tail(merged, x, mod, w_o, w_up, w_down, g1, b1, g2, b2, layer, emit_next):
    nb, rows, _ = x.shape
    tm = TAIL_TM
    row_blk = pl.BlockSpec((None, tm, D_MODEL), lambda b_, i: (b_, i, 0))
    vec = pl.BlockSpec((None, 1, D_MODEL), lambda b_, i: (layer, 0, 0))
    hbm = pl.BlockSpec(memory_space=pl.ANY)
    in_specs = [row_blk, hbm, hbm, hbm, hbm, _mod_spec(layer, nb)]
    args = [merged, x, w_o, w_up, w_down, mod]
    if emit_next:
        in_specs.append(_mod_spec(layer + 1, nb))
        args.append(mod)
    in_specs += [vec, vec, vec, vec]
    args += [v.reshape(DEPTH, 1, D_MODEL) for v in (g1, b1, g2, b2)]
    out_specs = [hbm]
    out_shape = [jax.ShapeDtypeStruct(x.shape, F32)]
    if emit_next:
        out_specs.append(row_blk)
        out_shape.append(jax.ShapeDtypeStruct(x.shape, BF16))
    out = pl.pallas_call(
        functools.partial(_tail_kernel, layer, emit_next),
        grid=(nb, rows // tm),
        in_specs=in_specs,
        out_specs=out_specs,
        out_shape=out_shape,
        scratch_shapes=[
            pltpu.VMEM((tm, D_MODEL), F32),
            pltpu.VMEM((tm, D_MODEL), F32),
            pltpu.VMEM((tm, D_MODEL), BF16),
            pltpu.VMEM((2, tm, TAIL_TF), BF16),
            pltpu.VMEM((2, TAIL_TK, D_MODEL), F32),
            pltpu.VMEM((2, D_MODEL, TAIL_TF), F32),
            pltpu.SemaphoreType.DMA((2,)),
            pltpu.SemaphoreType.DMA((2,)),
            pltpu.SemaphoreType.DMA((1,)),
            pltpu.SemaphoreType.DMA((1,)),
        ],
        compiler_params=_params(("arbitrary", "arbitrary"), 58),
        name="layer_tail",
    )(*args)
    return out if emit_next else (out[0], None)


def _pos_table():
    quarter = D_MODEL // 4
    omega = 1.0 / (POS_BASE ** (np.arange(quarter, dtype=np.float64) / quarter))
    t = np.arange(SEQ)
    ar = (t // GRID_W).astype(np.float64)[:, None] * omega
    ac = (t % GRID_W).astype(np.float64)[:, None] * omega
    return np.concatenate([np.sin(ar), np.cos(ar), np.sin(ac), np.cos(ac)], axis=-1).astype(np.float32)


_POS = _pos_table()


def _mod_tables(m):
    m = m.reshape(DEPTH, SUBLANES, 6, D_MODEL)
    m = jnp.pad(m, ((0, 0), (0, 0), (0, SUBLANES - 6), (0, 0)))
    return m[:, :BATCH], m[:, BATCH:BATCH + 1]


def _as_batch(a):
    return a.reshape(BATCH, CTX_LEN, a.shape[-1])


def _as_slab(a):
    return a.reshape(1, BATCH * CTX_LEN, a.shape[-1])


def kernel(x, c, ctx, c_ctx, w_mod, b_mod, w_in, lam_re, lam_im, log_step, ssm_b_re, ssm_b_im, ssm_c_re,
           ssm_c_im, d_skip, w_glu, w_ps, w_pf, w_o, ln1_g, ln1_b, w_up, w_down, ln2_g, ln2_b):
    cond8 = jnp.concatenate([c, c_ctx[None], jnp.zeros((SUBLANES - BATCH - 1, D_MODEL), F32)], axis=0)
    mod_lat, mod_ctx = _mod_tables(_modulation(cond8, w_mod, b_mod))

    x_lat, h_lat = _entry(x, jnp.asarray(_POS), mod_lat)
    x_ctx, h_ctx = _entry(_as_slab(ctx), None, mod_ctx)
    h_zero = jnp.zeros((2, SUBLANES, STATE_W), F32)
    prm = jax.vmap(_s5_params)(lam_re, lam_im, log_step, ssm_b_re, ssm_b_im, ssm_c_re, ssm_c_im)

    for l in range(DEPTH):
        need_ctx = l < DEPTH - 1

        def mixer_tail(us, rest, yf, yb, xx, mod, as_batch, as_rows):
            p, q = _channel_dft(rest)
            yq = as_rows(_position_dft(as_batch(p), as_batch(q)))
            s = _glu(us, as_rows(yf), as_rows(yb), d_skip, w_glu, l)
            merged = _merge(s, yq, rest, w_ps, w_pf, l)
            return _tail(merged, xx, mod, w_o, w_up, w_down, ln1_g, ln1_b, ln2_g, ln2_b, l, need_ctx)

        us_lat, rest_lat = _in_proj(h_lat, w_in, l, True)
        if need_ctx:
            us_ctx, rest_ctx = _in_proj(h_ctx, w_in, l, True)
            yf_c, yb_c, h_t = _s5(_as_batch(us_ctx), h_zero, prm, l, True)
        else:
            (us_ctx,) = _in_proj(h_ctx, w_in, l, False)
            (h_t,) = _s5(_as_batch(us_ctx), h_zero, prm, l, False)
        yf, yb, _ = _s5(us_lat, h_t, prm, l, True)
        ident = lambda a: a
        x_lat, h_lat = mixer_tail(us_lat, rest_lat, yf, yb, x_lat, mod_lat, ident, ident)
        if need_ctx:
            x_ctx, h_ctx = mixer_tail(us_ctx, rest_ctx, yf_c, yb_c, x_ctx, mod_ctx, _as_batch, _as_slab)

    return x_lat
```

```python
import functools
import math

import numpy as np
import jax
import jax.numpy as jnp
from jax import lax
from jax.experimental import pallas as pl
from jax.experimental.pallas import tpu as pltpu

D_MODEL = 2048
BATCH = 4
SEQ = 2048
DEPTH = 2
GRID_W = 64
CTX_LEN = 256
SSM_WIDTH = D_MODEL // 2
SSM_GROUP = 16
SSM_GROUPS = SSM_WIDTH // SSM_GROUP
SSM_STATE = 64
FFT_WIDTH = D_MODEL - SSM_WIDTH
FFT_GROUPS = 4
FFT_GROUP = FFT_WIDTH // FFT_GROUPS
IN_WIDTH = SSM_WIDTH + FFT_WIDTH + 2 * D_MODEL
REST_WIDTH = IN_WIDTH - SSM_WIDTH
D_FF = 4 * D_MODEL
ALPHA = (2 * DEPTH) ** 0.25
LN_EPS = 1e-5
POS_BASE = 10000.0

F32 = jnp.float32
BF16 = jnp.bfloat16

SUBLANES = 8
LANES = 128
STATE_W = SSM_GROUPS * SSM_STATE
SCAN_T = 64
S5_KGROUPS = 8
S5_NK = SSM_GROUPS // S5_KGROUPS
S5_KSTATE = S5_KGROUPS * SSM_STATE
MIB = 1024 * 1024

SHIFT1, SCALE1, GATE1, SHIFT2, SCALE2, GATE2 = range(6)


def _params(sem, vmem_mib):
    return pltpu.CompilerParams(dimension_semantics=sem, vmem_limit_bytes=vmem_mib * MIB)


def _dot(a, b):
    return jnp.dot(a, b, preferred_element_type=F32)


def _ln(x, eps=LN_EPS):
    mu = jnp.mean(x, axis=-1, keepdims=True)
    xc = x - mu
    var = jnp.mean(xc * xc, axis=-1, keepdims=True)
    return xc * lax.rsqrt(var + eps)


def _deepnorm_ln(x, gate, branch):
    return _ln(x + (gate * (1.0 / ALPHA)) * branch, LN_EPS / (ALPHA * ALPHA))


def _row(ref, j):
    return ref[j:j + 1, :]


def _modulated_ln(x, mod_ref, shift, scale):
    return (_ln(x) * (1.0 + _row(mod_ref, scale)) + _row(mod_ref, shift)).astype(BF16)


EPILOGUE_ROWS = 128


def _for_row_chunks(rows, fn):
    def body(r, carry):
        fn(pl.ds(pl.multiple_of(r * EPILOGUE_ROWS, EPILOGUE_ROWS), EPILOGUE_ROWS))
        return carry

    lax.fori_loop(0, rows // EPILOGUE_ROWS, body, 0)


def _mod_spec(layer, nb):
    if nb == 1:
        return pl.BlockSpec((None, None, SUBLANES, D_MODEL), lambda b, *_: (layer, 0, 0, 0))
    return pl.BlockSpec((None, None, SUBLANES, D_MODEL), lambda b, *_: (layer, b, 0, 0))


def _mod_kernel(c_ref, w_ref, b_ref, o_ref):
    c = c_ref[...]
    sc = c * jax.nn.sigmoid(c)
    o_ref[...] = _dot(sc.astype(BF16), w_ref[...].astype(BF16)) + b_ref[...]


def _modulation(cond8, w_mod, b_mod):
    tn = 1024
    n = 6 * D_MODEL
    return pl.pallas_call(
        _mod_kernel,
        grid=(DEPTH, n // tn),
        in_specs=[
            pl.BlockSpec((SUBLANES, D_MODEL), lambda l, j: (0, 0)),
            pl.BlockSpec((None, D_MODEL, tn), lambda l, j: (l, 0, j)),
            pl.BlockSpec((None, 1, tn), lambda l, j: (l, 0, j)),
        ],
        out_specs=pl.BlockSpec((None, SUBLANES, tn), lambda l, j: (l, 0, j)),
        out_shape=jax.ShapeDtypeStruct((DEPTH, SUBLANES, n), F32),
        compiler_params=_params(("parallel", "parallel"), 40),
        name="adaln_modulation",
    )(cond8, w_mod, b_mod.reshape(DEPTH, 1, n))


def _entry_kernel(has_pos, *refs):
    if has_pos:
        x_ref, p_ref, mod_ref, xo_ref, h_ref = refs
        x = x_ref[...] + p_ref[...]
        xo_ref[...] = x
    else:
        x_ref, mod_ref, h_ref = refs
        x = x_ref[...]
    h_ref[...] = _modulated_ln(x, mod_ref, SHIFT1, SCALE1)


def _entry(x, pos, mod):
    nb, rows, _ = x.shape
    tm = 512
    has_pos = pos is not None
    blk = pl.BlockSpec((None, tm, D_MODEL), lambda i, b: (b, i, 0))
    in_specs = [blk]
    args = [x]
    if has_pos:
        in_specs.append(pl.BlockSpec((tm, D_MODEL), lambda i, b: (i, 0)))
        args.append(pos)
    in_specs.append(pl.BlockSpec((None, None, SUBLANES, D_MODEL), lambda i, b: (0, b, 0, 0)))
    args.append(mod)
    h_shape = jax.ShapeDtypeStruct(x.shape, BF16)
    out = pl.pallas_call(
        functools.partial(_entry_kernel, has_pos),
        grid=(rows // tm, nb),
        in_specs=in_specs,
        out_specs=[blk, blk] if has_pos else [blk],
        out_shape=[jax.ShapeDtypeStruct(x.shape, F32), h_shape] if has_pos else [h_shape],
        compiler_params=_params(("parallel", "parallel"), 40),
        name="entry_ln",
    )(*args)
    return out if has_pos else (x, out[0])


ROW_SPLIT = 2


def _win_kernel(n_us, with_rest, h_ref, w_ref, us_ref, *rest):
    piece = h_ref.shape[0] // ROW_SPLIT

    def project(o_ref):
        w = w_ref[...].astype(BF16)
        for r in range(ROW_SPLIT):
            rs = slice(r * piece, (r + 1) * piece)
            o_ref[rs, :] = _dot(h_ref[rs, :], w).astype(o_ref.dtype)

    if not with_rest:
        project(us_ref)
        return
    rest_ref, = rest
    j = pl.program_id(2)

    @pl.when(j < n_us)
    def _():
        project(us_ref)

    @pl.when(j >= n_us)
    def _():
        project(rest_ref)


def _in_proj(h, w_in, layer, with_rest):
    nb, rows, _ = h.shape
    tm = min(rows, 2048)
    tn = 512
    n_us = SSM_WIDTH // tn
    n_tiles = (IN_WIDTH if with_rest else SSM_WIDTH) // tn
    out_specs = [pl.BlockSpec((None, tm, tn), lambda b, i, j: (b, i, jnp.minimum(j, n_us - 1)))]
    out_shape = [jax.ShapeDtypeStruct((nb, rows, SSM_WIDTH), F32)]
    if with_rest:
        out_specs.append(pl.BlockSpec((None, tm, tn), lambda b, i, j: (b, i, jnp.maximum(j - n_us, 0))))
        out_shape.append(jax.ShapeDtypeStruct((nb, rows, REST_WIDTH), BF16))
    return pl.pallas_call(
        functools.partial(_win_kernel, n_us, with_rest),
        grid=(nb, rows // tm, n_tiles),
        in_specs=[
            pl.BlockSpec((None, tm, D_MODEL), lambda b, i, j: (b, i, 0)),
            pl.BlockSpec((None, D_MODEL, tn), lambda b, i, j: (layer, 0, j)),
        ],
        out_specs=out_specs,
        out_shape=out_shape,
        compiler_params=_params(("parallel", "parallel", "arbitrary"), 48),
        name="in_proj",
    )(h, w_in)


def _s5_kernel(need_y, uf_ref, ub_ref, h0_ref, pf_ref, pb_ref, pft_ref, pbt_ref, a_ref, wd_ref, wr_ref,
               *rest):
    if need_y:
        yf_ref, yb_ref, ht_ref, v_ref, hc_ref, h_ref, ycf_ref, ycb_ref = rest
    else:
        ht_ref, v_ref, hc_ref = rest
    g = pl.program_id(0)
    rows_in = BATCH * SCAN_T
    pair = 2 * SUBLANES

    @pl.when(g == 0)
    def _():
        hc_ref[...] = h0_ref[...]

    uf = uf_ref[...].reshape(rows_in, SSM_WIDTH).astype(BF16)
    ub = ub_ref[...].reshape(rows_in, SSM_WIDTH).astype(BF16)
    up_f = _dot(pf_ref[...], uf).astype(BF16)
    up_b = _dot(pb_ref[...], ub).astype(BF16)

    def slab_cols(k):
        return slice(2 * k * S5_KSTATE, (2 * k + 2) * S5_KSTATE)

    def drive(k):
        cs = slice(k * LANES, (k + 1) * LANES)
        lhs = jnp.concatenate([up_f[:, cs], up_b[:, cs]], axis=1)
        v_ref[:, slab_cols(k)] = _dot(lhs, wd_ref[k])

    def scan(k):
        ss = slice(k * S5_KSTATE, (k + 1) * S5_KSTATE)
        re_cols = slice(2 * k * S5_KSTATE, (2 * k + 1) * S5_KSTATE)
        im_cols = slice((2 * k + 1) * S5_KSTATE, (2 * k + 2) * S5_KSTATE)
        ar = a_ref[0, :, ss]
        ai = a_ref[1, :, ss]
        hr = hc_ref[0, :, ss]
        hi = hc_ref[1, :, ss]
        for s2 in range(SCAN_T // 2):
            rows = []
            for s in (2 * s2, 2 * s2 + 1):
                rs = slice(s * SUBLANES, (s + 1) * SUBLANES)
                nr = ar * hr - ai * hi + v_ref[rs, re_cols]
                ni = ar * hi + ai * hr + v_ref[rs, im_cols]
                hr, hi = nr, ni
                rows.append((nr, ni))
            if need_y:
                ps = slice(s2 * pair, (s2 + 1) * pair)
                h_ref[ps, re_cols] = jnp.concatenate([rows[0][0], rows[1][0]], axis=0).astype(BF16)
                h_ref[ps, im_cols] = jnp.concatenate([rows[0][1], rows[1][1]], axis=0).astype(BF16)
        hc_ref[0, :, ss] = hr
        hc_ref[1, :, ss] = hi

    def readout(k):
        cs = slice(k * LANES, (k + 1) * LANES)
        y = _dot(h_ref[:, slab_cols(k)], wr_ref[k])
        ycf_ref[:, cs] = y[:, :LANES]
        ycb_ref[:, cs] = y[:, LANES:]

    lag = 1
    for k in range(S5_NK + lag):
        if k < S5_NK:
            drive(k)
            scan(k)
        if need_y and k >= lag:
            readout(k - lag)
    if need_y:
        yf = _dot(pft_ref[...], ycf_ref[...].astype(BF16))
        yb = _dot(pbt_ref[...], ycb_ref[...].astype(BF16))
        yf_ref[...] = yf.astype(BF16).reshape(BATCH, SCAN_T, SSM_WIDTH)
        yb_ref[...] = yb.astype(BF16).reshape(BATCH, SCAN_T, SSM_WIDTH)

    @pl.when(g == pl.num_programs(0) - 1)
    def _():
        ht_ref[...] = hc_ref[...]


def _scan_perms():
    t = SCAN_T
    pf = np.zeros((2 * BATCH * t, BATCH * t), np.float32)
    pb = np.zeros((2 * BATCH * t, BATCH * t), np.float32)
    for s in range(t):
        for b in range(BATCH):
            pf[s * 2 * BATCH + b, b * t + s] = 1.0
            pb[s * 2 * BATCH + BATCH + b, b * t + (t - 1 - s)] = 1.0
    return pf, pb


_PF, _PB = _scan_perms()


def _s5(u, h0, prm, layer, need_y):
    a, wd, wr = prm
    steps = u.shape[1]
    n = steps // SCAN_T
    rows_in = BATCH * SCAN_T
    rows_sc = 2 * rows_in
    pf = jnp.asarray(_PF, BF16)
    pb = jnp.asarray(_PB, BF16)
    const2 = lambda g: (0, 0)
    once = pl.Buffered(1)
    blk = (BATCH, SCAN_T, SSM_WIDTH)
    fwd_blk = pl.BlockSpec(blk, lambda g: (0, g, 0))
    bwd_blk = pl.BlockSpec(blk, lambda g: (0, n - 1 - g, 0))
    state_spec = pl.BlockSpec((2, SUBLANES, STATE_W), lambda g: (0, 0, 0))
    in_specs = [
        fwd_blk, bwd_blk, state_spec,
        pl.BlockSpec((rows_sc, rows_in), const2),
        pl.BlockSpec((rows_sc, rows_in), const2),
        pl.BlockSpec((rows_in, rows_sc), const2),
        pl.BlockSpec((rows_in, rows_sc), const2),
        pl.BlockSpec((None, 2, SUBLANES, STATE_W), lambda g: (layer, 0, 0, 0)),
        pl.BlockSpec((None, S5_NK, 2 * LANES, 2 * S5_KSTATE), lambda g: (layer, 0, 0, 0), pipeline_mode=once),
        pl.BlockSpec((None, S5_NK, 2 * S5_KSTATE, 2 * LANES), lambda g: (layer, 0, 0, 0), pipeline_mode=once),
    ]
    state_shape = jax.ShapeDtypeStruct((2, SUBLANES, STATE_W), F32)
    scratch = [pltpu.VMEM((rows_sc, 2 * STATE_W), F32), pltpu.VMEM((2, SUBLANES, STATE_W), F32)]
    if need_y:
        out_specs = [fwd_blk, bwd_blk, state_spec]
        y_shape = jax.ShapeDtypeStruct((BATCH, steps, SSM_WIDTH), BF16)
        out_shape = [y_shape, y_shape, state_shape]
        scratch += [pltpu.VMEM((rows_sc, 2 * STATE_W), BF16), pltpu.VMEM((rows_sc, SSM_WIDTH), F32),
                    pltpu.VMEM((rows_sc, SSM_WIDTH), F32)]
    else:
        out_specs = [state_spec]
        out_shape = [state_shape]
    return pl.pallas_call(
        functools.partial(_s5_kernel, need_y),
        grid=(n,),
        in_specs=in_specs,
        out_specs=out_specs,
        out_shape=out_shape,
        scratch_shapes=scratch,
        compiler_params=_params(("arbitrary",), 52),
        name="s5_scan",
    )(u, u, h0, pf, pb, pf.T, pb.T, a, wd, wr)


def _s5_params(lam_re, lam_im, log_step, b_re, b_im, c_re, c_im):
    dt = jnp.exp(log_step)[..., None]
    mag = jnp.exp(lam_re * dt)
    ang = lam_im * dt
    abar_re, abar_im = mag * jnp.cos(ang), mag * jnp.sin(ang)
    den = lam_re * lam_re + lam_im * lam_im
    nr, ni = abar_re - 1.0, abar_im
    coef_re = (nr * lam_re + ni * lam_im) / den
    coef_im = (ni * lam_re - nr * lam_im) / den
    bb_re = coef_re[..., None] * b_re - coef_im[..., None] * b_im
    bb_im = coef_re[..., None] * b_im + coef_im[..., None] * b_re
    bb = jnp.stack([bb_re, bb_im])
    bb = jnp.swapaxes(bb, -1, -2).reshape(-1, SSM_STATE)
    own = ((np.arange(bb.shape[0]) // SSM_GROUP) % S5_KGROUPS)[:, None] == (
        np.arange(S5_KSTATE) // SSM_STATE)[None, :]
    bb = jnp.where(own, jnp.tile(bb, (1, S5_KGROUPS)), 0.0)
    bb = bb.reshape(2, 2, S5_NK, LANES, S5_KSTATE)
    wd = jnp.concatenate([bb[0], bb[1]], axis=-1)
    wd = jnp.transpose(wd, (1, 0, 2, 3)).reshape(S5_NK, 2 * LANES, 2 * S5_KSTATE).astype(BF16)
    cc = jnp.stack([c_re, -c_im])
    cc = jnp.swapaxes(cc, -1, -2).reshape(-1, SSM_GROUP)
    own = ((np.arange(cc.shape[0]) // SSM_STATE) % S5_KGROUPS)[:, None] == (
        np.arange(LANES) // SSM_GROUP)[None, :]
    cc = jnp.where(own, jnp.tile(cc, (1, S5_KGROUPS)), 0.0)
    cc = cc.reshape(2, 2, S5_NK, S5_KSTATE, LANES)
    wr = jnp.concatenate([cc[:, 0], cc[:, 1]], axis=-1)
    wr = jnp.transpose(wr, (1, 0, 2, 3)).reshape(S5_NK, 2 * S5_KSTATE, 2 * LANES).astype(BF16)
    a = jnp.stack([abar_re.reshape(2, STATE_W), abar_im.reshape(2, STATE_W)])
    a = jnp.repeat(a, BATCH, axis=1)
    return a, wd, wr


def _dft_tables(n):
    j = np.arange(n, dtype=np.int64)
    ang = 2.0 * np.pi * ((j[:, None] * j[None, :]) % n).astype(np.float64) / n
    s = 1.0 / math.sqrt(n)
    return (np.cos(ang) * s).astype(np.float32), (np.sin(ang) * s).astype(np.float32)


_CH_COS, _CH_SIN = _dft_tables(FFT_GROUP)
_CH_CS = np.concatenate([_CH_COS, _CH_SIN], axis=1)


DFT_RADIX = 4
DFT_ILV = 256
_QUARTER_TURN = ((1, 0), (0, 1), (-1, 0), (0, -1))


def _radix_tables(n):
    nq = n // DFT_RADIX
    k = np.arange(nq, dtype=np.int64)
    s = 1.0 / math.sqrt(n)
    cos, sin = [], []
    for r in range(DFT_RADIX):
        j = DFT_RADIX * k + r
        ang = 2.0 * np.pi * ((j[:, None] * k[None, :]) % n).astype(np.float64) / n
        cos.append(np.cos(ang) * s)
        sin.append(np.sin(ang) * s)
    return np.stack(cos).astype(np.float32), np.stack(sin).astype(np.float32)


def _interleave_perm():
    per = DFT_ILV // DFT_RADIX
    perm = np.zeros((DFT_ILV, DFT_ILV), np.float32)
    for r in range(DFT_RADIX):
        for m in range(per):
            perm[DFT_RADIX * m + r, r * per + m] = 1.0
    return perm


_RADIX_TABLES = {n: _radix_tables(n) for n in (CTX_LEN, SEQ)}
_ILV_PERM = _interleave_perm()


def _fnet_kernel(n, u_ref, w_ref, c_ref, s_ref, perm_ref, o_ref, p_ref, q_ref, comb_ref, y_ref):
    nq = n // DFT_RADIX
    chunk = min(nq, EPILOGUE_ROWS)

    w = w_ref[...].astype(BF16)
    for g in range(FFT_GROUPS):
        cs = slice(g * FFT_GROUP, (g + 1) * FFT_GROUP)
        r = _dot(u_ref[:, cs], w)
        p_ref[:, cs] = r[:, :FFT_GROUP].astype(BF16)
        q_ref[:, cs] = r[:, FFT_GROUP:].astype(BF16)

    def combine(i, carry):
        rs = pl.ds(pl.multiple_of(i * chunk, chunk), chunk)
        p = [p_ref[pl.ds(pl.multiple_of(q * nq + i * chunk, chunk), chunk), :].astype(F32)
             for q in range(DFT_RADIX)]
        qq = [q_ref[pl.ds(pl.multiple_of(q * nq + i * chunk, chunk), chunk), :].astype(F32)
              for q in range(DFT_RADIX)]
        def signed_sum(terms):
            acc = None
            for sign, v in terms:
                if acc is None:
                    acc = v if sign > 0 else -v
                else:
                    acc = acc + v if sign > 0 else acc - v
            return acc

        for r in range(DFT_RADIX):
            pr, qr = [], []
            for q in range(DFT_RADIX):
                cs, sn = _QUARTER_TURN[(r * q) % DFT_RADIX]
                if cs:
                    pr.append((cs, p[q]))
                    qr.append((cs, qq[q]))
                if sn:
                    pr.append((-sn, qq[q]))
                    qr.append((sn, p[q]))
            comb_ref[2 * r, rs, :] = signed_sum(pr).astype(BF16)
            comb_ref[2 * r + 1, rs, :] = signed_sum(qr).astype(BF16)
        return carry

    lax.fori_loop(0, nq // chunk, combine, 0)
    for r in range(DFT_RADIX):
        y_ref[r] = (_dot(c_ref[r].astype(BF16), comb_ref[2 * r])
                    - _dot(s_ref[r].astype(BF16), comb_ref[2 * r + 1]))
    per = DFT_ILV // DFT_RADIX
    for blk in range(n // DFT_ILV):
        slab = jnp.concatenate([y_ref[r, blk * per:(blk + 1) * per, :] for r in range(DFT_RADIX)], axis=0)
        o_ref[blk * DFT_ILV:(blk + 1) * DFT_ILV, :] = _dot(perm_ref[...], slab.astype(BF16)).astype(BF16)


def _fourier_mix(rest):
    nb, steps, _ = rest.shape
    cos, sin = _RADIX_TABLES[steps]
    nq = steps // DFT_RADIX
    x_blk = pl.BlockSpec((None, steps, FFT_WIDTH), lambda b: (b, 0, 0))
    t_blk = pl.BlockSpec((DFT_RADIX, nq, nq), lambda b: (0, 0, 0), pipeline_mode=pl.Buffered(1))
    return pl.pallas_call(
        functools.partial(_fnet_kernel, steps),
        grid=(nb,),
        in_specs=[x_blk, pl.BlockSpec((FFT_GROUP, 2 * FFT_GROUP), lambda b: (0, 0)), t_blk, t_blk,
                  pl.BlockSpec((DFT_ILV, DFT_ILV), lambda b: (0, 0))],
        out_specs=x_blk,
        out_shape=jax.ShapeDtypeStruct((nb, steps, FFT_WIDTH), BF16),
        scratch_shapes=[pltpu.VMEM((steps, FFT_WIDTH), BF16), pltpu.VMEM((steps, FFT_WIDTH), BF16),
                        pltpu.VMEM((2 * DFT_RADIX, nq, FFT_WIDTH), BF16),
                        pltpu.VMEM((DFT_RADIX, nq, FFT_WIDTH), F32)],
        compiler_params=_params(("parallel",), 56),
        name="fourier_mix",
    )(rest, jnp.asarray(_CH_CS), jnp.asarray(cos), jnp.asarray(sin), jnp.asarray(_ILV_PERM, BF16))


def _glu_kernel(us_ref, yf_ref, yb_ref, dsk_ref, w_ref, o_ref):
    ys = dsk_ref[...] * us_ref[...] + yf_ref[...].astype(F32) + yb_ref[...].astype(F32)
    g = jax.nn.gelu(ys)
    z = _dot(g.astype(BF16), w_ref[...].astype(BF16))
    o_ref[...] = (g * jax.nn.sigmoid(z)).astype(BF16)


def _glu(us, yf, yb, d_skip, w_glu, layer):
    nb, rows, _ = us.shape
    tm = 1024
    blk = pl.BlockSpec((None, tm, SSM_WIDTH), lambda b, i: (b, i, 0))
    return pl.pallas_call(
        _glu_kernel,
        grid=(nb, rows // tm),
        in_specs=[blk, blk, blk,
                  pl.BlockSpec((None, 1, SSM_WIDTH), lambda b, i: (layer, 0, 0)),
                  pl.BlockSpec((None, SSM_WIDTH, SSM_WIDTH), lambda b, i: (layer, 0, 0))],
        out_specs=blk,
        out_shape=jax.ShapeDtypeStruct((nb, rows, SSM_WIDTH), BF16),
        compiler_params=_params(("parallel", "parallel"), 40),
        name="s5_glu",
    )(us, yf, yb, d_skip.reshape(DEPTH, 1, SSM_WIDTH), w_glu)


MERGE_SPLIT = 8


def _merge_kernel(s_ref, yq_ref, gs_ref, gf_ref, wps_ref, wpf_ref, o_ref):
    wps = wps_ref[...].astype(BF16)
    wpf = wpf_ref[...].astype(BF16)
    piece = o_ref.shape[0] // MERGE_SPLIT
    for r in range(MERGE_SPLIT):
        rs = slice(r * piece, (r + 1) * piece)
        ps = _dot(s_ref[rs, :], wps)
        pf = _dot(yq_ref[rs, :], wpf)
        o_ref[rs, :] = (jax.nn.sigmoid(gs_ref[rs, :].astype(F32)) * ps
                        + jax.nn.sigmoid(gf_ref[rs, :].astype(F32)) * pf).astype(BF16)


def _merge(s, yq, rest, w_ps, w_pf, layer):
    nb, rows, _ = s.shape
    tm = min(rows, 2048)
    tn = 512
    gs_off = FFT_WIDTH // tn
    gf_off = (FFT_WIDTH + D_MODEL) // tn
    row_blk = pl.BlockSpec((None, tm, SSM_WIDTH), lambda b, i, j: (b, i, 0))
    w_blk = pl.BlockSpec((None, SSM_WIDTH, tn), lambda b, i, j: (layer, 0, j))
    return pl.pallas_call(
        _merge_kernel,
        grid=(nb, rows // tm, D_MODEL // tn),
        in_specs=[
            row_blk, row_blk,
            pl.BlockSpec((None, tm, tn), lambda b, i, j: (b, i, gs_off + j)),
            pl.BlockSpec((None, tm, tn), lambda b, i, j: (b, i, gf_off + j)),
            w_blk, w_blk,
        ],
        out_specs=pl.BlockSpec((None, tm, tn), lambda b, i, j: (b, i, j)),
        out_shape=jax.ShapeDtypeStruct((nb, rows, D_MODEL), BF16),
        compiler_params=_params(("parallel", "parallel", "arbitrary"), 48),
        name="gated_merge",
    )(s, yq, rest, rest, w_ps, w_pf)


TAIL_TM = 1024
TAIL_TK = 512
TAIL_TF = 512


def _tail_kernel(layer, emit_next, merged_ref, x_hbm, wo_hbm, wu_hbm, wd_hbm, mod_ref, *refs):
    if emit_next:
        modn_ref, refs = refs[0], refs[1:]
    g1_ref, b1_ref, g2_ref, b2_ref, out_hbm = refs[:5]
    refs = refs[5:]
    if emit_next:
        hn_ref, refs = refs[0], refs[1:]
    xbuf, acc, h2, act, wa, wb, sem_a, sem_b, sem_x, sem_o = refs
    b = pl.program_id(0)
    i = pl.program_id(1)
    first = jnp.logical_and(b == 0, i == 0)
    last = jnp.logical_and(b == pl.num_programs(0) - 1, i == pl.num_programs(1) - 1)
    nk = D_MODEL // TAIL_TK
    nf = D_FF // TAIL_TF
    assert nk % 2 == 0
    rows = pl.ds(pl.multiple_of(i * TAIL_TM, TAIL_TM), TAIL_TM)

    def wo_copy(t, slot):
        return pltpu.make_async_copy(wo_hbm.at[layer, pl.ds(t * TAIL_TK, TAIL_TK), :], wa.at[slot], sem_a.at[slot])

    def wd_copy(f, slot):
        src = wd_hbm.at[layer, pl.ds(pl.multiple_of(f * TAIL_TK, TAIL_TK), TAIL_TK), :]
        return pltpu.make_async_copy(src, wa.at[slot], sem_a.at[slot])

    def wu_copy(f, slot):
        src = wu_hbm.at[layer, :, pl.ds(pl.multiple_of(f * TAIL_TF, TAIL_TF), TAIL_TF)]
        return pltpu.make_async_copy(src, wb.at[slot], sem_b.at[slot])

    def x_copy():
        return pltpu.make_async_copy(x_hbm.at[b, rows, :], xbuf, sem_x.at[0])

    def out_copy():
        return pltpu.make_async_copy(xbuf, out_hbm.at[b, rows, :], sem_o.at[0])

    def start_first_tiles():
        wo_copy(0, 0).start()
        wo_copy(1, 1).start()
        wu_copy(0, 0).start()
        wu_copy(1, 1).start()

    @pl.when(first)
    def _():
        start_first_tiles()

    for k in range(nk):
        slot = k % 2
        wo_copy(k, slot).wait()
        lhs = merged_ref[:, k * TAIL_TK:(k + 1) * TAIL_TK]
        if k == 0:
            acc[...] = _dot(lhs, wa[slot].astype(BF16))
        else:
            acc[...] += _dot(lhs, wa[slot].astype(BF16))
        if k + 2 < nk:
            wo_copy(k + 2, slot).start()
        else:
            wd_copy(k + 2 - nk, slot).start()
        if k == 1:
            @pl.when(jnp.logical_not(first))
            def _():
                out_copy().wait()

            x_copy().start()
    x_copy().wait()

    def finish1(rs):
        x1 = _deepnorm_ln(xbuf[rs, :], _row(mod_ref, GATE1), acc[rs, :]) * g1_ref[...] + b1_ref[...]
        xbuf[rs, :] = x1
        h2[rs, :] = _modulated_ln(x1, mod_ref, SHIFT2, SCALE2)

    _for_row_chunks(TAIL_TM, finish1)

    def up_tile(slot):
        a = jnp.maximum(_dot(h2[...], wb[slot].astype(BF16)), 0.0)
        act[slot] = (a * a).astype(BF16)

    def down_tile(slot, first=False):
        if first:
            acc[...] = _dot(act[slot], wa[slot].astype(BF16))
        else:
            acc[...] += _dot(act[slot], wa[slot].astype(BF16))

    def trip(f, slot):
        nxt = 1 - slot
        wu_copy(f + 1, nxt).wait()
        wd_copy(f, slot).wait()
        up_tile(nxt)
        down_tile(slot, first=isinstance(f, int) and f == 0)
        if isinstance(f, int):
            if f + 3 < nf:
                wu_copy(f + 3, nxt).start()
            if f + 2 < nf:
                wd_copy(f + 2, slot).start()
        else:
            @pl.when(f + 3 < nf)
            def _():
                wu_copy(f + 3, nxt).start()

            @pl.when(f + 2 < nf)
            def _():
                wd_copy(f + 2, slot).start()

    wu_copy(0, 0).wait()
    up_tile(0)
    wu_copy(2, 0).start()

    trip(0, 0)
    assert nf % 2 == 0

    def trip_pair(p, carry):
        trip(2 * p + 1, 1)
        trip(2 * p + 2, 0)
        return carry

    lax.fori_loop(0, (nf - 2) // 2, trip_pair, 0)

    def finish2(rs):
        x2 = _deepnorm_ln(xbuf[rs, :], _row(mod_ref, GATE2), acc[rs, :]) * g2_ref[...] + b2_ref[...]
        xbuf[rs, :] = x2
        if emit_next:
            hn_ref[rs, :] = _modulated_ln(x2, modn_ref, SHIFT1, SCALE1)

    wd_copy(nf - 1, (nf - 1) % 2).wait()
    down_tile((nf - 1) % 2)

    @pl.when(jnp.logical_not(last))
    def _():
        start_first_tiles()

    _for_row_chunks(TAIL_TM, finish2)
    out_copy().start()

    @pl.when(last)
    def _():
        out_copy().wait()


def _tail(merged, x, mod, w_o, w_up, w_down, g1, b1, g2, b2, layer, emit_next):
    nb, rows, _ = x.shape
    tm = TAIL_TM
    row_blk = pl.BlockSpec((None, tm, D_MODEL), lambda b_, i: (b_, i, 0))
    vec = pl.BlockSpec((None, 1, D_MODEL), lambda b_, i: (layer, 0, 0))
    hbm = pl.BlockSpec(memory_space=pl.ANY)
    in_specs = [row_blk, hbm, hbm, hbm, hbm, _mod_spec(layer, nb)]
    args = [merged, x, w_o, w_up, w_down, mod]
    if emit_next:
        in_specs.append(_mod_spec(layer + 1, nb))
        args.append(mod)
    in_specs += [vec, vec, vec, vec]
    args += [v.reshape(DEPTH, 1, D_MODEL) for v in (g1, b1, g2, b2)]
    out_specs = [hbm]
    out_shape = [jax.ShapeDtypeStruct(x.shape, F32)]
    if emit_next:
        out_specs.append(row_blk)
        out_shape.append(jax.ShapeDtypeStruct(x.shape, BF16))
    out = pl.pallas_call(
        functools.partial(_tail_kernel, layer, emit_next),
        grid=(nb, rows // tm),
        in_specs=in_specs,
        out_specs=out_specs,
        out_shape=out_shape,
        scratch_shapes=[
            pltpu.VMEM((tm, D_MODEL), F32),
            pltpu.VMEM((tm, D_MODEL), F32),
            pltpu.VMEM((tm, D_MODEL), BF16),
            pltpu.VMEM((2, tm, TAIL_TF), BF16),
            pltpu.VMEM((2, TAIL_TK, D_MODEL), F32),
            pltpu.VMEM((2, D_MODEL, TAIL_TF), F32),
            pltpu.SemaphoreType.DMA((2,)),
            pltpu.SemaphoreType.DMA((2,)),
            pltpu.SemaphoreType.DMA((1,)),
            pltpu.SemaphoreType.DMA((1,)),
        ],
        compiler_params=_params(("arbitrary", "arbitrary"), 58),
        name="layer_tail",
    )(*args)
    return out if emit_next else (out[0], None)


def _pos_table():
    quarter = D_MODEL // 4
    omega = 1.0 / (POS_BASE ** (np.arange(quarter, dtype=np.float64) / quarter))
    t = np.arange(SEQ)
    ar = (t // GRID_W).astype(np.float64)[:, None] * omega
    ac = (t % GRID_W).astype(np.float64)[:, None] * omega
    return np.concatenate([np.sin(ar), np.cos(ar), np.sin(ac), np.cos(ac)], axis=-1).astype(np.float32)


_POS = _pos_table()


def _mod_tables(m):
    m = m.reshape(DEPTH, SUBLANES, 6, D_MODEL)
    m = jnp.pad(m, ((0, 0), (0, 0), (0, SUBLANES - 6), (0, 0)))
    return m[:, :BATCH], m[:, BATCH:BATCH + 1]


def _as_batch(a):
    return a.reshape(BATCH, CTX_LEN, a.shape[-1])


def _as_slab(a):
    return a.reshape(1, BATCH * CTX_LEN, a.shape[-1])


def kernel(x, c, ctx, c_ctx, w_mod, b_mod, w_in, lam_re, lam_im, log_step, ssm_b_re, ssm_b_im, ssm_c_re,
           ssm_c_im, d_skip, w_glu, w_ps, w_pf, w_o, ln1_g, ln1_b, w_up, w_down, ln2_g, ln2_b):
    cond8 = jnp.concatenate([c, c_ctx[None], jnp.zeros((SUBLANES - BATCH - 1, D_MODEL), F32)], axis=0)
    mod_lat, mod_ctx = _mod_tables(_modulation(cond8, w_mod, b_mod))

    x_lat, h_lat = _entry(x, jnp.asarray(_POS), mod_lat)
    x_ctx, h_ctx = _entry(_as_slab(ctx), None, mod_ctx)
    h_zero = jnp.zeros((2, SUBLANES, STATE_W), F32)
    prm = jax.vmap(_s5_params)(lam_re, lam_im, log_step, ssm_b_re, ssm_b_im, ssm_c_re, ssm_c_im)

    for l in range(DEPTH):
        need_ctx = l < DEPTH - 1

        def mixer_tail(us, rest, yf, yb, xx, mod, as_batch, as_rows):
            yq = as_rows(_fourier_mix(as_batch(rest)))
            s = _glu(us, as_rows(yf), as_rows(yb), d_skip, w_glu, l)
            merged = _merge(s, yq, rest, w_ps, w_pf, l)
            return _tail(merged, xx, mod, w_o, w_up, w_down, ln1_g, ln1_b, ln2_g, ln2_b, l, need_ctx)

        us_lat, rest_lat = _in_proj(h_lat, w_in, l, True)
        if need_ctx:
            us_ctx, rest_ctx = _in_proj(h_ctx, w_in, l, True)
            yf_c, yb_c, h_t = _s5(_as_batch(us_ctx), h_zero, prm, l, True)
        else:
            (us_ctx,) = _in_proj(h_ctx, w_in, l, False)
            (h_t,) = _s5(_as_batch(us_ctx), h_zero, prm, l, False)
        yf, yb, _ = _s5(us_lat, h_t, prm, l, True)
        ident = lambda a: a
        x_lat, h_lat = mixer_tail(us_lat, rest_lat, yf, yb, x_lat, mod_lat, ident, ident)
        if need_ctx:
            x_ctx, h_ctx = mixer_tail(us_ctx, rest_ctx, yf_c, yb_c, x_ctx, mod_ctx, _as_batch, _as_slab)

    return x_lat
```

```python
import functools
import math

import numpy as np
import jax
import jax.numpy as jnp
from jax import lax
from jax.experimental import pallas as pl
from jax.experimental.pallas import tpu as pltpu

D_MODEL = 2048
BATCH = 4
SEQ = 2048
DEPTH = 2
GRID_W = 64
CTX_LEN = 256
SSM_WIDTH = D_MODEL // 2
SSM_GROUP = 16
SSM_GROUPS = SSM_WIDTH // SSM_GROUP
SSM_STATE = 64
FFT_WIDTH = D_MODEL - SSM_WIDTH
FFT_GROUPS = 4
FFT_GROUP = FFT_WIDTH // FFT_GROUPS
IN_WIDTH = SSM_WIDTH + FFT_WIDTH + 2 * D_MODEL
REST_WIDTH = IN_WIDTH - SSM_WIDTH
D_FF = 4 * D_MODEL
ALPHA = (2 * DEPTH) ** 0.25
LN_EPS = 1e-5
POS_BASE = 10000.0

F32 = jnp.float32
BF16 = jnp.bfloat16

SUBLANES = 8
LANES = 128
STATE_W = SSM_GROUPS * SSM_STATE
SCAN_T = 64
S5_KGROUPS = 8
S5_NK = SSM_GROUPS // S5_KGROUPS
S5_KSTATE = S5_KGROUPS * SSM_STATE
MIB = 1024 * 1024

SHIFT1, SCALE1, GATE1, SHIFT2, SCALE2, GATE2 = range(6)


def _params(sem, vmem_mib):
    return pltpu.CompilerParams(dimension_semantics=sem, vmem_limit_bytes=vmem_mib * MIB)


def _dot(a, b):
    return jnp.dot(a, b, preferred_element_type=F32)


def _ln(x, eps=LN_EPS):
    mu = jnp.mean(x, axis=-1, keepdims=True)
    xc = x - mu
    var = jnp.mean(xc * xc, axis=-1, keepdims=True)
    return xc * lax.rsqrt(var + eps)


def _deepnorm_ln(x, gate, branch):
    return _ln(x + (gate * (1.0 / ALPHA)) * branch, LN_EPS / (ALPHA * ALPHA))


def _row(ref, j):
    return ref[j:j + 1, :]


def _modulated_ln(x, mod_ref, shift, scale):
    return (_ln(x) * (1.0 + _row(mod_ref, scale)) + _row(mod_ref, shift)).astype(BF16)


EPILOGUE_ROWS = 128


def _for_row_chunks(rows, fn):
    def body(r, carry):
        fn(pl.ds(pl.multiple_of(r * EPILOGUE_ROWS, EPILOGUE_ROWS), EPILOGUE_ROWS))
        return carry

    lax.fori_loop(0, rows // EPILOGUE_ROWS, body, 0)


def _mod_spec(layer, nb):
    if nb == 1:
        return pl.BlockSpec((None, None, SUBLANES, D_MODEL), lambda b, *_: (layer, 0, 0, 0))
    return pl.BlockSpec((None, None, SUBLANES, D_MODEL), lambda b, *_: (layer, b, 0, 0))


def _mod_kernel(c_ref, w_ref, b_ref, o_ref):
    c = c_ref[...]
    sc = c * jax.nn.sigmoid(c)
    o_ref[...] = _dot(sc.astype(BF16), w_ref[...].astype(BF16)) + b_ref[...]


def _modulation(cond8, w_mod, b_mod):
    tn = 1024
    n = 6 * D_MODEL
    return pl.pallas_call(
        _mod_kernel,
        grid=(DEPTH, n // tn),
        in_specs=[
            pl.BlockSpec((SUBLANES, D_MODEL), lambda l, j: (0, 0)),
            pl.BlockSpec((None, D_MODEL, tn), lambda l, j: (l, 0, j)),
            pl.BlockSpec((None, 1, tn), lambda l, j: (l, 0, j)),
        ],
        out_specs=pl.BlockSpec((None, SUBLANES, tn), lambda l, j: (l, 0, j)),
        out_shape=jax.ShapeDtypeStruct((DEPTH, SUBLANES, n), F32),
        compiler_params=_params(("parallel", "parallel"), 40),
        name="adaln_modulation",
    )(cond8, w_mod, b_mod.reshape(DEPTH, 1, n))


def _entry_kernel(has_pos, *refs):
    if has_pos:
        x_ref, p_ref, mod_ref, xo_ref, h_ref = refs
        x = x_ref[...] + p_ref[...]
        xo_ref[...] = x
    else:
        x_ref, mod_ref, h_ref = refs
        x = x_ref[...]
    h_ref[...] = _modulated_ln(x, mod_ref, SHIFT1, SCALE1)


def _entry(x, pos, mod):
    nb, rows, _ = x.shape
    tm = 512
    has_pos = pos is not None
    blk = pl.BlockSpec((None, tm, D_MODEL), lambda i, b: (b, i, 0))
    in_specs = [blk]
    args = [x]
    if has_pos:
        in_specs.append(pl.BlockSpec((tm, D_MODEL), lambda i, b: (i, 0)))
        args.append(pos)
    in_specs.append(pl.BlockSpec((None, None, SUBLANES, D_MODEL), lambda i, b: (0, b, 0, 0)))
    args.append(mod)
    h_shape = jax.ShapeDtypeStruct(x.shape, BF16)
    out = pl.pallas_call(
        functools.partial(_entry_kernel, has_pos),
        grid=(rows // tm, nb),
        in_specs=in_specs,
        out_specs=[blk, blk] if has_pos else [blk],
        out_shape=[jax.ShapeDtypeStruct(x.shape, F32), h_shape] if has_pos else [h_shape],
        compiler_params=_params(("parallel", "parallel"), 40),
        name="entry_ln",
    )(*args)
    return out if has_pos else (x, out[0])


ROW_SPLIT = 2


def _win_kernel(n_us, with_rest, h_ref, w_ref, us_ref, *rest):
    piece = h_ref.shape[0] // ROW_SPLIT

    def project(o_ref):
        w = w_ref[...].astype(BF16)
        for r in range(ROW_SPLIT):
            rs = slice(r * piece, (r + 1) * piece)
            o_ref[rs, :] = _dot(h_ref[rs, :], w).astype(o_ref.dtype)

    if not with_rest:
        project(us_ref)
        return
    rest_ref, = rest
    j = pl.program_id(2)

    @pl.when(j < n_us)
    def _():
        project(us_ref)

    @pl.when(j >= n_us)
    def _():
        project(rest_ref)


def _in_proj(h, w_in, layer, with_rest):
    nb, rows, _ = h.shape
    tm = min(rows, 2048)
    tn = 512
    n_us = SSM_WIDTH // tn
    n_tiles = (IN_WIDTH if with_rest else SSM_WIDTH) // tn
    out_specs = [pl.BlockSpec((None, tm, tn), lambda b, i, j: (b, i, jnp.minimum(j, n_us - 1)))]
    out_shape = [jax.ShapeDtypeStruct((nb, rows, SSM_WIDTH), F32)]
    if with_rest:
        out_specs.append(pl.BlockSpec((None, tm, tn), lambda b, i, j: (b, i, jnp.maximum(j - n_us, 0))))
        out_shape.append(jax.ShapeDtypeStruct((nb, rows, REST_WIDTH), BF16))
    return pl.pallas_call(
        functools.partial(_win_kernel, n_us, with_rest),
        grid=(nb, rows // tm, n_tiles),
        in_specs=[
            pl.BlockSpec((None, tm, D_MODEL), lambda b, i, j: (b, i, 0)),
            pl.BlockSpec((None, D_MODEL, tn), lambda b, i, j: (layer, 0, j)),
        ],
        out_specs=out_specs,
        out_shape=out_shape,
        compiler_params=_params(("parallel", "parallel", "arbitrary"), 48),
        name="in_proj",
    )(h, w_in)


def _s5_kernel(need_y, uf_ref, ub_ref, h0_ref, pf_ref, pb_ref, pft_ref, pbt_ref, a_ref, wd_ref, wr_ref,
               *rest):
    if need_y:
        yf_ref, yb_ref, ht_ref, v_ref, hc_ref, h_ref, ycf_ref, ycb_ref = rest
    else:
        ht_ref, v_ref, hc_ref = rest
    g = pl.program_id(0)
    rows_in = BATCH * SCAN_T
    pair = 2 * SUBLANES

    @pl.when(g == 0)
    def _():
        hc_ref[...] = h0_ref[...]

    uf = uf_ref[...].reshape(rows_in, SSM_WIDTH).astype(BF16)
    ub = ub_ref[...].reshape(rows_in, SSM_WIDTH).astype(BF16)
    up_f = _dot(pf_ref[...], uf).astype(BF16)
    up_b = _dot(pb_ref[...], ub).astype(BF16)

    def slab_cols(k):
        return slice(2 * k * S5_KSTATE, (2 * k + 2) * S5_KSTATE)

    def drive(k):
        cs = slice(k * LANES, (k + 1) * LANES)
        lhs = jnp.concatenate([up_f[:, cs], up_b[:, cs]], axis=1)
        v_ref[:, slab_cols(k)] = _dot(lhs, wd_ref[k])

    def scan(k):
        ss = slice(k * S5_KSTATE, (k + 1) * S5_KSTATE)
        re_cols = slice(2 * k * S5_KSTATE, (2 * k + 1) * S5_KSTATE)
        im_cols = slice((2 * k + 1) * S5_KSTATE, (2 * k + 2) * S5_KSTATE)
        ar = a_ref[0, :, ss]
        ai = a_ref[1, :, ss]
        hr = hc_ref[0, :, ss]
        hi = hc_ref[1, :, ss]
        for s2 in range(SCAN_T // 2):
            rows = []
            for s in (2 * s2, 2 * s2 + 1):
                rs = slice(s * SUBLANES, (s + 1) * SUBLANES)
                nr = ar * hr - ai * hi + v_ref[rs, re_cols]
                ni = ar * hi + ai * hr + v_ref[rs, im_cols]
                hr, hi = nr, ni
                rows.append((nr, ni))
            if need_y:
                ps = slice(s2 * pair, (s2 + 1) * pair)
                h_ref[ps, re_cols] = jnp.concatenate([rows[0][0], rows[1][0]], axis=0).astype(BF16)
                h_ref[ps, im_cols] = jnp.concatenate([rows[0][1], rows[1][1]], axis=0).astype(BF16)
        hc_ref[0, :, ss] = hr
        hc_ref[1, :, ss] = hi

    def readout(k):
        cs = slice(k * LANES, (k + 1) * LANES)
        y = _dot(h_ref[:, slab_cols(k)], wr_ref[k])
        ycf_ref[:, cs] = y[:, :LANES]
        ycb_ref[:, cs] = y[:, LANES:]

    lag = 1
    for k in range(S5_NK + lag):
        if k < S5_NK:
            drive(k)
            scan(k)
        if need_y and k >= lag:
            readout(k - lag)
    if need_y:
        yf = _dot(pft_ref[...], ycf_ref[...].astype(BF16))
        yb = _dot(pbt_ref[...], ycb_ref[...].astype(BF16))
        yf_ref[...] = yf.astype(BF16).reshape(BATCH, SCAN_T, SSM_WIDTH)
        yb_ref[...] = yb.astype(BF16).reshape(BATCH, SCAN_T, SSM_WIDTH)

    @pl.when(g == pl.num_programs(0) - 1)
    def _():
        ht_ref[...] = hc_ref[...]


def _scan_perms():
    t = SCAN_T
    pf = np.zeros((2 * BATCH * t, BATCH * t), np.float32)
    pb = np.zeros((2 * BATCH * t, BATCH * t), np.float32)
    for s in range(t):
        for b in range(BATCH):
            pf[s * 2 * BATCH + b, b * t + s] = 1.0
            pb[s * 2 * BATCH + BATCH + b, b * t + (t - 1 - s)] = 1.0
    return pf, pb


_PF, _PB = _scan_perms()


def _s5(u, h0, prm, layer, need_y):
    a, wd, wr = prm
    steps = u.shape[1]
    n = steps // SCAN_T
    rows_in = BATCH * SCAN_T
    rows_sc = 2 * rows_in
    pf = jnp.asarray(_PF, BF16)
    pb = jnp.asarray(_PB, BF16)
    const2 = lambda g: (0, 0)
    once = pl.Buffered(1)
    blk = (BATCH, SCAN_T, SSM_WIDTH)
    fwd_blk = pl.BlockSpec(blk, lambda g: (0, g, 0))
    bwd_blk = pl.BlockSpec(blk, lambda g: (0, n - 1 - g, 0))
    state_spec = pl.BlockSpec((2, SUBLANES, STATE_W), lambda g: (0, 0, 0))
    in_specs = [
        fwd_blk, bwd_blk, state_spec,
        pl.BlockSpec((rows_sc, rows_in), const2),
        pl.BlockSpec((rows_sc, rows_in), const2),
        pl.BlockSpec((rows_in, rows_sc), const2),
        pl.BlockSpec((rows_in, rows_sc), const2),
        pl.BlockSpec((None, 2, SUBLANES, STATE_W), lambda g: (layer, 0, 0, 0)),
        pl.BlockSpec((None, S5_NK, 2 * LANES, 2 * S5_KSTATE), lambda g: (layer, 0, 0, 0), pipeline_mode=once),
        pl.BlockSpec((None, S5_NK, 2 * S5_KSTATE, 2 * LANES), lambda g: (layer, 0, 0, 0), pipeline_mode=once),
    ]
    state_shape = jax.ShapeDtypeStruct((2, SUBLANES, STATE_W), F32)
    scratch = [pltpu.VMEM((rows_sc, 2 * STATE_W), F32), pltpu.VMEM((2, SUBLANES, STATE_W), F32)]
    if need_y:
        out_specs = [fwd_blk, bwd_blk, state_spec]
        y_shape = jax.ShapeDtypeStruct((BATCH, steps, SSM_WIDTH), BF16)
        out_shape = [y_shape, y_shape, state_shape]
        scratch += [pltpu.VMEM((rows_sc, 2 * STATE_W), BF16), pltpu.VMEM((rows_sc, SSM_WIDTH), F32),
                    pltpu.VMEM((rows_sc, SSM_WIDTH), F32)]
    else:
        out_specs = [state_spec]
        out_shape = [state_shape]
    return pl.pallas_call(
        functools.partial(_s5_kernel, need_y),
        grid=(n,),
        in_specs=in_specs,
        out_specs=out_specs,
        out_shape=out_shape,
        scratch_shapes=scratch,
        compiler_params=_params(("arbitrary",), 52),
        name="s5_scan",
    )(u, u, h0, pf, pb, pf.T, pb.T, a, wd, wr)


def _s5_params(lam_re, lam_im, log_step, b_re, b_im, c_re, c_im):
    dt = jnp.exp(log_step)[..., None]
    mag = jnp.exp(lam_re * dt)
    ang = lam_im * dt
    abar_re, abar_im = mag * jnp.cos(ang), mag * jnp.sin(ang)
    den = lam_re * lam_re + lam_im * lam_im
    nr, ni = abar_re - 1.0, abar_im
    coef_re = (nr * lam_re + ni * lam_im) / den
    coef_im = (ni * lam_re - nr * lam_im) / den
    bb_re = coef_re[..., None] * b_re - coef_im[..., None] * b_im
    bb_im = coef_re[..., None] * b_im + coef_im[..., None] * b_re
    bb = jnp.stack([bb_re, bb_im])
    bb = jnp.swapaxes(bb, -1, -2).reshape(-1, SSM_STATE)
    own = ((np.arange(bb.shape[0]) // SSM_GROUP) % S5_KGROUPS)[:, None] == (
        np.arange(S5_KSTATE) // SSM_STATE)[None, :]
    bb = jnp.where(own, jnp.tile(bb, (1, S5_KGROUPS)), 0.0)
    bb = bb.reshape(2, 2, S5_NK, LANES, S5_KSTATE)
    wd = jnp.concatenate([bb[0], bb[1]], axis=-1)
    wd = jnp.transpose(wd, (1, 0, 2, 3)).reshape(S5_NK, 2 * LANES, 2 * S5_KSTATE).astype(BF16)
    cc = jnp.stack([c_re, -c_im])
    cc = jnp.swapaxes(cc, -1, -2).reshape(-1, SSM_GROUP)
    own = ((np.arange(cc.shape[0]) // SSM_STATE) % S5_KGROUPS)[:, None] == (
        np.arange(LANES) // SSM_GROUP)[None, :]
    cc = jnp.where(own, jnp.tile(cc, (1, S5_KGROUPS)), 0.0)
    cc = cc.reshape(2, 2, S5_NK, S5_KSTATE, LANES)
    wr = jnp.concatenate([cc[:, 0], cc[:, 1]], axis=-1)
    wr = jnp.transpose(wr, (1, 0, 2, 3)).reshape(S5_NK, 2 * S5_KSTATE, 2 * LANES).astype(BF16)
    a = jnp.stack([abar_re.reshape(2, STATE_W), abar_im.reshape(2, STATE_W)])
    a = jnp.repeat(a, BATCH, axis=1)
    return a, wd, wr


def _dft_tables(n):
    j = np.arange(n, dtype=np.int64)
    ang = 2.0 * np.pi * ((j[:, None] * j[None, :]) % n).astype(np.float64) / n
    s = 1.0 / math.sqrt(n)
    return (np.cos(ang) * s).astype(np.float32), (np.sin(ang) * s).astype(np.float32)


_CH_COS, _CH_SIN = _dft_tables(FFT_GROUP)
_CH_CS = np.concatenate([_CH_COS, _CH_SIN], axis=1)


DFT_RADIX = 4
DFT_ILV = 256
_QUARTER_TURN = ((1, 0), (0, 1), (-1, 0), (0, -1))


def _radix_tables(n):
    nq = n // DFT_RADIX
    k = np.arange(nq, dtype=np.int64)
    s = 1.0 / math.sqrt(n)
    cos, sin = [], []
    for r in range(DFT_RADIX):
        j = DFT_RADIX * k + r
        ang = 2.0 * np.pi * ((j[:, None] * k[None, :]) % n).astype(np.float64) / n
        cos.append(np.cos(ang) * s)
        sin.append(np.sin(ang) * s)
    return np.stack(cos).astype(np.float32), np.stack(sin).astype(np.float32)


def _interleave_perm():
    per = DFT_ILV // DFT_RADIX
    perm = np.zeros((DFT_ILV, DFT_ILV), np.float32)
    for r in range(DFT_RADIX):
        for m in range(per):
            perm[DFT_RADIX * m + r, r * per + m] = 1.0
    return perm


_RADIX_TABLES = {n: _radix_tables(n) for n in (CTX_LEN, SEQ)}
_ILV_PERM = _interleave_perm()


def _fnet_kernel(n, u_ref, w_ref, c_ref, s_ref, perm_ref, o_ref, p_ref, q_ref, comb_ref, y_ref):
    nq = n // DFT_RADIX
    chunk = min(nq, EPILOGUE_ROWS)

    w = w_ref[...].astype(BF16)
    for g in range(FFT_GROUPS):
        cs = slice(g * FFT_GROUP, (g + 1) * FFT_GROUP)
        r = _dot(u_ref[:, cs], w)
        p_ref[:, cs] = r[:, :FFT_GROUP].astype(BF16)
        q_ref[:, cs] = r[:, FFT_GROUP:].astype(BF16)

    def combine(i, carry):
        rs = pl.ds(pl.multiple_of(i * chunk, chunk), chunk)
        p = [p_ref[pl.ds(pl.multiple_of(q * nq + i * chunk, chunk), chunk), :].astype(F32)
             for q in range(DFT_RADIX)]
        qq = [q_ref[pl.ds(pl.multiple_of(q * nq + i * chunk, chunk), chunk), :].astype(F32)
              for q in range(DFT_RADIX)]
        def signed_sum(terms):
            acc = None
            for sign, v in terms:
                if acc is None:
                    acc = v if sign > 0 else -v
                else:
                    acc = acc + v if sign > 0 else acc - v
            return acc

        for r in range(DFT_RADIX):
            pr, qr = [], []
            for q in range(DFT_RADIX):
                cs, sn = _QUARTER_TURN[(r * q) % DFT_RADIX]
                if cs:
                    pr.append((cs, p[q]))
                    qr.append((cs, qq[q]))
                if sn:
                    pr.append((-sn, qq[q]))
                    qr.append((sn, p[q]))
            comb_ref[2 * r, rs, :] = signed_sum(pr).astype(BF16)
            comb_ref[2 * r + 1, rs, :] = signed_sum(qr).astype(BF16)
        return carry

    lax.fori_loop(0, nq // chunk, combine, 0)
    for r in range(DFT_RADIX):
        y_ref[r] = (_dot(c_ref[r].astype(BF16), comb_ref[2 * r])
                    - _dot(s_ref[r].astype(BF16), comb_ref[2 * r + 1]))
    per = DFT_ILV // DFT_RADIX
    for blk in range(n // DFT_ILV):
        slab = jnp.concatenate([y_ref[r, blk * per:(blk + 1) * per, :] for r in range(DFT_RADIX)], axis=0)
        o_ref[blk * DFT_ILV:(blk + 1) * DFT_ILV, :] = _dot(perm_ref[...], slab.astype(BF16)).astype(BF16)


def _fourier_mix(rest):
    nb, steps, _ = rest.shape
    cos, sin = _RADIX_TABLES[steps]
    nq = steps // DFT_RADIX
    x_blk = pl.BlockSpec((None, steps, FFT_WIDTH), lambda b: (b, 0, 0))
    t_blk = pl.BlockSpec((DFT_RADIX, nq, nq), lambda b: (0, 0, 0), pipeline_mode=pl.Buffered(1))
    return pl.pallas_call(
        functools.partial(_fnet_kernel, steps),
        grid=(nb,),
        in_specs=[x_blk, pl.BlockSpec((FFT_GROUP, 2 * FFT_GROUP), lambda b: (0, 0)), t_blk, t_blk,
                  pl.BlockSpec((DFT_ILV, DFT_ILV), lambda b: (0, 0))],
        out_specs=x_blk,
        out_shape=jax.ShapeDtypeStruct((nb, steps, FFT_WIDTH), BF16),
        scratch_shapes=[pltpu.VMEM((steps, FFT_WIDTH), BF16), pltpu.VMEM((steps, FFT_WIDTH), BF16),
                        pltpu.VMEM((2 * DFT_RADIX, nq, FFT_WIDTH), BF16),
                        pltpu.VMEM((DFT_RADIX, nq, FFT_WIDTH), F32)],
        compiler_params=_params(("parallel",), 56),
        name="fourier_mix",
    )(rest, jnp.asarray(_CH_CS), jnp.asarray(cos), jnp.asarray(sin), jnp.asarray(_ILV_PERM, BF16))


def _glu_kernel(us_ref, yf_ref, yb_ref, dsk_ref, w_ref, o_ref):
    ys = dsk_ref[...] * us_ref[...] + yf_ref[...].astype(F32) + yb_ref[...].astype(F32)
    g = jax.nn.gelu(ys)
    z = _dot(g.astype(BF16), w_ref[...].astype(BF16))
    o_ref[...] = (g * jax.nn.sigmoid(z)).astype(BF16)


def _glu(us, yf, yb, d_skip, w_glu, layer):
    nb, rows, _ = us.shape
    tm = 1024
    blk = pl.BlockSpec((None, tm, SSM_WIDTH), lambda b, i: (b, i, 0))
    return pl.pallas_call(
        _glu_kernel,
        grid=(nb, rows // tm),
        in_specs=[blk, blk, blk,
                  pl.BlockSpec((None, 1, SSM_WIDTH), lambda b, i: (layer, 0, 0)),
                  pl.BlockSpec((None, SSM_WIDTH, SSM_WIDTH), lambda b, i: (layer, 0, 0))],
        out_specs=blk,
        out_shape=jax.ShapeDtypeStruct((nb, rows, SSM_WIDTH), BF16),
        compiler_params=_params(("parallel", "parallel"), 40),
        name="s5_glu",
    )(us, yf, yb, d_skip.reshape(DEPTH, 1, SSM_WIDTH), w_glu)


MERGE_SPLIT = 8


def _merge_kernel(s_ref, yq_ref, gs_ref, gf_ref, wps_ref, wpf_ref, o_ref):
    wps = wps_ref[...].astype(BF16)
    wpf = wpf_ref[...].astype(BF16)
    piece = o_ref.shape[0] // MERGE_SPLIT
    for r in range(MERGE_SPLIT):
        rs = slice(r * piece, (r + 1) * piece)
        ps = _dot(s_ref[rs, :], wps)
        pf = _dot(yq_ref[rs, :], wpf)
        o_ref[rs, :] = (jax.nn.sigmoid(gs_ref[rs, :].astype(F32)) * ps
                        + jax.nn.sigmoid(gf_ref[rs, :].astype(F32)) * pf).astype(BF16)


def _merge(s, yq, rest, w_ps, w_pf, layer):
    nb, rows, _ = s.shape
    tm = min(rows, 2048)
    tn = 512
    gs_off = FFT_WIDTH // tn
    gf_off = (FFT_WIDTH + D_MODEL) // tn
    row_blk = pl.BlockSpec((None, tm, SSM_WIDTH), lambda b, i, j: (b, i, 0))
    w_blk = pl.BlockSpec((None, SSM_WIDTH, tn), lambda b, i, j: (layer, 0, j))
    return pl.pallas_call(
        _merge_kernel,
        grid=(nb, rows // tm, D_MODEL // tn),
        in_specs=[
            row_blk, row_blk,
            pl.BlockSpec((None, tm, tn), lambda b, i, j: (b, i, gs_off + j)),
            pl.BlockSpec((None, tm, tn), lambda b, i, j: (b, i, gf_off + j)),
            w_blk, w_blk,
        ],
        out_specs=pl.BlockSpec((None, tm, tn), lambda b, i, j: (b, i, j)),
        out_shape=jax.ShapeDtypeStruct((nb, rows, D_MODEL), BF16),
        compiler_params=_params(("parallel", "parallel", "arbitrary"), 48),
        name="gated_merge",
    )(s, yq, rest, rest, w_ps, w_pf)


TAIL_TM = 1024
TAIL_TK = 512
TAIL_TF = 512


def _tail_kernel(layer, emit_next, merged_ref, x_hbm, wo_hbm, wu_hbm, wd_hbm, mod_ref, *refs):
    if emit_next:
        modn_ref, refs = refs[0], refs[1:]
    g1_ref, b1_ref, g2_ref, b2_ref, out_hbm = refs[:5]
    refs = refs[5:]
    if emit_next:
        hn_ref, refs = refs[0], refs[1:]
    xbuf, acc, h2, act, wa, wb, sem_a, sem_b, sem_x, sem_o = refs
    b = pl.program_id(0)
    i = pl.program_id(1)
    first = jnp.logical_and(b == 0, i == 0)
    last = jnp.logical_and(b == pl.num_programs(0) - 1, i == pl.num_programs(1) - 1)
    nk = D_MODEL // TAIL_TK
    nf = D_FF // TAIL_TF
    assert nk % 2 == 0
    rows = pl.ds(pl.multiple_of(i * TAIL_TM, TAIL_TM), TAIL_TM)

    def wo_copy(t, slot):
        return pltpu.make_async_copy(wo_hbm.at[layer, pl.ds(t * TAIL_TK, TAIL_TK), :], wa.at[slot], sem_a.at[slot])

    def wd_copy(f, slot):
        src = wd_hbm.at[layer, pl.ds(pl.multiple_of(f * TAIL_TK, TAIL_TK), TAIL_TK), :]
        return pltpu.make_async_copy(src, wa.at[slot], sem_a.at[slot])

    def wu_copy(f, slot):
        src = wu_hbm.at[layer, :, pl.ds(pl.multiple_of(f * TAIL_TF, TAIL_TF), TAIL_TF)]
        return pltpu.make_async_copy(src, wb.at[slot], sem_b.at[slot])

    def x_copy():
        return pltpu.make_async_copy(x_hbm.at[b, rows, :], xbuf, sem_x.at[0])

    def out_copy():
        return pltpu.make_async_copy(xbuf, out_hbm.at[b, rows, :], sem_o.at[0])

    def start_first_tiles():
        wo_copy(0, 0).start()
        wo_copy(1, 1).start()
        wu_copy(0, 0).start()
        wu_copy(1, 1).start()

    @pl.when(first)
    def _():
        start_first_tiles()

    for k in range(nk):
        slot = k % 2
        wo_copy(k, slot).wait()
        lhs = merged_ref[:, k * TAIL_TK:(k + 1) * TAIL_TK]
        if k == 0:
            acc[...] = _dot(lhs, wa[slot].astype(BF16))
        else:
            acc[...] += _dot(lhs, wa[slot].astype(BF16))
        if k + 2 < nk:
            wo_copy(k + 2, slot).start()
        if k == 1:
            @pl.when(jnp.logical_not(first))
            def _():
                out_copy().wait()

            x_copy().start()
    for slot in range(2):
        wd_copy(slot, slot).start()
    x_copy().wait()

    def finish1(rs):
        x1 = _deepnorm_ln(xbuf[rs, :], _row(mod_ref, GATE1), acc[rs, :]) * g1_ref[...] + b1_ref[...]
        xbuf[rs, :] = x1
        h2[rs, :] = _modulated_ln(x1, mod_ref, SHIFT2, SCALE2)

    _for_row_chunks(TAIL_TM, finish1)

    def up_tile(slot):
        a = jnp.maximum(_dot(h2[...], wb[slot].astype(BF16)), 0.0)
        act[slot] = (a * a).astype(BF16)

    def down_tile(slot, first=False):
        if first:
            acc[...] = _dot(act[slot], wa[slot].astype(BF16))
        else:
            acc[...] += _dot(act[slot], wa[slot].astype(BF16))

    def trip(f, slot):
        nxt = 1 - slot
        wu_copy(f + 1, nxt).wait()
        wd_copy(f, slot).wait()
        up_tile(nxt)
        down_tile(slot, first=isinstance(f, int) and f == 0)
        if isinstance(f, int):
            if f + 3 < nf:
                wu_copy(f + 3, nxt).start()
            if f + 2 < nf:
                wd_copy(f + 2, slot).start()
        else:
            @pl.when(f + 3 < nf)
            def _():
                wu_copy(f + 3, nxt).start()

            @pl.when(f + 2 < nf)
            def _():
                wd_copy(f + 2, slot).start()

    wu_copy(0, 0).wait()
    up_tile(0)
    wu_copy(2, 0).start()

    trip(0, 0)
    assert nf % 2 == 0

    def trip_pair(p, carry):
        trip(2 * p + 1, 1)
        trip(2 * p + 2, 0)
        return carry

    lax.fori_loop(0, (nf - 2) // 2, trip_pair, 0)

    def finish2(rs):
        x2 = _deepnorm_ln(xbuf[rs, :], _row(mod_ref, GATE2), acc[rs, :]) * g2_ref[...] + b2_ref[...]
        xbuf[rs, :] = x2
        if emit_next:
            hn_ref[rs, :] = _modulated_ln(x2, modn_ref, SHIFT1, SCALE1)

    wd_copy(nf - 1, (nf - 1) % 2).wait()
    down_tile((nf - 1) % 2)

    @pl.when(jnp.logical_not(last))
    def _():
        start_first_tiles()

    _for_row_chunks(TAIL_TM, finish2)
    out_copy().start()

    @pl.when(last)
    def _():
        out_copy().wait()


def _tail(merged, x, mod, w_o, w_up, w_down, g1, b1, g2, b2, layer, emit_next):
    nb, rows, _ = x.shape
    tm = TAIL_TM
    row_blk = pl.BlockSpec((None, tm, D_MODEL), lambda b_, i: (b_, i, 0))
    vec = pl.BlockSpec((None, 1, D_MODEL), lambda b_, i: (layer, 0, 0))
    hbm = pl.BlockSpec(memory_space=pl.ANY)
    in_specs = [row_blk, hbm, hbm, hbm, hbm, _mod_spec(layer, nb)]
    args = [merged, x, w_o, w_up, w_down, mod]
    if emit_next:
        in_specs.append(_mod_spec(layer + 1, nb))
        args.append(mod)
    in_specs += [vec, vec, vec, vec]
    args += [v.reshape(DEPTH, 1, D_MODEL) for v in (g1, b1, g2, b2)]
    out_specs = [hbm]
    out_shape = [jax.ShapeDtypeStruct(x.shape, F32)]
    if emit_next:
        out_specs.append(row_blk)
        out_shape.append(jax.ShapeDtypeStruct(x.shape, BF16))
    out = pl.pallas_call(
        functools.partial(_tail_kernel, layer, emit_next),
        grid=(nb, rows // tm),
        in_specs=in_specs,
        out_specs=out_specs,
        out_shape=out_shape,
        scratch_shapes=[
            pltpu.VMEM((tm, D_MODEL), F32),
            pltpu.VMEM((tm, D_MODEL), F32),
            pltpu.VMEM((tm, D_MODEL), BF16),
            pltpu.VMEM((2, tm, TAIL_TF), BF16),
            pltpu.VMEM((2, TAIL_TK, D_MODEL), F32),
            pltpu.VMEM((2, D_MODEL, TAIL_TF), F32),
            pltpu.SemaphoreType.DMA((2,)),
            pltpu.SemaphoreType.DMA((2,)),
            pltpu.SemaphoreType.DMA((1,)),
            pltpu.SemaphoreType.DMA((1,)),
        ],
        compiler_params=_params(("arbitrary", "arbitrary"), 58),
        name="layer_tail",
    )(*args)
    return out if emit_next else (out[0], None)


def _pos_table():
    quarter = D_MODEL // 4
    omega = 1.0 / (POS_BASE ** (np.arange(quarter, dtype=np.float64) / quarter))
    t = np.arange(SEQ)
    ar = (t // GRID_W).astype(np.float64)[:, None] * omega
    ac = (t % GRID_W).astype(np.float64)[:, None] * omega
    return np.concatenate([np.sin(ar), np.cos(ar), np.sin(ac), np.cos(ac)], axis=-1).astype(np.float32)


_POS = _pos_table()


def _mod_tables(m):
    m = m.reshape(DEPTH, SUBLANES, 6, D_MODEL)
    m = jnp.pad(m, ((0, 0), (0, 0), (0, SUBLANES - 6), (0, 0)))
    return m[:, :BATCH], m[:, BATCH:BATCH + 1]


def _as_batch(a):
    return a.reshape(BATCH, CTX_LEN, a.shape[-1])


def _as_slab(a):
    return a.reshape(1, BATCH * CTX_LEN, a.shape[-1])


def kernel(x, c, ctx, c_ctx, w_mod, b_mod, w_in, lam_re, lam_im, log_step, ssm_b_re, ssm_b_im, ssm_c_re,
           ssm_c_im, d_skip, w_glu, w_ps, w_pf, w_o, ln1_g, ln1_b, w_up, w_down, ln2_g, ln2_b):
    cond8 = jnp.concatenate([c, c_ctx[None], jnp.zeros((SUBLANES - BATCH - 1, D_MODEL), F32)], axis=0)
    mod_lat, mod_ctx = _mod_tables(_modulation(cond8, w_mod, b_mod))

    x_lat, h_lat = _entry(x, jnp.asarray(_POS), mod_lat)
    x_ctx, h_ctx = _entry(_as_slab(ctx), None, mod_ctx)
    h_zero = jnp.zeros((2, SUBLANES, STATE_W), F32)
    prm = jax.vmap(_s5_params)(lam_re, lam_im, log_step, ssm_b_re, ssm_b_im, ssm_c_re, ssm_c_im)

    for l in range(DEPTH):
        need_ctx = l < DEPTH - 1

        def mixer_tail(us, rest, yf, yb, xx, mod, as_batch, as_rows):
            yq = as_rows(_fourier_mix(as_batch(rest)))
            s = _glu(us, as_rows(yf), as_rows(yb), d_skip, w_glu, l)
            merged = _merge(s, yq, rest, w_ps, w_pf, l)
            return _tail(merged, xx, mod, w_o, w_up, w_down, ln1_g, ln1_b, ln2_g, ln2_b, l, need_ctx)

        us_lat, rest_lat = _in_proj(h_lat, w_in, l, True)
        if need_ctx:
            us_ctx, rest_ctx = _in_proj(h_ctx, w_in, l, True)
            yf_c, yb_c, h_t = _s5(_as_batch(us_ctx), h_zero, prm, l, True)
        else:
            (us_ctx,) = _in_proj(h_ctx, w_in, l, False)
            (h_t,) = _s5(_as_batch(us_ctx), h_zero, prm, l, False)
        yf, yb, _ = _s5(us_lat, h_t, prm, l, True)
        ident = lambda a: a
        x_lat, h_lat = mixer_tail(us_lat, rest_lat, yf, yb, x_lat, mod_lat, ident, ident)
        if need_ctx:
            x_ctx, h_ctx = mixer_tail(us_ctx, rest_ctx, yf_c, yb_c, x_ctx, mod_ctx, _as_batch, _as_slab)

    return x_lat
```

```python
import functools
import math

import numpy as np
import jax
import jax.numpy as jnp
from jax import lax
from jax.experimental import pallas as pl
from jax.experimental.pallas import tpu as pltpu

D_MODEL = 2048
BATCH = 4
SEQ = 2048
DEPTH = 2
GRID_W = 64
CTX_LEN = 256
SSM_WIDTH = D_MODEL // 2
SSM_GROUP = 16
SSM_GROUPS = SSM_WIDTH // SSM_GROUP
SSM_STATE = 64
FFT_WIDTH = D_MODEL - SSM_WIDTH
FFT_GROUPS = 4
FFT_GROUP = FFT_WIDTH // FFT_GROUPS
IN_WIDTH = SSM_WIDTH + FFT_WIDTH + 2 * D_MODEL
REST_WIDTH = IN_WIDTH - SSM_WIDTH
D_FF = 4 * D_MODEL
ALPHA = (2 * DEPTH) ** 0.25
LN_EPS = 1e-5
POS_BASE = 10000.0

F32 = jnp.float32
BF16 = jnp.bfloat16

SUBLANES = 8
LANES = 128
STATE_W = SSM_GROUPS * SSM_STATE
SCAN_T = 64
S5_KGROUPS = 8
S5_NK = SSM_GROUPS // S5_KGROUPS
S5_KSTATE = S5_KGROUPS * SSM_STATE
MIB = 1024 * 1024

SHIFT1, SCALE1, GATE1, SHIFT2, SCALE2, GATE2 = range(6)


def _params(sem, vmem_mib):
    return pltpu.CompilerParams(dimension_semantics=sem, vmem_limit_bytes=vmem_mib * MIB)


def _dot(a, b):
    return jnp.dot(a, b, preferred_element_type=F32)


def _ln(x, eps=LN_EPS):
    mu = jnp.mean(x, axis=-1, keepdims=True)
    xc = x - mu
    var = jnp.mean(xc * xc, axis=-1, keepdims=True)
    return xc * lax.rsqrt(var + eps)


def _deepnorm_ln(x, gate, branch):
    return _ln(x + (gate * (1.0 / ALPHA)) * branch, LN_EPS / (ALPHA * ALPHA))


def _row(ref, j):
    return ref[j:j + 1, :]


def _modulated_ln(x, mod_ref, shift, scale):
    return (_ln(x) * (1.0 + _row(mod_ref, scale)) + _row(mod_ref, shift)).astype(BF16)


EPILOGUE_ROWS = 128


def _for_row_chunks(rows, fn):
    def body(r, carry):
        fn(pl.ds(pl.multiple_of(r * EPILOGUE_ROWS, EPILOGUE_ROWS), EPILOGUE_ROWS))
        return carry

    lax.fori_loop(0, rows // EPILOGUE_ROWS, body, 0)


def _mod_spec(layer, nb):
    if nb == 1:
        return pl.BlockSpec((None, None, SUBLANES, D_MODEL), lambda b, *_: (layer, 0, 0, 0))
    return pl.BlockSpec((None, None, SUBLANES, D_MODEL), lambda b, *_: (layer, b, 0, 0))


def _mod_kernel(c_ref, w_ref, b_ref, o_ref):
    c = c_ref[...]
    sc = c * jax.nn.sigmoid(c)
    o_ref[...] = _dot(sc.astype(BF16), w_ref[...].astype(BF16)) + b_ref[...]


def _modulation(cond8, w_mod, b_mod):
    tn = 1024
    n = 6 * D_MODEL
    return pl.pallas_call(
        _mod_kernel,
        grid=(DEPTH, n // tn),
        in_specs=[
            pl.BlockSpec((SUBLANES, D_MODEL), lambda l, j: (0, 0)),
            pl.BlockSpec((None, D_MODEL, tn), lambda l, j: (l, 0, j)),
            pl.BlockSpec((None, 1, tn), lambda l, j: (l, 0, j)),
        ],
        out_specs=pl.BlockSpec((None, SUBLANES, tn), lambda l, j: (l, 0, j)),
        out_shape=jax.ShapeDtypeStruct((DEPTH, SUBLANES, n), F32),
        compiler_params=_params(("parallel", "parallel"), 40),
        name="adaln_modulation",
    )(cond8, w_mod, b_mod.reshape(DEPTH, 1, n))


def _entry_kernel(has_pos, *refs):
    if has_pos:
        x_ref, p_ref, mod_ref, xo_ref, h_ref = refs
        x = x_ref[...] + p_ref[...]
        xo_ref[...] = x
    else:
        x_ref, mod_ref, h_ref = refs
        x = x_ref[...]
    h_ref[...] = _modulated_ln(x, mod_ref, SHIFT1, SCALE1)


def _entry(x, pos, mod):
    nb, rows, _ = x.shape
    tm = 512
    has_pos = pos is not None
    blk = pl.BlockSpec((None, tm, D_MODEL), lambda i, b: (b, i, 0))
    in_specs = [blk]
    args = [x]
    if has_pos:
        in_specs.append(pl.BlockSpec((tm, D_MODEL), lambda i, b: (i, 0)))
        args.append(pos)
    in_specs.append(pl.BlockSpec((None, None, SUBLANES, D_MODEL), lambda i, b: (0, b, 0, 0)))
    args.append(mod)
    h_shape = jax.ShapeDtypeStruct(x.shape, BF16)
    out = pl.pallas_call(
        functools.partial(_entry_kernel, has_pos),
        grid=(rows // tm, nb),
        in_specs=in_specs,
        out_specs=[blk, blk] if has_pos else [blk],
        out_shape=[jax.ShapeDtypeStruct(x.shape, F32), h_shape] if has_pos else [h_shape],
        compiler_params=_params(("parallel", "parallel"), 40),
        name="entry_ln",
    )(*args)
    return out if has_pos else (x, out[0])


ROW_SPLIT = 2


def _win_kernel(n_us, with_rest, h_ref, w_ref, us_ref, *rest):
    piece = h_ref.shape[0] // ROW_SPLIT

    def project(o_ref):
        w = w_ref[...].astype(BF16)
        for r in range(ROW_SPLIT):
            rs = slice(r * piece, (r + 1) * piece)
            o_ref[rs, :] = _dot(h_ref[rs, :], w).astype(o_ref.dtype)

    if not with_rest:
        project(us_ref)
        return
    rest_ref, = rest
    j = pl.program_id(2)

    @pl.when(j < n_us)
    def _():
        project(us_ref)

    @pl.when(j >= n_us)
    def _():
        project(rest_ref)


def _in_proj(h, w_in, layer, with_rest):
    nb, rows, _ = h.shape
    tm = min(rows, 2048)
    tn = 512
    n_us = SSM_WIDTH // tn
    n_tiles = (IN_WIDTH if with_rest else SSM_WIDTH) // tn
    out_specs = [pl.BlockSpec((None, tm, tn), lambda b, i, j: (b, i, jnp.minimum(j, n_us - 1)))]
    out_shape = [jax.ShapeDtypeStruct((nb, rows, SSM_WIDTH), F32)]
    if with_rest:
        out_specs.append(pl.BlockSpec((None, tm, tn), lambda b, i, j: (b, i, jnp.maximum(j - n_us, 0))))
        out_shape.append(jax.ShapeDtypeStruct((nb, rows, REST_WIDTH), BF16))
    return pl.pallas_call(
        functools.partial(_win_kernel, n_us, with_rest),
        grid=(nb, rows // tm, n_tiles),
        in_specs=[
            pl.BlockSpec((None, tm, D_MODEL), lambda b, i, j: (b, i, 0)),
            pl.BlockSpec((None, D_MODEL, tn), lambda b, i, j: (layer, 0, j)),
        ],
        out_specs=out_specs,
        out_shape=out_shape,
        compiler_params=_params(("parallel", "parallel", "arbitrary"), 48),
        name="in_proj",
    )(h, w_in)


def _s5_kernel(need_y, uf_ref, ub_ref, h0_ref, pf_ref, pb_ref, pft_ref, pbt_ref, a_ref, wd_ref, wr_ref,
               *rest):
    if need_y:
        yf_ref, yb_ref, ht_ref, v_ref, hc_ref, h_ref, ycf_ref, ycb_ref = rest
    else:
        ht_ref, v_ref, hc_ref = rest
    g = pl.program_id(0)
    rows_in = BATCH * SCAN_T
    pair = 2 * SUBLANES

    @pl.when(g == 0)
    def _():
        hc_ref[...] = h0_ref[...]

    uf = uf_ref[...].reshape(rows_in, SSM_WIDTH).astype(BF16)
    ub = ub_ref[...].reshape(rows_in, SSM_WIDTH).astype(BF16)
    up_f = _dot(pf_ref[...], uf).astype(BF16)
    up_b = _dot(pb_ref[...], ub).astype(BF16)

    def slab_cols(k):
        return slice(2 * k * S5_KSTATE, (2 * k + 2) * S5_KSTATE)

    def drive(k):
        cs = slice(k * LANES, (k + 1) * LANES)
        lhs = jnp.concatenate([up_f[:, cs], up_b[:, cs]], axis=1)
        v_ref[:, slab_cols(k)] = _dot(lhs, wd_ref[k])

    def scan(k):
        ss = slice(k * S5_KSTATE, (k + 1) * S5_KSTATE)
        re_cols = slice(2 * k * S5_KSTATE, (2 * k + 1) * S5_KSTATE)
        im_cols = slice((2 * k + 1) * S5_KSTATE, (2 * k + 2) * S5_KSTATE)
        ar = a_ref[0, :, ss]
        ai = a_ref[1, :, ss]
        hr = hc_ref[0, :, ss]
        hi = hc_ref[1, :, ss]
        for s2 in range(SCAN_T // 2):
            rows = []
            for s in (2 * s2, 2 * s2 + 1):
                rs = slice(s * SUBLANES, (s + 1) * SUBLANES)
                nr = ar * hr - ai * hi + v_ref[rs, re_cols]
                ni = ar * hi + ai * hr + v_ref[rs, im_cols]
                hr, hi = nr, ni
                rows.append((nr, ni))
            if need_y:
                ps = slice(s2 * pair, (s2 + 1) * pair)
                h_ref[ps, re_cols] = jnp.concatenate([rows[0][0], rows[1][0]], axis=0).astype(BF16)
                h_ref[ps, im_cols] = jnp.concatenate([rows[0][1], rows[1][1]], axis=0).astype(BF16)
        hc_ref[0, :, ss] = hr
        hc_ref[1, :, ss] = hi

    def readout(k):
        cs = slice(k * LANES, (k + 1) * LANES)
        y = _dot(h_ref[:, slab_cols(k)], wr_ref[k])
        ycf_ref[:, cs] = y[:, :LANES]
        ycb_ref[:, cs] = y[:, LANES:]

    lag = 1
    for k in range(S5_NK + lag):
        if k < S5_NK:
            drive(k)
            scan(k)
        if need_y and k >= lag:
            readout(k - lag)
    if need_y:
        yf = _dot(pft_ref[...], ycf_ref[...].astype(BF16))
        yb = _dot(pbt_ref[...], ycb_ref[...].astype(BF16))
        yf_ref[...] = yf.astype(BF16).reshape(BATCH, SCAN_T, SSM_WIDTH)
        yb_ref[...] = yb.astype(BF16).reshape(BATCH, SCAN_T, SSM_WIDTH)

    @pl.when(g == pl.num_programs(0) - 1)
    def _():
        ht_ref[...] = hc_ref[...]


def _scan_perms():
    t = SCAN_T
    pf = np.zeros((2 * BATCH * t, BATCH * t), np.float32)
    pb = np.zeros((2 * BATCH * t, BATCH * t), np.float32)
    for s in range(t):
        for b in range(BATCH):
            pf[s * 2 * BATCH + b, b * t + s] = 1.0
            pb[s * 2 * BATCH + BATCH + b, b * t + (t - 1 - s)] = 1.0
    return pf, pb


_PF, _PB = _scan_perms()


def _s5(u, h0, prm, layer, need_y):
    a, wd, wr = prm
    steps = u.shape[1]
    n = steps // SCAN_T
    rows_in = BATCH * SCAN_T
    rows_sc = 2 * rows_in
    pf = jnp.asarray(_PF, BF16)
    pb = jnp.asarray(_PB, BF16)
    const2 = lambda g: (0, 0)
    once = pl.Buffered(1)
    blk = (BATCH, SCAN_T, SSM_WIDTH)
    fwd_blk = pl.BlockSpec(blk, lambda g: (0, g, 0))
    bwd_blk = pl.BlockSpec(blk, lambda g: (0, n - 1 - g, 0))
    state_spec = pl.BlockSpec((2, SUBLANES, STATE_W), lambda g: (0, 0, 0))
    in_specs = [
        fwd_blk, bwd_blk, state_spec,
        pl.BlockSpec((rows_sc, rows_in), const2),
        pl.BlockSpec((rows_sc, rows_in), const2),
        pl.BlockSpec((rows_in, rows_sc), const2),
        pl.BlockSpec((rows_in, rows_sc), const2),
        pl.BlockSpec((None, 2, SUBLANES, STATE_W), lambda g: (layer, 0, 0, 0)),
        pl.BlockSpec((None, S5_NK, 2 * LANES, 2 * S5_KSTATE), lambda g: (layer, 0, 0, 0), pipeline_mode=once),
        pl.BlockSpec((None, S5_NK, 2 * S5_KSTATE, 2 * LANES), lambda g: (layer, 0, 0, 0), pipeline_mode=once),
    ]
    state_shape = jax.ShapeDtypeStruct((2, SUBLANES, STATE_W), F32)
    scratch = [pltpu.VMEM((rows_sc, 2 * STATE_W), F32), pltpu.VMEM((2, SUBLANES, STATE_W), F32)]
    if need_y:
        out_specs = [fwd_blk, bwd_blk, state_spec]
        y_shape = jax.ShapeDtypeStruct((BATCH, steps, SSM_WIDTH), BF16)
        out_shape = [y_shape, y_shape, state_shape]
        scratch += [pltpu.VMEM((rows_sc, 2 * STATE_W), BF16), pltpu.VMEM((rows_sc, SSM_WIDTH), F32),
                    pltpu.VMEM((rows_sc, SSM_WIDTH), F32)]
    else:
        out_specs = [state_spec]
        out_shape = [state_shape]
    return pl.pallas_call(
        functools.partial(_s5_kernel, need_y),
        grid=(n,),
        in_specs=in_specs,
        out_specs=out_specs,
        out_shape=out_shape,
        scratch_shapes=scratch,
        compiler_params=_params(("arbitrary",), 52),
        name="s5_scan",
    )(u, u, h0, pf, pb, pf.T, pb.T, a, wd, wr)


def _s5_params(lam_re, lam_im, log_step, b_re, b_im, c_re, c_im):
    dt = jnp.exp(log_step)[..., None]
    mag = jnp.exp(lam_re * dt)
    ang = lam_im * dt
    abar_re, abar_im = mag * jnp.cos(ang), mag * jnp.sin(ang)
    den = lam_re * lam_re + lam_im * lam_im
    nr, ni = abar_re - 1.0, abar_im
    coef_re = (nr * lam_re + ni * lam_im) / den
    coef_im = (ni * lam_re - nr * lam_im) / den
    bb_re = coef_re[..., None] * b_re - coef_im[..., None] * b_im
    bb_im = coef_re[..., None] * b_im + coef_im[..., None] * b_re
    bb = jnp.stack([bb_re, bb_im])
    bb = jnp.swapaxes(bb, -1, -2).reshape(-1, SSM_STATE)
    own = ((np.arange(bb.shape[0]) // SSM_GROUP) % S5_KGROUPS)[:, None] == (
        np.arange(S5_KSTATE) // SSM_STATE)[None, :]
    bb = jnp.where(own, jnp.tile(bb, (1, S5_KGROUPS)), 0.0)
    bb = bb.reshape(2, 2, S5_NK, LANES, S5_KSTATE)
    wd = jnp.concatenate([bb[0], bb[1]], axis=-1)
    wd = jnp.transpose(wd, (1, 0, 2, 3)).reshape(S5_NK, 2 * LANES, 2 * S5_KSTATE).astype(BF16)
    cc = jnp.stack([c_re, -c_im])
    cc = jnp.swapaxes(cc, -1, -2).reshape(-1, SSM_GROUP)
    own = ((np.arange(cc.shape[0]) // SSM_STATE) % S5_KGROUPS)[:, None] == (
        np.arange(LANES) // SSM_GROUP)[None, :]
    cc = jnp.where(own, jnp.tile(cc, (1, S5_KGROUPS)), 0.0)
    cc = cc.reshape(2, 2, S5_NK, S5_KSTATE, LANES)
    wr = jnp.concatenate([cc[:, 0], cc[:, 1]], axis=-1)
    wr = jnp.transpose(wr, (1, 0, 2, 3)).reshape(S5_NK, 2 * S5_KSTATE, 2 * LANES).astype(BF16)
    a = jnp.stack([abar_re.reshape(2, STATE_W), abar_im.reshape(2, STATE_W)])
    a = jnp.repeat(a, BATCH, axis=1)
    return a, wd, wr


def _dft_tables(n):
    j = np.arange(n, dtype=np.int64)
    ang = 2.0 * np.pi * ((j[:, None] * j[None, :]) % n).astype(np.float64) / n
    s = 1.0 / math.sqrt(n)
    return (np.cos(ang) * s).astype(np.float32), (np.sin(ang) * s).astype(np.float32)


_CH_COS, _CH_SIN = _dft_tables(FFT_GROUP)
_CH_CS = np.concatenate([_CH_COS, _CH_SIN], axis=1)


DFT_RADIX = 4
DFT_ILV = 256
_QUARTER_TURN = ((1, 0), (0, 1), (-1, 0), (0, -1))


def _radix_tables(n):
    nq = n // DFT_RADIX
    k = np.arange(nq, dtype=np.int64)
    s = 1.0 / math.sqrt(n)
    cos, sin = [], []
    for r in range(DFT_RADIX):
        j = DFT_RADIX * k + r
        ang = 2.0 * np.pi * ((j[:, None] * k[None, :]) % n).astype(np.float64) / n
        cos.append(np.cos(ang) * s)
        sin.append(np.sin(ang) * s)
    return np.stack(cos).astype(np.float32), np.stack(sin).astype(np.float32)


def _interleave_perm():
    per = DFT_ILV // DFT_RADIX
    perm = np.zeros((DFT_ILV, DFT_ILV), np.float32)
    for r in range(DFT_RADIX):
        for m in range(per):
            perm[DFT_RADIX * m + r, r * per + m] = 1.0
    return perm


_RADIX_TABLES = {n: _radix_tables(n) for n in (CTX_LEN, SEQ)}
_ILV_PERM = _interleave_perm()


def _fnet_kernel(n, u_ref, w_ref, c_ref, s_ref, perm_ref, o_ref, p_ref, q_ref, comb_ref, y_ref):
    nq = n // DFT_RADIX
    chunk = min(nq, EPILOGUE_ROWS)

    w = w_ref[...].astype(BF16)
    for g in range(FFT_GROUPS):
        cs = slice(g * FFT_GROUP, (g + 1) * FFT_GROUP)
        r = _dot(u_ref[:, cs], w)
        p_ref[:, cs] = r[:, :FFT_GROUP].astype(BF16)
        q_ref[:, cs] = r[:, FFT_GROUP:].astype(BF16)

    def combine(i, carry):
        rs = pl.ds(pl.multiple_of(i * chunk, chunk), chunk)
        p = [p_ref[pl.ds(pl.multiple_of(q * nq + i * chunk, chunk), chunk), :].astype(F32)
             for q in range(DFT_RADIX)]
        qq = [q_ref[pl.ds(pl.multiple_of(q * nq + i * chunk, chunk), chunk), :].astype(F32)
              for q in range(DFT_RADIX)]
        def signed_sum(terms):
            acc = None
            for sign, v in terms:
                if acc is None:
                    acc = v if sign > 0 else -v
                else:
                    acc = acc + v if sign > 0 else acc - v
            return acc

        for r in range(DFT_RADIX):
            pr, qr = [], []
            for q in range(DFT_RADIX):
                cs, sn = _QUARTER_TURN[(r * q) % DFT_RADIX]
                if cs:
                    pr.append((cs, p[q]))
                    qr.append((cs, qq[q]))
                if sn:
                    pr.append((-sn, qq[q]))
                    qr.append((sn, p[q]))
            comb_ref[2 * r, rs, :] = signed_sum(pr).astype(BF16)
            comb_ref[2 * r + 1, rs, :] = signed_sum(qr).astype(BF16)
        return carry

    lax.fori_loop(0, nq // chunk, combine, 0)
    for r in range(DFT_RADIX):
        y_ref[r] = (_dot(c_ref[r].astype(BF16), comb_ref[2 * r])
                    - _dot(s_ref[r].astype(BF16), comb_ref[2 * r + 1]))
    per = DFT_ILV // DFT_RADIX
    for blk in range(n // DFT_ILV):
        slab = jnp.concatenate([y_ref[r, blk * per:(blk + 1) * per, :] for r in range(DFT_RADIX)], axis=0)
        o_ref[blk * DFT_ILV:(blk + 1) * DFT_ILV, :] = _dot(perm_ref[...], slab.astype(BF16)).astype(BF16)


def _fourier_mix(rest):
    nb, steps, _ = rest.shape
    cos, sin = _RADIX_TABLES[steps]
    nq = steps // DFT_RADIX
    x_blk = pl.BlockSpec((None, steps, FFT_WIDTH), lambda b: (b, 0, 0))
    t_blk = pl.BlockSpec((DFT_RADIX, nq, nq), lambda b: (0, 0, 0), pipeline_mode=pl.Buffered(1))
    return pl.pallas_call(
        functools.partial(_fnet_kernel, steps),
        grid=(nb,),
        in_specs=[x_blk, pl.BlockSpec((FFT_GROUP, 2 * FFT_GROUP), lambda b: (0, 0)), t_blk, t_blk,
                  pl.BlockSpec((DFT_ILV, DFT_ILV), lambda b: (0, 0))],
        out_specs=x_blk,
        out_shape=jax.ShapeDtypeStruct((nb, steps, FFT_WIDTH), BF16),
        scratch_shapes=[pltpu.VMEM((steps, FFT_WIDTH), BF16), pltpu.VMEM((steps, FFT_WIDTH), BF16),
                        pltpu.VMEM((2 * DFT_RADIX, nq, FFT_WIDTH), BF16),
                        pltpu.VMEM((DFT_RADIX, nq, FFT_WIDTH), F32)],
        compiler_params=_params(("parallel",), 56),
        name="fourier_mix",
    )(rest, jnp.asarray(_CH_CS), jnp.asarray(cos), jnp.asarray(sin), jnp.asarray(_ILV_PERM, BF16))


def _glu_kernel(us_ref, yf_ref, yb_ref, dsk_ref, w_ref, o_ref):
    ys = dsk_ref[...] * us_ref[...] + yf_ref[...].astype(F32) + yb_ref[...].astype(F32)
    g = jax.nn.gelu(ys)
    z = _dot(g.astype(BF16), w_ref[...].astype(BF16))
    o_ref[...] = (g * jax.nn.sigmoid(z)).astype(BF16)


def _glu(us, yf, yb, d_skip, w_glu, layer):
    nb, rows, _ = us.shape
    tm = 1024
    blk = pl.BlockSpec((None, tm, SSM_WIDTH), lambda b, i: (b, i, 0))
    return pl.pallas_call(
        _glu_kernel,
        grid=(nb, rows // tm),
        in_specs=[blk, blk, blk,
                  pl.BlockSpec((None, 1, SSM_WIDTH), lambda b, i: (layer, 0, 0)),
                  pl.BlockSpec((None, SSM_WIDTH, SSM_WIDTH), lambda b, i: (layer, 0, 0))],
        out_specs=blk,
        out_shape=jax.ShapeDtypeStruct((nb, rows, SSM_WIDTH), BF16),
        compiler_params=_params(("parallel", "parallel"), 40),
        name="s5_glu",
    )(us, yf, yb, d_skip.reshape(DEPTH, 1, SSM_WIDTH), w_glu)


MERGE_SPLIT = 8


def _merge_kernel(s_ref, yq_ref, gs_ref, gf_ref, wps_ref, wpf_ref, o_ref):
    wps = wps_ref[...].astype(BF16)
    wpf = wpf_ref[...].astype(BF16)
    piece = o_ref.shape[0] // MERGE_SPLIT
    for r in range(MERGE_SPLIT):
        rs = slice(r * piece, (r + 1) * piece)
        ps = _dot(s_ref[rs, :], wps)
        pf = _dot(yq_ref[rs, :], wpf)
        o_ref[rs, :] = (jax.nn.sigmoid(gs_ref[rs, :].astype(F32)) * ps
                        + jax.nn.sigmoid(gf_ref[rs, :].astype(F32)) * pf).astype(BF16)


def _merge(s, yq, rest, w_ps, w_pf, layer):
    nb, rows, _ = s.shape
    tm = min(rows, 2048)
    tn = 512
    gs_off = FFT_WIDTH // tn
    gf_off = (FFT_WIDTH + D_MODEL) // tn
    row_blk = pl.BlockSpec((None, tm, SSM_WIDTH), lambda b, i, j: (b, i, 0))
    w_blk = pl.BlockSpec((None, SSM_WIDTH, tn), lambda b, i, j: (layer, 0, j))
    return pl.pallas_call(
        _merge_kernel,
        grid=(nb, rows // tm, D_MODEL // tn),
        in_specs=[
            row_blk, row_blk,
            pl.BlockSpec((None, tm, tn), lambda b, i, j: (b, i, gs_off + j)),
            pl.BlockSpec((None, tm, tn), lambda b, i, j: (b, i, gf_off + j)),
            w_blk, w_blk,
        ],
        out_specs=pl.BlockSpec((None, tm, tn), lambda b, i, j: (b, i, j)),
        out_shape=jax.ShapeDtypeStruct((nb, rows, D_MODEL), BF16),
        compiler_params=_params(("parallel", "parallel", "arbitrary"), 48),
        name="gated_merge",
    )(s, yq, rest, rest, w_ps, w_pf)


TAIL_TM = 1024
TAIL_TK = 512
TAIL_TF = 512


def _tail_kernel(layer, emit_next, merged_ref, x_hbm, wo_hbm, wu_hbm, wd_hbm, mod_ref, *refs):
    if emit_next:
        modn_ref, refs = refs[0], refs[1:]
    g1_ref, b1_ref, g2_ref, b2_ref, out_hbm = refs[:5]
    refs = refs[5:]
    if emit_next:
        hn_ref, refs = refs[0], refs[1:]
    xbuf, acc, h2, act, wa, wb, sem_a, sem_b, sem_x, sem_o = refs
    b = pl.program_id(0)
    i = pl.program_id(1)
    first = jnp.logical_and(b == 0, i == 0)
    last = jnp.logical_and(b == pl.num_programs(0) - 1, i == pl.num_programs(1) - 1)
    nk = D_MODEL // TAIL_TK
    nf = D_FF // TAIL_TF
    assert nk % 2 == 0
    rows = pl.ds(pl.multiple_of(i * TAIL_TM, TAIL_TM), TAIL_TM)

    def wo_copy(t, slot):
        return pltpu.make_async_copy(wo_hbm.at[layer, pl.ds(t * TAIL_TK, TAIL_TK), :], wa.at[slot], sem_a.at[slot])

    def wd_copy(f, slot):
        src = wd_hbm.at[layer, pl.ds(pl.multiple_of(f * TAIL_TK, TAIL_TK), TAIL_TK), :]
        return pltpu.make_async_copy(src, wa.at[slot], sem_a.at[slot])

    def wu_copy(f, slot):
        src = wu_hbm.at[layer, :, pl.ds(pl.multiple_of(f * TAIL_TF, TAIL_TF), TAIL_TF)]
        return pltpu.make_async_copy(src, wb.at[slot], sem_b.at[slot])

    def x_copy():
        return pltpu.make_async_copy(x_hbm.at[b, rows, :], xbuf, sem_x.at[0])

    def out_copy():
        return pltpu.make_async_copy(xbuf, out_hbm.at[b, rows, :], sem_o.at[0])

    def start_first_tiles():
        wo_copy(0, 0).start()
        wo_copy(1, 1).start()
        wu_copy(0, 0).start()
        wu_copy(1, 1).start()

    @pl.when(first)
    def _():
        start_first_tiles()

    for k in range(nk):
        slot = k % 2
        wo_copy(k, slot).wait()
        lhs = merged_ref[:, k * TAIL_TK:(k + 1) * TAIL_TK]
        if k == 0:
            acc[...] = _dot(lhs, wa[slot].astype(BF16))
        else:
            acc[...] += _dot(lhs, wa[slot].astype(BF16))
        if k + 2 < nk:
            wo_copy(k + 2, slot).start()
        else:
            wd_copy(k + 2 - nk, slot).start()
        if k == 1:
            @pl.when(jnp.logical_not(first))
            def _():
                out_copy().wait()

            x_copy().start()
    x_copy().wait()

    def finish1(rs):
        x1 = _deepnorm_ln(xbuf[rs, :], _row(mod_ref, GATE1), acc[rs, :]) * g1_ref[...] + b1_ref[...]
        xbuf[rs, :] = x1
        h2[rs, :] = _modulated_ln(x1, mod_ref, SHIFT2, SCALE2)

    _for_row_chunks(TAIL_TM, finish1)

    def up_tile(slot):
        a = jnp.maximum(_dot(h2[...], wb[slot].astype(BF16)), 0.0)
        act[slot] = (a * a).astype(BF16)

    def down_tile(slot, first=False):
        if first:
            acc[...] = _dot(act[slot], wa[slot].astype(BF16))
        else:
            acc[...] += _dot(act[slot], wa[slot].astype(BF16))

    def trip(f, slot):
        nxt = 1 - slot
        wu_copy(f + 1, nxt).wait()
        wd_copy(f, slot).wait()
        up_tile(nxt)
        down_tile(slot, first=isinstance(f, int) and f == 0)
        if isinstance(f, int):
            if f + 3 < nf:
                wu_copy(f + 3, nxt).start()
            if f + 2 < nf:
                wd_copy(f + 2, slot).start()
        else:
            @pl.when(f + 3 < nf)
            def _():
                wu_copy(f + 3, nxt).start()

            @pl.when(f + 2 < nf)
            def _():
                wd_copy(f + 2, slot).start()

    wu_copy(0, 0).wait()
    up_tile(0)
    wu_copy(2, 0).start()

    trip(0, 0)
    assert nf % 2 == 0

    def trip_pair(p, carry):
        trip(2 * p + 1, 1)
        trip(2 * p + 2, 0)
        return carry

    lax.fori_loop(0, (nf - 2) // 2, trip_pair, 0)

    def finish2(rs):
        x2 = _deepnorm_ln(xbuf[rs, :], _row(mod_ref, GATE2), acc[rs, :]) * g2_ref[...] + b2_ref[...]
        xbuf[rs, :] = x2
        if emit_next:
            hn_ref[rs, :] = _modulated_ln(x2, modn_ref, SHIFT1, SCALE1)

    @pl.when(jnp.logical_not(last))
    def _():
        wo_copy(0, 0).start()
        wu_copy(0, 0).start()
        wu_copy(1, 1).start()

    wd_copy(nf - 1, (nf - 1) % 2).wait()
    down_tile((nf - 1) % 2)

    @pl.when(jnp.logical_not(last))
    def _():
        wo_copy(1, 1).start()

    _for_row_chunks(TAIL_TM, finish2)
    out_copy().start()

    @pl.when(last)
    def _():
        out_copy().wait()


def _tail(merged, x, mod, w_o, w_up, w_down, g1, b1, g2, b2, layer, emit_next):
    nb, rows, _ = x.shape
    tm = TAIL_TM
    row_blk = pl.BlockSpec((None, tm, D_MODEL), lambda b_, i: (b_, i, 0))
    vec = pl.BlockSpec((None, 1, D_MODEL), lambda b_, i: (layer, 0, 0))
    hbm = pl.BlockSpec(memory_space=pl.ANY)
    in_specs = [row_blk, hbm, hbm, hbm, hbm, _mod_spec(layer, nb)]
    args = [merged, x, w_o, w_up, w_down, mod]
    if emit_next:
        in_specs.append(_mod_spec(layer + 1, nb))
        args.append(mod)
    in_specs += [vec, vec, vec, vec]
    args += [v.reshape(DEPTH, 1, D_MODEL) for v in (g1, b1, g2, b2)]
    out_specs = [hbm]
    out_shape = [jax.ShapeDtypeStruct(x.shape, F32)]
    if emit_next:
        out_specs.append(row_blk)
        out_shape.append(jax.ShapeDtypeStruct(x.shape, BF16))
    out = pl.pallas_call(
        functools.partial(_tail_kernel, layer, emit_next),
        grid=(nb, rows // tm),
        in_specs=in_specs,
        out_specs=out_specs,
        out_shape=out_shape,
        scratch_shapes=[
            pltpu.VMEM((tm, D_MODEL), F32),
            pltpu.VMEM((tm, D_MODEL), F32),
            pltpu.VMEM((tm, D_MODEL), BF16),
            pltpu.VMEM((2, tm, TAIL_TF), BF16),
            pltpu.VMEM((2, TAIL_TK, D_MODEL), F32),
            pltpu.VMEM((2, D_MODEL, TAIL_TF), F32),
            pltpu.SemaphoreType.DMA((2,)),
            pltpu.SemaphoreType.DMA((2,)),
            pltpu.SemaphoreType.DMA((1,)),
            pltpu.SemaphoreType.DMA((1,)),
        ],
        compiler_params=_params(("arbitrary", "arbitrary"), 58),
        name="layer_tail",
    )(*args)
    return out if emit_next else (out[0], None)


def _pos_table():
    quarter = D_MODEL // 4
    omega = 1.0 / (POS_BASE ** (np.arange(quarter, dtype=np.float64) / quarter))
    t = np.arange(SEQ)
    ar = (t // GRID_W).astype(np.float64)[:, None] * omega
    ac = (t % GRID_W).astype(np.float64)[:, None] * omega
    return np.concatenate([np.sin(ar), np.cos(ar), np.sin(ac), np.cos(ac)], axis=-1).astype(np.float32)


_POS = _pos_table()


def _mod_tables(m):
    m = m.reshape(DEPTH, SUBLANES, 6, D_MODEL)
    m = jnp.pad(m, ((0, 0), (0, 0), (0, SUBLANES - 6), (0, 0)))
    return m[:, :BATCH], m[:, BATCH:BATCH + 1]


def _as_batch(a):
    return a.reshape(BATCH, CTX_LEN, a.shape[-1])


def _as_slab(a):
    return a.reshape(1, BATCH * CTX_LEN, a.shape[-1])


def kernel(x, c, ctx, c_ctx, w_mod, b_mod, w_in, lam_re, lam_im, log_step, ssm_b_re, ssm_b_im, ssm_c_re,
           ssm_c_im, d_skip, w_glu, w_ps, w_pf, w_o, ln1_g, ln1_b, w_up, w_down, ln2_g, ln2_b):
    cond8 = jnp.concatenate([c, c_ctx[None], jnp.zeros((SUBLANES - BATCH - 1, D_MODEL), F32)], axis=0)
    mod_lat, mod_ctx = _mod_tables(_modulation(cond8, w_mod, b_mod))

    x_lat, h_lat = _entry(x, jnp.asarray(_POS), mod_lat)
    x_ctx, h_ctx = _entry(_as_slab(ctx), None, mod_ctx)
    h_zero = jnp.zeros((2, SUBLANES, STATE_W), F32)
    prm = jax.vmap(_s5_params)(lam_re, lam_im, log_step, ssm_b_re, ssm_b_im, ssm_c_re, ssm_c_im)

    for l in range(DEPTH):
        need_ctx = l < DEPTH - 1

        def mixer_tail(us, rest, yf, yb, xx, mod, as_batch, as_rows):
            yq = as_rows(_fourier_mix(as_batch(rest)))
            s = _glu(us, as_rows(yf), as_rows(yb), d_skip, w_glu, l)
            merged = _merge(s, yq, rest, w_ps, w_pf, l)
            return _tail(merged, xx, mod, w_o, w_up, w_down, ln1_g, ln1_b, ln2_g, ln2_b, l, need_ctx)

        us_lat, rest_lat = _in_proj(h_lat, w_in, l, True)
        if need_ctx:
            us_ctx, rest_ctx = _in_proj(h_ctx, w_in, l, True)
            yf_c, yb_c, h_t = _s5(_as_batch(us_ctx), h_zero, prm, l, True)
        else:
            (us_ctx,) = _in_proj(h_ctx, w_in, l, False)
            (h_t,) = _s5(_as_batch(us_ctx), h_zero, prm, l, False)
        yf, yb, _ = _s5(us_lat, h_t, prm, l, True)
        ident = lambda a: a
        x_lat, h_lat = mixer_tail(us_lat, rest_lat, yf, yb, x_lat, mod_lat, ident, ident)
        if need_ctx:
            x_ctx, h_ctx = mixer_tail(us_ctx, rest_ctx, yf_c, yb_c, x_ctx, mod_ctx, _as_batch, _as_slab)

    return x_lat
```

```python
import functools
import math

import numpy as np
import jax
import jax.numpy as jnp
from jax import lax
from jax.experimental import pallas as pl
from jax.experimental.pallas import tpu as pltpu

D_MODEL = 2048
BATCH = 4
SEQ = 2048
DEPTH = 2
GRID_W = 64
CTX_LEN = 256
SSM_WIDTH = D_MODEL // 2
SSM_GROUP = 16
SSM_GROUPS = SSM_WIDTH // SSM_GROUP
SSM_STATE = 64
FFT_WIDTH = D_MODEL - SSM_WIDTH
FFT_GROUPS = 4
FFT_GROUP = FFT_WIDTH // FFT_GROUPS
IN_WIDTH = SSM_WIDTH + FFT_WIDTH + 2 * D_MODEL
REST_WIDTH = IN_WIDTH - SSM_WIDTH
D_FF = 4 * D_MODEL
ALPHA = (2 * DEPTH) ** 0.25
LN_EPS = 1e-5
POS_BASE = 10000.0

F32 = jnp.float32
BF16 = jnp.bfloat16

SUBLANES = 8
LANES = 128
STATE_W = SSM_GROUPS * SSM_STATE
SCAN_T = 64
S5_KGROUPS = 8
S5_NK = SSM_GROUPS // S5_KGROUPS
S5_KSTATE = S5_KGROUPS * SSM_STATE
MIB = 1024 * 1024

SHIFT1, SCALE1, GATE1, SHIFT2, SCALE2, GATE2 = range(6)


def _params(sem, vmem_mib):
    return pltpu.CompilerParams(dimension_semantics=sem, vmem_limit_bytes=vmem_mib * MIB)


def _dot(a, b):
    return jnp.dot(a, b, preferred_element_type=F32)


def _ln(x, eps=LN_EPS):
    mu = jnp.mean(x, axis=-1, keepdims=True)
    xc = x - mu
    var = jnp.mean(xc * xc, axis=-1, keepdims=True)
    return xc * lax.rsqrt(var + eps)


def _deepnorm_ln(x, gate, branch):
    return _ln(x + (gate * (1.0 / ALPHA)) * branch, LN_EPS / (ALPHA * ALPHA))


def _row(ref, j):
    return ref[j:j + 1, :]


def _modulated_ln(x, mod_ref, shift, scale):
    return (_ln(x) * (1.0 + _row(mod_ref, scale)) + _row(mod_ref, shift)).astype(BF16)


EPILOGUE_ROWS = 128


def _for_row_chunks(rows, fn):
    def body(r, carry):
        fn(pl.ds(pl.multiple_of(r * EPILOGUE_ROWS, EPILOGUE_ROWS), EPILOGUE_ROWS))
        return carry

    lax.fori_loop(0, rows // EPILOGUE_ROWS, body, 0)


def _mod_spec(layer, nb):
    if nb == 1:
        return pl.BlockSpec((None, None, SUBLANES, D_MODEL), lambda b, *_: (layer, 0, 0, 0))
    return pl.BlockSpec((None, None, SUBLANES, D_MODEL), lambda b, *_: (layer, b, 0, 0))


def _mod_kernel(c_ref, w_ref, b_ref, o_ref):
    c = c_ref[...]
    sc = c * jax.nn.sigmoid(c)
    o_ref[...] = _dot(sc.astype(BF16), w_ref[...].astype(BF16)) + b_ref[...]


def _modulation(cond8, w_mod, b_mod):
    tn = 1024
    n = 6 * D_MODEL
    return pl.pallas_call(
        _mod_kernel,
        grid=(DEPTH, n // tn),
        in_specs=[
            pl.BlockSpec((SUBLANES, D_MODEL), lambda l, j: (0, 0)),
            pl.BlockSpec((None, D_MODEL, tn), lambda l, j: (l, 0, j)),
            pl.BlockSpec((None, 1, tn), lambda l, j: (l, 0, j)),
        ],
        out_specs=pl.BlockSpec((None, SUBLANES, tn), lambda l, j: (l, 0, j)),
        out_shape=jax.ShapeDtypeStruct((DEPTH, SUBLANES, n), F32),
        compiler_params=_params(("parallel", "parallel"), 40),
        name="adaln_modulation",
    )(cond8, w_mod, b_mod.reshape(DEPTH, 1, n))


def _entry_kernel(has_pos, *refs):
    if has_pos:
        x_ref, p_ref, mod_ref, xo_ref, h_ref = refs
        x = x_ref[...] + p_ref[...]
        xo_ref[...] = x
    else:
        x_ref, mod_ref, h_ref = refs
        x = x_ref[...]
    h_ref[...] = _modulated_ln(x, mod_ref, SHIFT1, SCALE1)


def _entry(x, pos, mod):
    nb, rows, _ = x.shape
    tm = 512
    has_pos = pos is not None
    blk = pl.BlockSpec((None, tm, D_MODEL), lambda i, b: (b, i, 0))
    in_specs = [blk]
    args = [x]
    if has_pos:
        in_specs.append(pl.BlockSpec((tm, D_MODEL), lambda i, b: (i, 0)))
        args.append(pos)
    in_specs.append(pl.BlockSpec((None, None, SUBLANES, D_MODEL), lambda i, b: (0, b, 0, 0)))
    args.append(mod)
    h_shape = jax.ShapeDtypeStruct(x.shape, BF16)
    out = pl.pallas_call(
        functools.partial(_entry_kernel, has_pos),
        grid=(rows // tm, nb),
        in_specs=in_specs,
        out_specs=[blk, blk] if has_pos else [blk],
        out_shape=[jax.ShapeDtypeStruct(x.shape, F32), h_shape] if has_pos else [h_shape],
        compiler_params=_params(("parallel", "parallel"), 40),
        name="entry_ln",
    )(*args)
    return out if has_pos else (x, out[0])


ROW_SPLIT = 2


def _win_kernel(n_us, with_rest, h_ref, w_ref, us_ref, *rest):
    piece = h_ref.shape[0] // ROW_SPLIT

    def project(o_ref):
        w = w_ref[...].astype(BF16)
        for r in range(ROW_SPLIT):
            rs = slice(r * piece, (r + 1) * piece)
            o_ref[rs, :] = _dot(h_ref[rs, :], w).astype(o_ref.dtype)

    if not with_rest:
        project(us_ref)
        return
    rest_ref, = rest
    j = pl.program_id(2)

    @pl.when(j < n_us)
    def _():
        project(us_ref)

    @pl.when(j >= n_us)
    def _():
        project(rest_ref)


def _in_proj(h, w_in, layer, with_rest):
    nb, rows, _ = h.shape
    tm = min(rows, 2048)
    tn = 512
    n_us = SSM_WIDTH // tn
    n_tiles = (IN_WIDTH if with_rest else SSM_WIDTH) // tn
    out_specs = [pl.BlockSpec((None, tm, tn), lambda b, i, j: (b, i, jnp.minimum(j, n_us - 1)))]
    out_shape = [jax.ShapeDtypeStruct((nb, rows, SSM_WIDTH), F32)]
    if with_rest:
        out_specs.append(pl.BlockSpec((None, tm, tn), lambda b, i, j: (b, i, jnp.maximum(j - n_us, 0))))
        out_shape.append(jax.ShapeDtypeStruct((nb, rows, REST_WIDTH), BF16))
    return pl.pallas_call(
        functools.partial(_win_kernel, n_us, with_rest),
        grid=(nb, rows // tm, n_tiles),
        in_specs=[
            pl.BlockSpec((None, tm, D_MODEL), lambda b, i, j: (b, i, 0)),
            pl.BlockSpec((None, D_MODEL, tn), lambda b, i, j: (layer, 0, j)),
        ],
        out_specs=out_specs,
        out_shape=out_shape,
        compiler_params=_params(("parallel", "parallel", "arbitrary"), 48),
        name="in_proj",
    )(h, w_in)


def _s5_kernel(need_y, uf_ref, ub_ref, h0_ref, pf_ref, pb_ref, pft_ref, pbt_ref, a_ref, wd_ref, wr_ref,
               *rest):
    if need_y:
        yf_ref, yb_ref, ht_ref, v_ref, hc_ref, h_ref, ycf_ref, ycb_ref = rest
    else:
        ht_ref, v_ref, hc_ref = rest
    g = pl.program_id(0)
    rows_in = BATCH * SCAN_T
    pair = 2 * SUBLANES

    @pl.when(g == 0)
    def _():
        hc_ref[...] = h0_ref[...]

    uf = uf_ref[...].reshape(rows_in, SSM_WIDTH).astype(BF16)
    ub = ub_ref[...].reshape(rows_in, SSM_WIDTH).astype(BF16)
    up_f = _dot(pf_ref[...], uf).astype(BF16)
    up_b = _dot(pb_ref[...], ub).astype(BF16)

    def slab_cols(k):
        return slice(2 * k * S5_KSTATE, (2 * k + 2) * S5_KSTATE)

    def drive(k):
        cs = slice(k * LANES, (k + 1) * LANES)
        lhs = jnp.concatenate([up_f[:, cs], up_b[:, cs]], axis=1)
        v_ref[:, slab_cols(k)] = _dot(lhs, wd_ref[k])

    def scan(k):
        ss = slice(k * S5_KSTATE, (k + 1) * S5_KSTATE)
        re_cols = slice(2 * k * S5_KSTATE, (2 * k + 1) * S5_KSTATE)
        im_cols = slice((2 * k + 1) * S5_KSTATE, (2 * k + 2) * S5_KSTATE)
        ar = a_ref[0, :, ss]
        ai = a_ref[1, :, ss]
        hr = hc_ref[0, :, ss]
        hi = hc_ref[1, :, ss]
        for s2 in range(SCAN_T // 2):
            rows = []
            for s in (2 * s2, 2 * s2 + 1):
                rs = slice(s * SUBLANES, (s + 1) * SUBLANES)
                nr = ar * hr - ai * hi + v_ref[rs, re_cols]
                ni = ar * hi + ai * hr + v_ref[rs, im_cols]
                hr, hi = nr, ni
                rows.append((nr, ni))
            if need_y:
                ps = slice(s2 * pair, (s2 + 1) * pair)
                h_ref[ps, re_cols] = jnp.concatenate([rows[0][0], rows[1][0]], axis=0).astype(BF16)
                h_ref[ps, im_cols] = jnp.concatenate([rows[0][1], rows[1][1]], axis=0).astype(BF16)
        hc_ref[0, :, ss] = hr
        hc_ref[1, :, ss] = hi

    def readout(k):
        cs = slice(k * LANES, (k + 1) * LANES)
        y = _dot(h_ref[:, slab_cols(k)], wr_ref[k])
        ycf_ref[:, cs] = y[:, :LANES]
        ycb_ref[:, cs] = y[:, LANES:]

    lag = 1
    for k in range(S5_NK + lag):
        if k < S5_NK:
            drive(k)
            scan(k)
        if need_y and k >= lag:
            readout(k - lag)
    if need_y:
        yf = _dot(pft_ref[...], ycf_ref[...].astype(BF16))
        yb = _dot(pbt_ref[...], ycb_ref[...].astype(BF16))
        yf_ref[...] = yf.astype(BF16).reshape(BATCH, SCAN_T, SSM_WIDTH)
        yb_ref[...] = yb.astype(BF16).reshape(BATCH, SCAN_T, SSM_WIDTH)

    @pl.when(g == pl.num_programs(0) - 1)
    def _():
        ht_ref[...] = hc_ref[...]


def _scan_perms():
    t = SCAN_T
    pf = np.zeros((2 * BATCH * t, BATCH * t), np.float32)
    pb = np.zeros((2 * BATCH * t, BATCH * t), np.float32)
    for s in range(t):
        for b in range(BATCH):
            pf[s * 2 * BATCH + b, b * t + s] = 1.0
            pb[s * 2 * BATCH + BATCH + b, b * t + (t - 1 - s)] = 1.0
    return pf, pb


_PF, _PB = _scan_perms()


def _s5(u, h0, prm, layer, need_y):
    a, wd, wr = prm
    steps = u.shape[1]
    n = steps // SCAN_T
    rows_in = BATCH * SCAN_T
    rows_sc = 2 * rows_in
    pf = jnp.asarray(_PF, BF16)
    pb = jnp.asarray(_PB, BF16)
    const2 = lambda g: (0, 0)
    once = pl.Buffered(1)
    blk = (BATCH, SCAN_T, SSM_WIDTH)
    fwd_blk = pl.BlockSpec(blk, lambda g: (0, g, 0))
    bwd_blk = pl.BlockSpec(blk, lambda g: (0, n - 1 - g, 0))
    state_spec = pl.BlockSpec((2, SUBLANES, STATE_W), lambda g: (0, 0, 0))
    in_specs = [
        fwd_blk, bwd_blk, state_spec,
        pl.BlockSpec((rows_sc, rows_in), const2),
        pl.BlockSpec((rows_sc, rows_in), const2),
        pl.BlockSpec((rows_in, rows_sc), const2),
        pl.BlockSpec((rows_in, rows_sc), const2),
        pl.BlockSpec((None, 2, SUBLANES, STATE_W), lambda g: (layer, 0, 0, 0)),
        pl.BlockSpec((None, S5_NK, 2 * LANES, 2 * S5_KSTATE), lambda g: (layer, 0, 0, 0), pipeline_mode=once),
        pl.BlockSpec((None, S5_NK, 2 * S5_KSTATE, 2 * LANES), lambda g: (layer, 0, 0, 0), pipeline_mode=once),
    ]
    state_shape = jax.ShapeDtypeStruct((2, SUBLANES, STATE_W), F32)
    scratch = [pltpu.VMEM((rows_sc, 2 * STATE_W), F32), pltpu.VMEM((2, SUBLANES, STATE_W), F32)]
    if need_y:
        out_specs = [fwd_blk, bwd_blk, state_spec]
        y_shape = jax.ShapeDtypeStruct((BATCH, steps, SSM_WIDTH), BF16)
        out_shape = [y_shape, y_shape, state_shape]
        scratch += [pltpu.VMEM((rows_sc, 2 * STATE_W), BF16), pltpu.VMEM((rows_sc, SSM_WIDTH), F32),
                    pltpu.VMEM((rows_sc, SSM_WIDTH), F32)]
    else:
        out_specs = [state_spec]
        out_shape = [state_shape]
    return pl.pallas_call(
        functools.partial(_s5_kernel, need_y),
        grid=(n,),
        in_specs=in_specs,
        out_specs=out_specs,
        out_shape=out_shape,
        scratch_shapes=scratch,
        compiler_params=_params(("arbitrary",), 52),
        name="s5_scan",
    )(u, u, h0, pf, pb, pf.T, pb.T, a, wd, wr)


def _s5_params(lam_re, lam_im, log_step, b_re, b_im, c_re, c_im):
    dt = jnp.exp(log_step)[..., None]
    mag = jnp.exp(lam_re * dt)
    ang = lam_im * dt
    abar_re, abar_im = mag * jnp.cos(ang), mag * jnp.sin(ang)
    den = lam_re * lam_re + lam_im * lam_im
    nr, ni = abar_re - 1.0, abar_im
    coef_re = (nr * lam_re + ni * lam_im) / den
    coef_im = (ni * lam_re - nr * lam_im) / den
    bb_re = coef_re[..., None] * b_re - coef_im[..., None] * b_im
    bb_im = coef_re[..., None] * b_im + coef_im[..., None] * b_re
    bb = jnp.stack([bb_re, bb_im])
    bb = jnp.swapaxes(bb, -1, -2).reshape(-1, SSM_STATE)
    own = ((np.arange(bb.shape[0]) // SSM_GROUP) % S5_KGROUPS)[:, None] == (
        np.arange(S5_KSTATE) // SSM_STATE)[None, :]
    bb = jnp.where(own, jnp.tile(bb, (1, S5_KGROUPS)), 0.0)
    bb = bb.reshape(2, 2, S5_NK, LANES, S5_KSTATE)
    wd = jnp.concatenate([bb[0], bb[1]], axis=-1)
    wd = jnp.transpose(wd, (1, 0, 2, 3)).reshape(S5_NK, 2 * LANES, 2 * S5_KSTATE).astype(BF16)
    cc = jnp.stack([c_re, -c_im])
    cc = jnp.swapaxes(cc, -1, -2).reshape(-1, SSM_GROUP)
    own = ((np.arange(cc.shape[0]) // SSM_STATE) % S5_KGROUPS)[:, None] == (
        np.arange(LANES) // SSM_GROUP)[None, :]
    cc = jnp.where(own, jnp.tile(cc, (1, S5_KGROUPS)), 0.0)
    cc = cc.reshape(2, 2, S5_NK, S5_KSTATE, LANES)
    wr = jnp.concatenate([cc[:, 0], cc[:, 1]], axis=-1)
    wr = jnp.transpose(wr, (1, 0, 2, 3)).reshape(S5_NK, 2 * S5_KSTATE, 2 * LANES).astype(BF16)
    a = jnp.stack([abar_re.reshape(2, STATE_W), abar_im.reshape(2, STATE_W)])
    a = jnp.repeat(a, BATCH, axis=1)
    return a, wd, wr


def _dft_tables(n):
    j = np.arange(n, dtype=np.int64)
    ang = 2.0 * np.pi * ((j[:, None] * j[None, :]) % n).astype(np.float64) / n
    s = 1.0 / math.sqrt(n)
    return (np.cos(ang) * s).astype(np.float32), (np.sin(ang) * s).astype(np.float32)


_CH_COS, _CH_SIN = _dft_tables(FFT_GROUP)
_CH_CS = np.concatenate([_CH_COS, _CH_SIN], axis=1)


DFT_RADIX = 4
DFT_ILV = 256
_QUARTER_TURN = ((1, 0), (0, 1), (-1, 0), (0, -1))


def _radix_tables(n):
    nq = n // DFT_RADIX
    k = np.arange(nq, dtype=np.int64)
    s = 1.0 / math.sqrt(n)
    cos, sin = [], []
    for r in range(DFT_RADIX):
        j = DFT_RADIX * k + r
        ang = 2.0 * np.pi * ((j[:, None] * k[None, :]) % n).astype(np.float64) / n
        cos.append(np.cos(ang) * s)
        sin.append(np.sin(ang) * s)
    return np.stack(cos).astype(np.float32), np.stack(sin).astype(np.float32)


def _interleave_perm():
    per = DFT_ILV // DFT_RADIX
    perm = np.zeros((DFT_ILV, DFT_ILV), np.float32)
    for r in range(DFT_RADIX):
        for m in range(per):
            perm[DFT_RADIX * m + r, r * per + m] = 1.0
    return perm


_RADIX_TABLES = {n: _radix_tables(n) for n in (CTX_LEN, SEQ)}
_ILV_PERM = _interleave_perm()


def _fnet_kernel(n, u_ref, w_ref, c_ref, s_ref, perm_ref, o_ref, p_ref, q_ref, comb_ref, y_ref):
    nq = n // DFT_RADIX
    chunk = min(nq, EPILOGUE_ROWS)

    w = w_ref[...].astype(BF16)
    for g in range(FFT_GROUPS):
        cs = slice(g * FFT_GROUP, (g + 1) * FFT_GROUP)
        r = _dot(u_ref[:, cs], w)
        p_ref[:, cs] = r[:, :FFT_GROUP].astype(BF16)
        q_ref[:, cs] = r[:, FFT_GROUP:].astype(BF16)

    def combine(i, carry):
        rs = pl.ds(pl.multiple_of(i * chunk, chunk), chunk)
        p = [p_ref[pl.ds(pl.multiple_of(q * nq + i * chunk, chunk), chunk), :].astype(F32)
             for q in range(DFT_RADIX)]
        qq = [q_ref[pl.ds(pl.multiple_of(q * nq + i * chunk, chunk), chunk), :].astype(F32)
              for q in range(DFT_RADIX)]
        def signed_sum(terms):
            acc = None
            for sign, v in terms:
                if acc is None:
                    acc = v if sign > 0 else -v
                else:
                    acc = acc + v if sign > 0 else acc - v
            return acc

        for r in range(DFT_RADIX):
            pr, qr = [], []
            for q in range(DFT_RADIX):
                cs, sn = _QUARTER_TURN[(r * q) % DFT_RADIX]
                if cs:
                    pr.append((cs, p[q]))
                    qr.append((cs, qq[q]))
                if sn:
                    pr.append((-sn, qq[q]))
                    qr.append((sn, p[q]))
            comb_ref[2 * r, rs, :] = signed_sum(pr).astype(BF16)
            comb_ref[2 * r + 1, rs, :] = signed_sum(qr).astype(BF16)
        return carry

    lax.fori_loop(0, nq // chunk, combine, 0)
    for r in range(DFT_RADIX):
        y_ref[r] = (_dot(c_ref[r].astype(BF16), comb_ref[2 * r])
                    - _dot(s_ref[r].astype(BF16), comb_ref[2 * r + 1]))
    per = DFT_ILV // DFT_RADIX
    for blk in range(n // DFT_ILV):
        slab = jnp.concatenate([y_ref[r, blk * per:(blk + 1) * per, :] for r in range(DFT_RADIX)], axis=0)
        o_ref[blk * DFT_ILV:(blk + 1) * DFT_ILV, :] = _dot(perm_ref[...], slab.astype(BF16)).astype(BF16)


def _fourier_mix(rest):
    nb, steps, _ = rest.shape
    cos, sin = _RADIX_TABLES[steps]
    nq = steps // DFT_RADIX
    x_blk = pl.BlockSpec((None, steps, FFT_WIDTH), lambda b: (b, 0, 0))
    t_blk = pl.BlockSpec((DFT_RADIX, nq, nq), lambda b: (0, 0, 0), pipeline_mode=pl.Buffered(1))
    return pl.pallas_call(
        functools.partial(_fnet_kernel, steps),
        grid=(nb,),
        in_specs=[x_blk, pl.BlockSpec((FFT_GROUP, 2 * FFT_GROUP), lambda b: (0, 0)), t_blk, t_blk,
                  pl.BlockSpec((DFT_ILV, DFT_ILV), lambda b: (0, 0))],
        out_specs=x_blk,
        out_shape=jax.ShapeDtypeStruct((nb, steps, FFT_WIDTH), BF16),
        scratch_shapes=[pltpu.VMEM((steps, FFT_WIDTH), BF16), pltpu.VMEM((steps, FFT_WIDTH), BF16),
                        pltpu.VMEM((2 * DFT_RADIX, nq, FFT_WIDTH), BF16),
                        pltpu.VMEM((DFT_RADIX, nq, FFT_WIDTH), F32)],
        compiler_params=_params(("parallel",), 56),
        name="fourier_mix",
    )(rest, jnp.asarray(_CH_CS), jnp.asarray(cos), jnp.asarray(sin), jnp.asarray(_ILV_PERM, BF16))


def _glu_kernel(us_ref, yf_ref, yb_ref, dsk_ref, w_ref, o_ref):
    ys = dsk_ref[...] * us_ref[...] + yf_ref[...].astype(F32) + yb_ref[...].astype(F32)
    g = jax.nn.gelu(ys)
    z = _dot(g.astype(BF16), w_ref[...].astype(BF16))
    o_ref[...] = (g * jax.nn.sigmoid(z)).astype(BF16)


def _glu(us, yf, yb, d_skip, w_glu, layer):
    nb, rows, _ = us.shape
    tm = 1024
    blk = pl.BlockSpec((None, tm, SSM_WIDTH), lambda b, i: (b, i, 0))
    return pl.pallas_call(
        _glu_kernel,
        grid=(nb, rows // tm),
        in_specs=[blk, blk, blk,
                  pl.BlockSpec((None, 1, SSM_WIDTH), lambda b, i: (layer, 0, 0)),
                  pl.BlockSpec((None, SSM_WIDTH, SSM_WIDTH), lambda b, i: (layer, 0, 0))],
        out_specs=blk,
        out_shape=jax.ShapeDtypeStruct((nb, rows, SSM_WIDTH), BF16),
        compiler_params=_params(("parallel", "parallel"), 40),
        name="s5_glu",
    )(us, yf, yb, d_skip.reshape(DEPTH, 1, SSM_WIDTH), w_glu)


MERGE_SPLIT = 8


def _merge_kernel(s_ref, yq_ref, gs_ref, gf_ref, wps_ref, wpf_ref, o_ref):
    wps = wps_ref[...]
    wpf = wpf_ref[...]
    piece = o_ref.shape[0] // MERGE_SPLIT
    for r in range(MERGE_SPLIT):
        rs = slice(r * piece, (r + 1) * piece)
        ps = _dot(s_ref[rs, :], wps)
        pf = _dot(yq_ref[rs, :], wpf)
        o_ref[rs, :] = (jax.nn.sigmoid(gs_ref[rs, :].astype(F32)) * ps
                        + jax.nn.sigmoid(gf_ref[rs, :].astype(F32)) * pf).astype(BF16)


def _merge(s, yq, rest, w_ps, w_pf, layer):
    nb, rows, _ = s.shape
    tm = min(rows, 2048)
    tn = 512
    gs_off = FFT_WIDTH // tn
    gf_off = (FFT_WIDTH + D_MODEL) // tn
    row_blk = pl.BlockSpec((None, tm, SSM_WIDTH), lambda b, i, j: (b, i, 0))
    w_blk = pl.BlockSpec((None, SSM_WIDTH, tn), lambda b, i, j: (layer, 0, j))
    return pl.pallas_call(
        _merge_kernel,
        grid=(nb, rows // tm, D_MODEL // tn),
        in_specs=[
            row_blk, row_blk,
            pl.BlockSpec((None, tm, tn), lambda b, i, j: (b, i, gs_off + j)),
            pl.BlockSpec((None, tm, tn), lambda b, i, j: (b, i, gf_off + j)),
            w_blk, w_blk,
        ],
        out_specs=pl.BlockSpec((None, tm, tn), lambda b, i, j: (b, i, j)),
        out_shape=jax.ShapeDtypeStruct((nb, rows, D_MODEL), BF16),
        compiler_params=_params(("parallel", "parallel", "arbitrary"), 48),
        name="gated_merge",
    )(s, yq, rest, rest, w_ps, w_pf)


TAIL_TM = 1024
TAIL_TK = 512
TAIL_TF = 512


def _tail_kernel(layer, emit_next, merged_ref, x_hbm, wo_hbm, wu_hbm, wd_hbm, mod_ref, *refs):
    if emit_next:
        modn_ref, refs = refs[0], refs[1:]
    g1_ref, b1_ref, g2_ref, b2_ref, out_hbm = refs[:5]
    refs = refs[5:]
    if emit_next:
        hn_ref, refs = refs[0], refs[1:]
    xbuf, acc, h2, act, wa, wb, sem_a, sem_b, sem_x, sem_o = refs
    b = pl.program_id(0)
    i = pl.program_id(1)
    first = jnp.logical_and(b == 0, i == 0)
    last = jnp.logical_and(b == pl.num_programs(0) - 1, i == pl.num_programs(1) - 1)
    nk = D_MODEL // TAIL_TK
    nf = D_FF // TAIL_TF
    assert nk % 2 == 0
    rows = pl.ds(pl.multiple_of(i * TAIL_TM, TAIL_TM), TAIL_TM)

    def wo_copy(t, slot):
        return pltpu.make_async_copy(wo_hbm.at[layer, pl.ds(t * TAIL_TK, TAIL_TK), :], wa.at[slot], sem_a.at[slot])

    def wd_copy(f, slot):
        src = wd_hbm.at[layer, pl.ds(pl.multiple_of(f * TAIL_TK, TAIL_TK), TAIL_TK), :]
        return pltpu.make_async_copy(src, wa.at[slot], sem_a.at[slot])

    def wu_copy(f, slot):
        src = wu_hbm.at[layer, :, pl.ds(pl.multiple_of(f * TAIL_TF, TAIL_TF), TAIL_TF)]
        return pltpu.make_async_copy(src, wb.at[slot], sem_b.at[slot])

    def x_copy():
        return pltpu.make_async_copy(x_hbm.at[b, rows, :], xbuf, sem_x.at[0])

    def out_copy():
        return pltpu.make_async_copy(xbuf, out_hbm.at[b, rows, :], sem_o.at[0])

    def start_first_tiles():
        wo_copy(0, 0).start()
        wo_copy(1, 1).start()
        wu_copy(0, 0).start()
        wu_copy(1, 1).start()

    @pl.when(first)
    def _():
        start_first_tiles()

    for k in range(nk):
        slot = k % 2
        wo_copy(k, slot).wait()
        lhs = merged_ref[:, k * TAIL_TK:(k + 1) * TAIL_TK]
        if k == 0:
            acc[...] = _dot(lhs, wa[slot].astype(BF16))
        else:
            acc[...] += _dot(lhs, wa[slot].astype(BF16))
        if k + 2 < nk:
            wo_copy(k + 2, slot).start()
        else:
            wd_copy(k + 2 - nk, slot).start()
        if k == 1:
            @pl.when(jnp.logical_not(first))
            def _():
                out_copy().wait()

            x_copy().start()
    x_copy().wait()

    def finish1(rs):
        x1 = _deepnorm_ln(xbuf[rs, :], _row(mod_ref, GATE1), acc[rs, :]) * g1_ref[...] + b1_ref[...]
        xbuf[rs, :] = x1
        h2[rs, :] = _modulated_ln(x1, mod_ref, SHIFT2, SCALE2)

    _for_row_chunks(TAIL_TM, finish1)

    def up_tile(slot):
        a = jnp.maximum(_dot(h2[...], wb[slot].astype(BF16)), 0.0)
        act[slot] = (a * a).astype(BF16)

    def down_tile(slot, first=False):
        if first:
            acc[...] = _dot(act[slot], wa[slot].astype(BF16))
        else:
            acc[...] += _dot(act[slot], wa[slot].astype(BF16))

    def trip(f, slot):
        nxt = 1 - slot
        wu_copy(f + 1, nxt).wait()
        wd_copy(f, slot).wait()
        up_tile(nxt)
        down_tile(slot, first=isinstance(f, int) and f == 0)
        if isinstance(f, int):
            if f + 3 < nf:
                wu_copy(f + 3, nxt).start()
            if f + 2 < nf:
                wd_copy(f + 2, slot).start()
        else:
            @pl.when(f + 3 < nf)
            def _():
                wu_copy(f + 3, nxt).start()

            @pl.when(f + 2 < nf)
            def _():
                wd_copy(f + 2, slot).start()

    wu_copy(0, 0).wait()
    up_tile(0)
    wu_copy(2, 0).start()

    trip(0, 0)
    assert nf % 2 == 0

    def trip_pair(p, carry):
        trip(2 * p + 1, 1)
        trip(2 * p + 2, 0)
        return carry

    lax.fori_loop(0, (nf - 2) // 2, trip_pair, 0)

    def finish2(rs):
        x2 = _deepnorm_ln(xbuf[rs, :], _row(mod_ref, GATE2), acc[rs, :]) * g2_ref[...] + b2_ref[...]
        xbuf[rs, :] = x2
        if emit_next:
            hn_ref[rs, :] = _modulated_ln(x2, modn_ref, SHIFT1, SCALE1)

    wd_copy(nf - 1, (nf - 1) % 2).wait()
    down_tile((nf - 1) % 2)

    @pl.when(jnp.logical_not(last))
    def _():
        start_first_tiles()

    _for_row_chunks(TAIL_TM, finish2)
    out_copy().start()

    @pl.when(last)
    def _():
        out_copy().wait()


def _tail(merged, x, mod, w_o, w_up, w_down, g1, b1, g2, b2, layer, emit_next):
    nb, rows, _ = x.shape
    tm = TAIL_TM
    row_blk = pl.BlockSpec((None, tm, D_MODEL), lambda b_, i: (b_, i, 0))
    vec = pl.BlockSpec((None, 1, D_MODEL), lambda b_, i: (layer, 0, 0))
    hbm = pl.BlockSpec(memory_space=pl.ANY)
    in_specs = [row_blk, hbm, hbm, hbm, hbm, _mod_spec(layer, nb)]
    args = [merged, x, w_o, w_up, w_down, mod]
    if emit_next:
        in_specs.append(_mod_spec(layer + 1, nb))
        args.append(mod)
    in_specs += [vec, vec, vec, vec]
    args += [v.reshape(DEPTH, 1, D_MODEL) for v in (g1, b1, g2, b2)]
    out_specs = [hbm]
    out_shape = [jax.ShapeDtypeStruct(x.shape, F32)]
    if emit_next:
        out_specs.append(row_blk)
        out_shape.append(jax.ShapeDtypeStruct(x.shape, BF16))
    out = pl.pallas_call(
        functools.partial(_tail_kernel, layer, emit_next),
        grid=(nb, rows // tm),
        in_specs=in_specs,
        out_specs=out_specs,
        out_shape=out_shape,
        scratch_shapes=[
            pltpu.VMEM((tm, D_MODEL), F32),
            pltpu.VMEM((tm, D_MODEL), F32),
            pltpu.VMEM((tm, D_MODEL), BF16),
            pltpu.VMEM((2, tm, TAIL_TF), BF16),
            pltpu.VMEM((2, TAIL_TK, D_MODEL), F32),
            pltpu.VMEM((2, D_MODEL, TAIL_TF), F32),
            pltpu.SemaphoreType.DMA((2,)),
            pltpu.SemaphoreType.DMA((2,)),
            pltpu.SemaphoreType.DMA((1,)),
            pltpu.SemaphoreType.DMA((1,)),
        ],
        compiler_params=_params(("arbitrary", "arbitrary"), 58),
        name="layer_tail",
    )(*args)
    return out if emit_next else (out[0], None)


def _pos_table():
    quarter = D_MODEL // 4
    omega = 1.0 / (POS_BASE ** (np.arange(quarter, dtype=np.float64) / quarter))
    t = np.arange(SEQ)
    ar = (t // GRID_W).astype(np.float64)[:, None] * omega
    ac = (t % GRID_W).astype(np.float64)[:, None] * omega
    return np.concatenate([np.sin(ar), np.cos(ar), np.sin(ac), np.cos(ac)], axis=-1).astype(np.float32)


_POS = _pos_table()


def _mod_tables(m):
    m = m.reshape(DEPTH, SUBLANES, 6, D_MODEL)
    m = jnp.pad(m, ((0, 0), (0, 0), (0, SUBLANES - 6), (0, 0)))
    return m[:, :BATCH], m[:, BATCH:BATCH + 1]


def _as_batch(a):
    return a.reshape(BATCH, CTX_LEN, a.shape[-1])


def _as_slab(a):
    return a.reshape(1, BATCH * CTX_LEN, a.shape[-1])


def kernel(x, c, ctx, c_ctx, w_mod, b_mod, w_in, lam_re, lam_im, log_step, ssm_b_re, ssm_b_im, ssm_c_re,
           ssm_c_im, d_skip, w_glu, w_ps, w_pf, w_o, ln1_g, ln1_b, w_up, w_down, ln2_g, ln2_b):
    cond8 = jnp.concatenate([c, c_ctx[None], jnp.zeros((SUBLANES - BATCH - 1, D_MODEL), F32)], axis=0)
    mod_lat, mod_ctx = _mod_tables(_modulation(cond8, w_mod, b_mod))

    x_lat, h_lat = _entry(x, jnp.asarray(_POS), mod_lat)
    x_ctx, h_ctx = _entry(_as_slab(ctx), None, mod_ctx)
    h_zero = jnp.zeros((2, SUBLANES, STATE_W), F32)
    prm = jax.vmap(_s5_params)(lam_re, lam_im, log_step, ssm_b_re, ssm_b_im, ssm_c_re, ssm_c_im)
    w_ps_b, w_pf_b = w_ps.astype(BF16), w_pf.astype(BF16)

    for l in range(DEPTH):
        need_ctx = l < DEPTH - 1

        def mixer_tail(us, rest, yf, yb, xx, mod, as_batch, as_rows):
            yq = as_rows(_fourier_mix(as_batch(rest)))
            s = _glu(us, as_rows(yf), as_rows(yb), d_skip, w_glu, l)
            merged = _merge(s, yq, rest, w_ps_b, w_pf_b, l)
            return _tail(merged, xx, mod, w_o, w_up, w_down, ln1_g, ln1_b, ln2_g, ln2_b, l, need_ctx)

        us_lat, rest_lat = _in_proj(h_lat, w_in, l, True)
        if need_ctx:
            us_ctx, rest_ctx = _in_proj(h_ctx, w_in, l, True)
            yf_c, yb_c, h_t = _s5(_as_batch(us_ctx), h_zero, prm, l, True)
        else:
            (us_ctx,) = _in_proj(h_ctx, w_in, l, False)
            (h_t,) = _s5(_as_batch(us_ctx), h_zero, prm, l, False)
        yf, yb, _ = _s5(us_lat, h_t, prm, l, True)
        ident = lambda a: a
        x_lat, h_lat = mixer_tail(us_lat, rest_lat, yf, yb, x_lat, mod_lat, ident, ident)
        if need_ctx:
            x_ctx, h_ctx = mixer_tail(us_ctx, rest_ctx, yf_c, yb_c, x_ctx, mod_ctx, _as_batch, _as_slab)

    return x_lat
```
